```python
import math
import jax, jax.numpy as jnp
from jax import lax
import numpy as np

D_MODEL = 1024
BATCH = 8
SEQ = 2048
DEPTH = 4
DEC_BATCH = 128
DEC_SEQ = 4
PAST_LEN = 16384
PAGE_SIZE = 128

N_HEADS_M = 4
HEAD_DIM_M = 256
D_MLSTM = N_HEADS_M * HEAD_DIM_M
MLSTM_CHUNK = 64
POOL_WINDOWS = (2, 4, 8, 16)
N_POOL_GROUPS = 4
POOL_GROUP_DIM = 128
D_POOL = N_POOL_GROUPS * POOL_GROUP_DIM
POOL_BUF = 15
N_EXPERTS = 16
N_EXPERT_GROUPS = 4
EXPERTS_PER_GROUP = N_EXPERTS // N_EXPERT_GROUPS
TOP_K = 2
D_EXPERT = 512
N_IN = 4 * D_MLSTM + 2 * N_HEADS_M + D_POOL + 2 * D_MODEL
ALPHA = (2 * DEPTH) ** 0.25
BETA = (8 * DEPTH) ** -0.25
LN_EPS = 1e-5

kernel_name = "xlstm_pool_moe_hybrid_step"


def _layer_norm(x, g, b):
    xf = x.astype(jnp.float32)
    mu = jnp.mean(xf, -1, keepdims=True)
    var = jnp.mean(jnp.square(xf - mu), -1, keepdims=True)
    y = (xf - mu) * lax.rsqrt(var + LN_EPS) * g.astype(jnp.float32) + b.astype(jnp.float32)
    return y.astype(x.dtype)


def _mlstm_chunk(carry, inp):
    c_prev, n_prev, m_prev = carry
    q, k, v, li, lf = inp
    L = q.shape[2]
    b = jnp.cumsum(lf, axis=-1)
    causal = jnp.tril(jnp.ones((L, L), dtype=bool))
    d = jnp.where(causal, b[..., :, None] - b[..., None, :] + li[..., None, :], -jnp.inf)
    inter = b + m_prev[..., None]
    m_t = jnp.maximum(inter, jnp.max(d, -1))
    s_inter = jnp.exp(inter - m_t)
    qk = jnp.einsum('bhtd,bhsd->bhts', q, k) * jnp.exp(d - m_t[..., None])
    num = s_inter[..., None] * jnp.einsum('bhtd,bhde->bhte', q, c_prev) + jnp.einsum('bhts,bhse->bhte', qk, v)
    den = s_inter * jnp.einsum('bhtd,bhd->bht', q, n_prev) + jnp.sum(qk, -1)
    h = num / jnp.maximum(jnp.abs(den), jnp.exp(-m_t))[..., None]
    b_last = b[..., -1]
    d_last = b_last[..., None] - b + li
    inter_last = b_last + m_prev
    m_new = jnp.maximum(inter_last, jnp.max(d_last, -1))
    w_last = jnp.exp(d_last - m_new[..., None])
    s_last = jnp.exp(inter_last - m_new)
    c_new = s_last[..., None, None] * c_prev + jnp.einsum('bhs,bhsd,bhse->bhde', w_last, k, v)
    n_new = s_last[..., None] * n_prev + jnp.einsum('bhs,bhsd->bhd', w_last, k)
    return (c_new, n_new, m_new), h


def _to_chunks(a, n_chunks, chunk):
    a = jnp.moveaxis(a, 2, 1)
    a = a.reshape(a.shape[:2] + (n_chunks, chunk) + a.shape[3:])
    return jnp.moveaxis(a, 2, 0)


def _mlstm(q, k, v, li, lf, c0, n0, m0):
    B, T, H, _ = q.shape
    chunk = math.gcd(T, MLSTM_CHUNK)
    nc = T // chunk
    xs = (_to_chunks(q, nc, chunk), _to_chunks(k, nc, chunk), _to_chunks(v, nc, chunk),
          _to_chunks(li, nc, chunk), _to_chunks(lf, nc, chunk))
    (c1, n1, m1), h = lax.scan(_mlstm_chunk, (c0, n0, m0), xs)
    h = jnp.moveaxis(h, 0, 2).reshape(B, H, T, HEAD_DIM_M)
    return jnp.swapaxes(h, 1, 2), c1, n1, m1


def _causal_pool(u, buf, start):
    B, T, _ = u.shape
    ext = jnp.concatenate([buf.astype(jnp.float32), u.astype(jnp.float32)], axis=1)
    csum = jnp.concatenate([jnp.zeros((B, 1, D_POOL), jnp.float32), jnp.cumsum(ext, axis=1)], axis=1)
    uf = ext[:, POOL_BUF:]
    pos = start + jnp.arange(T)
    off = POOL_BUF + 1
    groups = []
    for g, w in enumerate(POOL_WINDOWS):
        sl = slice(g * POOL_GROUP_DIM, (g + 1) * POOL_GROUP_DIM)
        wsum = csum[:, off:off + T, sl] - csum[:, off - w:off - w + T, sl]
        cnt = jnp.minimum(pos + 1, w).astype(jnp.float32)
        groups.append(wsum / cnt[None, :, None] - uf[..., sl])
    return jnp.concatenate(groups, axis=-1), ext[:, -POOL_BUF:]


def _moe(x, w_router, b_router, w_e_gate, w_e_up, w_e_down):
    B, T, D = x.shape
    xt = x.reshape(B * T, D)
    probs = jax.nn.softmax((xt @ w_router).astype(jnp.float32) + b_router.astype(jnp.float32), axis=-1)
    gscore = jnp.sum(lax.top_k(probs.reshape(-1, N_EXPERT_GROUPS, EXPERTS_PER_GROUP), TOP_K)[0], -1)
    gsel = jnp.argmax(gscore, axis=-1)
    in_group = (jnp.arange(N_EXPERTS) // EXPERTS_PER_GROUP)[None, :] == gsel[:, None]
    vals, idx = lax.top_k(jnp.where(in_group, probs, -1.0), TOP_K)
    wts = vals / jnp.sum(vals, -1, keepdims=True)
    gate = jnp.sum(jax.nn.one_hot(idx, N_EXPERTS, dtype=jnp.float32) * wts[..., None], axis=1)
    h = jax.nn.silu(jnp.einsum('nd,edf->nef', xt, w_e_gate)) * jnp.einsum('nd,edf->nef', xt, w_e_up)
    h = h * gate[..., None].astype(h.dtype)
    return jnp.einsum('nef,efd->nd', h, w_e_down).reshape(B, T, D)


def _layer(x, c0, n0, m0, buf, start, w_in, b_gate, hn_gain, w_pool, pool_scale, w_proj_a, w_proj_b,
           w_out, ln1_g, ln1_b, ln2_g, ln2_b, w_router, b_router, w_e_gate, w_e_up, w_e_down):
    B, T, _ = x.shape
    f32 = jnp.float32
    z = x @ w_in
    cuts = [D_MLSTM, 2 * D_MLSTM, 3 * D_MLSTM, 4 * D_MLSTM, 4 * D_MLSTM + N_HEADS_M,
            4 * D_MLSTM + 2 * N_HEADS_M, 4 * D_MLSTM + 2 * N_HEADS_M + D_POOL,
            4 * D_MLSTM + 2 * N_HEADS_M + D_POOL + D_MODEL]
    q, k, v, o, gi, gf, u, ga, gb = jnp.split(z, cuts, axis=-1)
    q = q.astype(f32).reshape(B, T, N_HEADS_M, HEAD_DIM_M)
    k = k.astype(f32).reshape(B, T, N_HEADS_M, HEAD_DIM_M) * HEAD_DIM_M ** -0.5
    v = v.astype(f32).reshape(B, T, N_HEADS_M, HEAD_DIM_M)
    li = gi.astype(f32) + b_gate[:N_HEADS_M].astype(f32)
    lf = jax.nn.log_sigmoid(gf.astype(f32) + b_gate[N_HEADS_M:].astype(f32))
    h, c1, n1, m1 = _mlstm(q, k, v, li, lf, c0.astype(f32), n0.astype(f32), m0.astype(f32))
    mu = jnp.mean(h, -1, keepdims=True)
    var = jnp.mean(jnp.square(h - mu), -1, keepdims=True)
    h = (h - mu) * lax.rsqrt(var + LN_EPS) * hn_gain.astype(f32)
    y_a = (jax.nn.sigmoid(o) * h.reshape(B, T, D_MLSTM).astype(x.dtype)) @ w_proj_a
    pooled, buf_new = _causal_pool(u, buf, start)
    pooled = jnp.einsum('btgc,gce->btge', pooled.reshape(B, T, N_POOL_GROUPS, POOL_GROUP_DIM).astype(x.dtype), w_pool)
    y_b = (pooled.reshape(B, T, D_POOL) * pool_scale) @ w_proj_b
    mix = jax.nn.sigmoid(ga) * y_a + jax.nn.sigmoid(gb) * y_b
    x = _layer_norm(ALPHA * x + mix @ w_out, ln1_g, ln1_b)
    x = _layer_norm(ALPHA * x + _moe(x, w_router, b_router, w_e_gate, w_e_up, w_e_down), ln2_g, ln2_b)
    return x, c1, n1, m1, buf_new


def _trunk(x, c_all, n_all, m_all, buf_all, start, w_in, b_gate, hn_gain, w_pool, pool_scale, w_proj_a,
           w_proj_b, w_out, ln1_g, ln1_b, ln2_g, ln2_b, w_router, b_router, w_e_gate, w_e_up, w_e_down):
    cs, ns, ms, bs = [], [], [], []
    for l in range(DEPTH):
        x, c1, n1, m1, b1 = _layer(x, c_all[l], n_all[l], m_all[l], buf_all[l], start, w_in[l], b_gate[l],
                                   hn_gain[l], w_pool[l], pool_scale[l], w_proj_a[l], w_proj_b[l], w_out[l],
                                   ln1_g[l], ln1_b[l], ln2_g[l], ln2_b[l], w_router, b_router,
                                   w_e_gate[l], w_e_up[l], w_e_down[l])
        cs.append(c1)
        ns.append(n1)
        ms.append(m1)
        bs.append(b1)
    return x, jnp.stack(cs), jnp.stack(ns), jnp.stack(ms), jnp.stack(bs)


def setup_inputs(seed: int = 0) -> dict:
    key = jax.random.key(seed)
    ks = jax.random.split(key, 24)
    nrm = jax.random.normal
    H, dh = N_HEADS_M, HEAD_DIM_M
    forget_bias = jnp.linspace(3.0, 6.0, H)[None, :] + 0.1 * nrm(ks[7], (DEPTH, H))
    input_bias = 0.1 * nrm(ks[8], (DEPTH, H))
    return {
        "x_prompt": nrm(ks[0], (BATCH, SEQ, D_MODEL), jnp.float32),
        "x_sample": nrm(ks[1], (DEC_BATCH, DEC_SEQ, D_MODEL), jnp.float32),
        "state_C": nrm(ks[2], (DEPTH, DEC_BATCH, H, dh, dh), jnp.float32) * dh ** -0.5,
        "state_n": 0.5 * nrm(ks[3], (DEPTH, DEC_BATCH, H, dh), jnp.float32),
        "state_m": 0.5 * nrm(ks[4], (DEPTH, DEC_BATCH, H), jnp.float32),
        "state_pool": nrm(ks[5], (DEPTH, DEC_BATCH, POOL_BUF, D_POOL), jnp.float32),
        "w_in": nrm(ks[6], (DEPTH, D_MODEL, N_IN), jnp.float32) * D_MODEL ** -0.5,
        "b_gate": jnp.concatenate([input_bias, forget_bias], axis=-1).astype(jnp.float32),
        "hn_gain": 1.0 + 0.02 * nrm(ks[9], (DEPTH, H, dh), jnp.float32),
        "w_pool": nrm(ks[10], (DEPTH, N_POOL_GROUPS, POOL_GROUP_DIM, POOL_GROUP_DIM), jnp.float32) * POOL_GROUP_DIM ** -0.5,
        "pool_scale": 1.0 + 0.1 * nrm(ks[11], (DEPTH, D_POOL), jnp.float32),
        "w_proj_a": nrm(ks[12], (DEPTH, D_MLSTM, D_MODEL), jnp.float32) * D_MLSTM ** -0.5,
        "w_proj_b": nrm(ks[13], (DEPTH, D_POOL, D_MODEL), jnp.float32) * D_POOL ** -0.5,
        "w_out": nrm(ks[14], (DEPTH, D_MODEL, D_MODEL), jnp.float32) * (BETA * D_MODEL ** -0.5),
        "ln1_g": 1.0 + 0.02 * nrm(ks[15], (DEPTH, D_MODEL), jnp.float32),
        "ln1_b": 0.02 * nrm(ks[16], (DEPTH, D_MODEL), jnp.float32),
        "ln2_g": 1.0 + 0.02 * nrm(ks[17], (DEPTH, D_MODEL), jnp.float32),
        "ln2_b": 0.02 * nrm(ks[18], (DEPTH, D_MODEL), jnp.float32),
        "w_router": nrm(ks[19], (D_MODEL, N_EXPERTS), jnp.float32) * D_MODEL ** -0.5,
        "b_router": 0.01 * nrm(ks[20], (N_EXPERTS,), jnp.float32),
        "w_e_gate": nrm(ks[21], (DEPTH, N_EXPERTS, D_MODEL, D_EXPERT), jnp.float32) * D_MODEL ** -0.5,
        "w_e_up": nrm(ks[22], (DEPTH, N_EXPERTS, D_MODEL, D_EXPERT), jnp.float32) * D_MODEL ** -0.5,
        "w_e_down": nrm(ks[23], (DEPTH, N_EXPERTS, D_EXPERT, D_MODEL), jnp.float32) * (BETA * D_EXPERT ** -0.5),
    }


def reference(x_prompt, x_sample, state_C, state_n, state_m, state_pool, w_in, b_gate, hn_gain, w_pool,
              pool_scale, w_proj_a, w_proj_b, w_out, ln1_g, ln1_b, ln2_g, ln2_b, w_router, b_router,
              w_e_gate, w_e_up, w_e_down):
    f32 = jnp.float32
    bp = x_prompt.shape[0]
    c0 = jnp.zeros((DEPTH, bp, N_HEADS_M, HEAD_DIM_M, HEAD_DIM_M), f32)
    n0 = jnp.zeros((DEPTH, bp, N_HEADS_M, HEAD_DIM_M), f32)
    m0 = jnp.zeros((DEPTH, bp, N_HEADS_M), f32)
    buf0 = jnp.zeros((DEPTH, bp, POOL_BUF, D_POOL), f32)
    y_prompt, c_p, n_p, m_p, pool_p = _trunk(
        x_prompt, c0, n0, m0, buf0, 0, w_in, b_gate, hn_gain, w_pool, pool_scale, w_proj_a, w_proj_b,
        w_out, ln1_g, ln1_b, ln2_g, ln2_b, w_router, b_router, w_e_gate, w_e_up, w_e_down)
    y_sample, c_s, n_s, m_s, pool_s = _trunk(
        x_sample, state_C, state_n, state_m, state_pool, PAST_LEN, w_in, b_gate, hn_gain, w_pool, pool_scale,
        w_proj_a, w_proj_b, w_out, ln1_g, ln1_b, ln2_g, ln2_b, w_router, b_router, w_e_gate, w_e_up, w_e_down)
    return (y_prompt, y_sample,
            c_p.astype(state_C.dtype), n_p.astype(state_n.dtype), m_p.astype(state_m.dtype), pool_p.astype(state_pool.dtype),
            c_s.astype(state_C.dtype), n_s.astype(state_n.dtype), m_s.astype(state_m.dtype), pool_s.astype(state_pool.dtype))
```

```python
import functools
import math

import jax
import jax.numpy as jnp
from jax import lax
from jax.experimental import pallas as pl
from jax.experimental.pallas import tpu as pltpu

F32 = jnp.float32
BF16 = jnp.bfloat16

D_MODEL = 1024
N_HEADS = 4
HEAD_DIM = 256
D_MLSTM = N_HEADS * HEAD_DIM
POOL_WINDOWS = (2, 4, 8, 16)
POOL_GROUP_DIM = 128
D_POOL = len(POOL_WINDOWS) * POOL_GROUP_DIM
POOL_BUF = 15
N_EXPERTS = 16
N_EXPERT_GROUPS = 4
EXPERTS_PER_GROUP = 4
D_EXPERT = 512
DEPTH = 4
PAST_LEN = 16384
ALPHA = (2 * DEPTH) ** 0.25
LN_EPS = 1e-5
K_SCALE = HEAD_DIM ** -0.5

LANES = 128
GATE_ROWS = 16
ROUTER_ROWS = 32
VMEM_LIMIT = 52 * 1024 * 1024
MLSTM_PROMPT_CHUNK = 256
SAMPLE_BLOCK_ROWS = 128

_Q0, _K0, _V0, _O0 = 0, D_MLSTM, 2 * D_MLSTM, 3 * D_MLSTM
_G0 = 4 * D_MLSTM
_U0 = _G0 + 2 * N_HEADS
_GA0 = _U0 + D_POOL
_GB0 = _GA0 + D_MODEL
_N_IN = _GB0 + D_MODEL
_MAIN_SEGS = ((0, 1024), (1024, 2048), (2048, 3072), (3072, 4096), (4096, 4608), (4608, 5632), (5632, 6656))


def _params(sem):
    return pltpu.CompilerParams(dimension_semantics=sem, vmem_limit_bytes=VMEM_LIMIT)


def _const_spec(shape):
    nd = len(shape)
    return pl.BlockSpec(shape, lambda *_: (0,) * nd, pipeline_mode=pl.Buffered(1))


def _row_tile(n, cap=512):
    t = cap
    while n % t:
        t //= 2
    return t


def _inproj_kernel(x_ref, w_ref, wg_ref, wgt_ref, brow_ref, bcol_ref,
                   q_ref, k_ref, v_ref, o_ref, u_ref, ga_ref, gb_ref, gcol_ref, grow_ref):
    x = x_ref[...]

    def seg(i):
        lo, hi = _MAIN_SEGS[i]
        return jnp.dot(x, w_ref[:, lo:hi], preferred_element_type=F32)

    q_ref[...] = seg(0).astype(BF16)
    k_ref[...] = (seg(1) * K_SCALE).astype(BF16)
    v_ref[...] = seg(2).astype(BF16)
    o_ref[...] = seg(3)
    u_ref[...] = seg(4)
    ga_ref[...] = seg(5)
    gb_ref[...] = seg(6)
    g = jnp.dot(x, wg_ref[...], preferred_element_type=F32) + brow_ref[...]
    lane = lax.broadcasted_iota(jnp.int32, g.shape, 1)
    gcol_ref[...] = jnp.where(lane < N_HEADS, g, jax.nn.log_sigmoid(g))
    gt = lax.dot_general(wgt_ref[...], x, (((1,), (1,)), ((), ())), preferred_element_type=F32) + bcol_ref[...]
    sub = lax.broadcasted_iota(jnp.int32, gt.shape, 0)
    grow_ref[...] = jnp.where(sub < N_HEADS, gt, jax.nn.log_sigmoid(gt))


def _inproj(xb, w_main, w_gate, w_gate_t, b_row, b_col):
    n = xb.shape[0]
    tm = _row_tile(n)
    row = lambda w: pl.BlockSpec((tm, w), lambda i: (i, 0))
    out_shape = (
        jax.ShapeDtypeStruct((n, D_MLSTM), BF16), jax.ShapeDtypeStruct((n, D_MLSTM), BF16),
        jax.ShapeDtypeStruct((n, D_MLSTM), BF16), jax.ShapeDtypeStruct((n, D_MLSTM), F32),
        jax.ShapeDtypeStruct((n, D_POOL), F32), jax.ShapeDtypeStruct((n, D_MODEL), F32),
        jax.ShapeDtypeStruct((n, D_MODEL), F32), jax.ShapeDtypeStruct((n, LANES), F32),
        jax.ShapeDtypeStruct((GATE_ROWS, n), F32))
    return pl.pallas_call(
        _inproj_kernel,
        grid=(n // tm,),
        in_specs=[row(D_MODEL), _const_spec(w_main.shape), _const_spec(w_gate.shape),
                  _const_spec(w_gate_t.shape), _const_spec(b_row.shape), _const_spec(b_col.shape)],
        out_specs=(row(D_MLSTM), row(D_MLSTM), row(D_MLSTM), row(D_MLSTM), row(D_POOL), row(D_MODEL),
                   row(D_MODEL), row(LANES), pl.BlockSpec((GATE_ROWS, tm), lambda i: (0, i))),
        out_shape=out_shape,
        compiler_params=_params(("parallel",)),
        name="inproj",
    )(xb, w_main, w_gate, w_gate_t, b_row, b_col)


def _mlstm_kernel(*refs, rows, seq_rows, streamed):
    if streamed:
        (q_ref, k_ref, v_ref, gcol_ref, grow_ref, o_ref, gain_ref, c0_ref, n0_ref, m0_ref,
         hg_ref, c1_ref, n1_ref, m1_ref) = refs
    else:
        (q_ref, k_ref, v_ref, gcol_ref, grow_ref, o_ref, gain_ref,
         hg_ref, c1_ref, n1_ref, m1_ref) = refs
        c0_ref, n0_ref, m0_ref = c1_ref, n1_ref, m1_ref
    step = pl.program_id(1)
    t_idx = lax.broadcasted_iota(jnp.int32, (rows, 1), 0)
    s_idx = lax.broadcasted_iota(jnp.int32, (1, rows), 1)
    if streamed:
        shift = int(math.log2(seq_rows))
        rowm = jnp.right_shift(t_idx, shift) == step
        colm = jnp.right_shift(s_idx, shift) == step
        mask = (s_idx <= t_idx) & rowm & colm
        mask_t = (t_idx <= s_idx) & rowm & colm

        @pl.when(step == 0)
        def _():
            hg_ref[...] = jnp.zeros_like(hg_ref)
    else:
        mask = s_idx <= t_idx
        mask_t = t_idx <= s_idx

        @pl.when(step == 0)
        def _():
            c1_ref[...] = jnp.zeros_like(c1_ref)
            n1_ref[...] = jnp.zeros_like(n1_ref)
            m1_ref[...] = jnp.zeros_like(m1_ref)

    gcol = gcol_ref[...]
    grow = grow_ref[...]
    neg_inf = jnp.float32(-jnp.inf)
    for h in range(N_HEADS):
        sl = slice(h * HEAD_DIM, (h + 1) * HEAD_DIM)
        q = q_ref[:, sl]
        k = k_ref[:, sl]
        v = v_ref[:, sl]
        li_r = grow[h:h + 1, :]
        lf_r = grow[N_HEADS + h:N_HEADS + h + 1, :]
        li_c = gcol[:, h:h + 1]
        lf_c = gcol[:, N_HEADS + h:N_HEADS + h + 1]
        m_prev = m0_ref[0, h:h + 1, 0:1]
        c_prev = c0_ref[0, h]
        n_prev = n0_ref[0, h:h + 1, :]
        b_c = jnp.sum(jnp.where(mask, lf_r, 0.0), axis=1, keepdims=True)
        b_r = jnp.sum(jnp.where(mask_t, lf_c, 0.0), axis=0, keepdims=True)
        dmat = jnp.where(mask, b_c - b_r + li_r, neg_inf)
        inter = b_c + m_prev
        m_t = jnp.maximum(inter, jnp.max(dmat, axis=1, keepdims=True))
        s_inter = jnp.exp(inter - m_t)
        qk = lax.dot_general(q, k, (((1,), (1,)), ((), ())), preferred_element_type=F32)
        s = qk * jnp.exp(dmat - m_t)
        num = (s_inter * jnp.dot(q, c_prev.astype(BF16), preferred_element_type=F32)
               + jnp.dot(s.astype(BF16), v, preferred_element_type=F32))
        den = (s_inter * jnp.sum(q.astype(F32) * n_prev, axis=1, keepdims=True)
               + jnp.sum(s, axis=1, keepdims=True))
        hval = num / jnp.maximum(jnp.abs(den), jnp.exp(-m_t))
        if streamed:
            b_last = jnp.sum(jnp.where(colm, lf_r, 0.0), axis=1, keepdims=True)
            d_last = jnp.where(rowm, b_last - b_c + li_c, neg_inf)
        else:
            b_last = jnp.sum(lf_r, axis=1, keepdims=True)
            d_last = b_last - b_c + li_c
        m_new = jnp.maximum(b_last + m_prev, jnp.max(d_last, axis=0, keepdims=True))
        kw = k.astype(F32) * jnp.exp(d_last - m_new)
        s_last = jnp.exp(b_last + m_prev - m_new)
        upd = lax.dot_general(kw.astype(BF16), v, (((0,), (0,)), ((), ())), preferred_element_type=F32)
        c1_ref[0, h] = s_last * c_prev + upd
        n1_ref[0, h:h + 1, :] = s_last * n_prev + jnp.sum(kw, axis=0, keepdims=True)
        m1_ref[0, h:h + 1, :] = jnp.broadcast_to(m_new, (1, LANES))
        mu = jnp.mean(hval, axis=1, keepdims=True)
        xc = hval - mu
        var = jnp.mean(xc * xc, axis=1, keepdims=True)
        hn = xc * lax.rsqrt(var + LN_EPS) * gain_ref[:, sl]
        out = (jax.nn.sigmoid(o_ref[:, sl]) * hn).astype(BF16)
        if streamed:
            hg_ref[:, sl] = jnp.where(rowm, out, hg_ref[:, sl])
        else:
            hg_ref[:, sl] = out


def _state_shapes(n_seq):
    return (jax.ShapeDtypeStruct((n_seq, N_HEADS, HEAD_DIM, HEAD_DIM), F32),
            jax.ShapeDtypeStruct((n_seq, N_HEADS, HEAD_DIM), F32),
            jax.ShapeDtypeStruct((n_seq, N_HEADS, LANES), F32))


def _mlstm_prompt(q, k, v, gcol, grow, o, gain, n_seq, seq_len):
    n = q.shape[0]
    chunk = math.gcd(seq_len, MLSTM_PROMPT_CHUNK)
    nc = seq_len // chunk
    row = lambda w: pl.BlockSpec((chunk, w), lambda b, c: (b * nc + c, 0))
    st = lambda *tail: pl.BlockSpec((1,) + tail, lambda b, c: (b,) + (0,) * len(tail))
    return pl.pallas_call(
        functools.partial(_mlstm_kernel, rows=chunk, seq_rows=seq_len, streamed=False),
        grid=(n_seq, nc),
        in_specs=[row(D_MLSTM), row(D_MLSTM), row(D_MLSTM), row(LANES),
                  pl.BlockSpec((GATE_ROWS, chunk), lambda b, c: (0, b * nc + c)),
                  row(D_MLSTM), _const_spec(gain.shape)],
        out_specs=(row(D_MLSTM), st(N_HEADS, HEAD_DIM, HEAD_DIM), st(N_HEADS, HEAD_DIM), st(N_HEADS, LANES)),
        out_shape=(jax.ShapeDtypeStruct((n, D_MLSTM), BF16),) + _state_shapes(n_seq),
        compiler_params=_params(("parallel", "arbitrary")),
        name="mlstm_prompt",
    )(q, k, v, gcol, grow, o, gain)


def _mlstm_sample(q, k, v, gcol, grow, o, gain, c0, n0, m0, seq_len):
    n = q.shape[0]
    n_seq = n // seq_len
    rows = min(SAMPLE_BLOCK_ROWS, n)
    per_block = rows // seq_len
    row = lambda w: pl.BlockSpec((rows, w), lambda i, j: (i, 0))
    st = lambda *tail: pl.BlockSpec((1,) + tail, lambda i, j: (i * per_block + j,) + (0,) * len(tail))
    state_specs = (st(N_HEADS, HEAD_DIM, HEAD_DIM), st(N_HEADS, HEAD_DIM), st(N_HEADS, LANES))
    return pl.pallas_call(
        functools.partial(_mlstm_kernel, rows=rows, seq_rows=seq_len, streamed=True),
        grid=(n // rows, per_block),
        in_specs=[row(D_MLSTM), row(D_MLSTM), row(D_MLSTM), row(LANES),
                  pl.BlockSpec((GATE_ROWS, rows), lambda i, j: (0, i)),
                  row(D_MLSTM), _const_spec(gain.shape)] + list(state_specs),
        out_specs=(row(D_MLSTM),) + state_specs,
        out_shape=(jax.ShapeDtypeStruct((n, D_MLSTM), BF16),) + _state_shapes(n_seq),
        compiler_params=_params(("parallel", "arbitrary")),
        name="mlstm_sample",
    )(q, k, v, gcol, grow, o, gain, c0, n0, m0)


def _pool_prompt_kernel(u_ref, prev_ref, out_ref, ext_ref, *, tm, tiles_per_seq):
    tile = pl.program_id(0) % tiles_per_seq
    head = 16
    ext_ref[0:head, :] = jnp.where(tile == 0, 0.0, prev_ref[...])
    ext_ref[head:, :] = u_ref[...]
    pos = tile * tm + lax.broadcasted_iota(jnp.int32, (tm, 1), 0)
    for g, w in enumerate(POOL_WINDOWS):
        sl = slice(g * POOL_GROUP_DIM, (g + 1) * POOL_GROUP_DIM)
        acc = ext_ref[head:head + tm, sl]
        for d in range(1, w):
            acc = acc + ext_ref[head - d:head - d + tm, sl]
        cnt = jnp.minimum(pos + 1, w).astype(F32)
        out_ref[:, sl] = (acc / cnt - u_ref[:, sl]).astype(BF16)


def _pool_prompt(u, seq_len):
    n = u.shape[0]
    tm = _row_tile(seq_len)
    head = 16
    return pl.pallas_call(
        functools.partial(_pool_prompt_kernel, tm=tm, tiles_per_seq=seq_len // tm),
        grid=(n // tm,),
        in_specs=[pl.BlockSpec((tm, D_POOL), lambda i: (i, 0)),
                  pl.BlockSpec((head, D_POOL), lambda i: (jnp.maximum(i * (tm // head) - 1, 0), 0))],
        out_specs=pl.BlockSpec((tm, D_POOL), lambda i: (i, 0)),
        out_shape=jax.ShapeDtypeStruct((n, D_POOL), BF16),
        scratch_shapes=[pltpu.VMEM((head + tm, D_POOL), F32)],
        compiler_params=_params(("parallel",)),
        name="pool_prompt",
    )(u, u)


def _pool_sample_kernel(u_ref, buf_ref, out_ref, nbuf_ref, ext_ref, *, seq_len, start):
    ext_ref[:, 0:POOL_BUF, :] = buf_ref[...]
    ext_ref[:, POOL_BUF:POOL_BUF + seq_len, :] = u_ref[...]
    pos = start + lax.broadcasted_iota(jnp.int32, (1, seq_len, 1), 1)
    for g, w in enumerate(POOL_WINDOWS):
        sl = slice(g * POOL_GROUP_DIM, (g + 1) * POOL_GROUP_DIM)
        acc = ext_ref[:, POOL_BUF:POOL_BUF + seq_len, sl]
        for d in range(1, w):
            acc = acc + ext_ref[:, POOL_BUF - d:POOL_BUF - d + seq_len, sl]
        cnt = jnp.minimum(pos + 1, w).astype(F32)
        out_ref[:, :, sl] = (acc / cnt - u_ref[:, :, sl]).astype(BF16)
    nbuf_ref[...] = ext_ref[:, seq_len:seq_len + POOL_BUF, :]


def _pool_sample(u3, buf, start):
    n_seq, seq_len, _ = u3.shape
    bs = _row_tile(n_seq, 32)
    spec = lambda r: pl.BlockSpec((bs, r, D_POOL), lambda i: (i, 0, 0))
    return pl.pallas_call(
        functools.partial(_pool_sample_kernel, seq_len=seq_len, start=start),
        grid=(n_seq // bs,),
        in_specs=[spec(seq_len), spec(POOL_BUF)],
        out_specs=(spec(seq_len), spec(POOL_BUF)),
        out_shape=(jax.ShapeDtypeStruct((n_seq, seq_len, D_POOL), BF16),
                   jax.ShapeDtypeStruct((n_seq, POOL_BUF, D_POOL), F32)),
        scratch_shapes=[pltpu.VMEM((bs, POOL_BUF + seq_len + 5, D_POOL), F32)],
        compiler_params=_params(("parallel",)),
        name="pool_sample",
    )(u3, buf)


def _layer_norm(y, g, b):
    mu = jnp.mean(y, axis=1, keepdims=True)
    yc = y - mu
    var = jnp.mean(yc * yc, axis=1, keepdims=True)
    return yc * lax.rsqrt(var + LN_EPS) * g + b


def _route(logits_t):
    tokens = logits_t.shape[1]
    grp = lax.broadcasted_iota(jnp.int32, (8, tokens), 0)
    live = grp < N_EXPERT_GROUPS
    neg_inf = jnp.float32(-jnp.inf)
    lm = [jnp.where(live, logits_t[8 * m:8 * m + 8, :], neg_inf) for m in range(EXPERTS_PER_GROUP)]
    mx = jnp.max(jnp.maximum(jnp.maximum(lm[0], lm[1]), jnp.maximum(lm[2], lm[3])), axis=0, keepdims=True)
    ex = [jnp.exp(l - mx) for l in lm]
    tot = jnp.sum(ex[0] + ex[1] + ex[2] + ex[3], axis=0, keepdims=True)
    p = [e / tot for e in ex]
    top1 = jnp.maximum(jnp.maximum(p[0], p[1]), jnp.maximum(p[2], p[3]))
    i1 = jnp.where(p[0] == top1, 0, jnp.where(p[1] == top1, 1, jnp.where(p[2] == top1, 2, 3)))
    r = [jnp.where(i1 == m, -1.0, p[m]) for m in range(EXPERTS_PER_GROUP)]
    top2 = jnp.maximum(jnp.maximum(r[0], r[1]), jnp.maximum(r[2], r[3]))
    i2 = jnp.where(r[0] == top2, 0, jnp.where(r[1] == top2, 1, jnp.where(r[2] == top2, 2, 3)))
    gscore = jnp.where(live, top1 + top2, neg_inf)
    gmax = jnp.max(gscore, axis=0, keepdims=True)
    gsel = jnp.min(jnp.where(gscore == gmax, grp, 8), axis=0, keepdims=True)
    chosen = grp == gsel
    tsum = top1 + top2
    w1 = top1 / tsum
    w2 = top2 / tsum
    gates = [jnp.where(chosen & (i1 == m), w1, jnp.where(chosen & (i2 == m), w2, 0.0))
             for m in range(EXPERTS_PER_GROUP)]
    return jnp.concatenate(gates, axis=0)


def _mix_kernel(hg_ref, pooled_ref, ga_ref, gb_ref, x_ref, wpool_ref, pscale_ref, wa_ref, wb_ref, wout_ref,
                g1_ref, b1_ref, wr_ref, br_ref, x1_ref, x1b_ref, gate_ref):
    ya = jnp.dot(hg_ref[...], wa_ref[...], preferred_element_type=F32)
    pooled = pooled_ref[...]
    parts = []
    for g in range(len(POOL_WINDOWS)):
        sl = slice(g * POOL_GROUP_DIM, (g + 1) * POOL_GROUP_DIM)
        parts.append(jnp.dot(pooled[:, sl], wpool_ref[g], preferred_element_type=F32))
    pl_lin = jnp.concatenate(parts, axis=1) * pscale_ref[...]
    yb = jnp.dot(pl_lin.astype(BF16), wb_ref[...], preferred_element_type=F32)
    mix = jax.nn.sigmoid(ga_ref[...]) * ya + jax.nn.sigmoid(gb_ref[...]) * yb
    res = jnp.dot(mix.astype(BF16), wout_ref[...], preferred_element_type=F32)
    x1 = _layer_norm(ALPHA * x_ref[...] + res, g1_ref[...], b1_ref[...])
    x1_ref[...] = x1
    x1b_ref[...] = x1.astype(BF16)
    logits_t = lax.dot_general(wr_ref[...], x1, (((1,), (1,)), ((), ())), preferred_element_type=F32,
                               precision=lax.Precision.HIGHEST) + br_ref[...]
    gates_t = _route(logits_t)
    tokens = gates_t.shape[1]
    padded = jnp.concatenate([gates_t, jnp.zeros((LANES - ROUTER_ROWS, tokens), F32)], axis=0)
    gate_ref[...] = padded.T


def _mix(hg, pooled, ga, gb, x, lw):
    n = x.shape[0]
    tm = _row_tile(n)
    row = lambda w: pl.BlockSpec((tm, w), lambda i: (i, 0))
    consts = (lw["w_pool"], lw["pool_scale"], lw["w_proj_a"], lw["w_proj_b"], lw["w_out"],
              lw["ln1_g"], lw["ln1_b"], lw["w_router_t"], lw["b_router_col"])
    return pl.pallas_call(
        _mix_kernel,
        grid=(n // tm,),
        in_specs=[row(D_MLSTM), row(D_POOL), row(D_MODEL), row(D_MODEL), row(D_MODEL)]
                 + [_const_spec(c.shape) for c in consts],
        out_specs=(row(D_MODEL), row(D_MODEL), row(LANES)),
        out_shape=(jax.ShapeDtypeStruct((n, D_MODEL), F32), jax.ShapeDtypeStruct((n, D_MODEL), BF16),
                   jax.ShapeDtypeStruct((n, LANES), F32)),
        compiler_params=_params(("parallel",)),
        name="mix",
    )(hg, pooled, ga, gb, x, *consts)


def _moe_kernel(x1b_ref, gate_ref, x1_ref, wg_ref, wu_ref, wd_ref, g2_ref, b2_ref, x2_ref, x2b_ref, acc_ref):
    e = pl.program_id(1)

    @pl.when(e == 0)
    def _():
        acc_ref[...] = jnp.zeros_like(acc_ref)

    x = x1b_ref[...]
    g = jnp.dot(x, wg_ref[0], preferred_element_type=F32)
    u = jnp.dot(x, wu_ref[0], preferred_element_type=F32)
    lane = lax.broadcasted_iota(jnp.int32, (1, LANES), 1)
    gate_lane = (e % EXPERTS_PER_GROUP) * 8 + e // EXPERTS_PER_GROUP
    gcol = jnp.sum(jnp.where(lane == gate_lane, gate_ref[...], 0.0), axis=1, keepdims=True)
    hid = (g * jax.nn.sigmoid(g)) * u * gcol
    acc_ref[...] += jnp.dot(hid.astype(BF16), wd_ref[0], preferred_element_type=F32)

    @pl.when(e == N_EXPERTS - 1)
    def _():
        x2 = _layer_norm(ALPHA * x1_ref[...] + acc_ref[...], g2_ref[...], b2_ref[...])
        x2_ref[...] = x2
        x2b_ref[...] = x2.astype(BF16)


def _moe(x1b, gate, x1, lw):
    n = x1.shape[0]
    tm = _row_tile(n)
    row = lambda w: pl.BlockSpec((tm, w), lambda i, e: (i, 0))
    return pl.pallas_call(
        _moe_kernel,
        grid=(n // tm, N_EXPERTS),
        in_specs=[row(D_MODEL), row(LANES), row(D_MODEL),
                  pl.BlockSpec((1, D_MODEL, D_EXPERT), lambda i, e: (e, 0, 0)),
                  pl.BlockSpec((1, D_MODEL, D_EXPERT), lambda i, e: (e, 0, 0)),
                  pl.BlockSpec((1, D_EXPERT, D_MODEL), lambda i, e: (e, 0, 0)),
                  _const_spec(lw["ln2_g"].shape), _const_spec(lw["ln2_b"].shape)],
        out_specs=(row(D_MODEL), row(D_MODEL)),
        out_shape=(jax.ShapeDtypeStruct((n, D_MODEL), F32), jax.ShapeDtypeStruct((n, D_MODEL), BF16)),
        scratch_shapes=[pltpu.VMEM((tm, D_MODEL), F32)],
        compiler_params=_params(("parallel", "arbitrary")),
        name="moe",
    )(x1b, gate, x1, lw["w_e_gate"], lw["w_e_up"], lw["w_e_down"], lw["ln2_g"], lw["ln2_b"])


def _prepare_weights(w_in, b_gate, hn_gain, w_pool, pool_scale, w_proj_a, w_proj_b, w_out, ln1_g, ln1_b,
                     ln2_g, ln2_b, w_router, b_router, w_e_gate, w_e_up, w_e_down):
    w_main = jnp.concatenate([w_in[:, :, :_G0], w_in[:, :, _U0:]], axis=2).astype(BF16)
    w_gate = w_in[:, :, _G0:_U0]
    w_gate_p = jnp.pad(w_gate, ((0, 0), (0, 0), (0, LANES - 2 * N_HEADS))).astype(BF16)
    w_gate_t = jnp.pad(jnp.swapaxes(w_gate, 1, 2), ((0, 0), (0, GATE_ROWS - 2 * N_HEADS), (0, 0))).astype(BF16)
    b_row = jnp.pad(b_gate, ((0, 0), (0, LANES - 2 * N_HEADS)))[:, None, :]
    b_col = jnp.pad(b_gate, ((0, 0), (0, GATE_ROWS - 2 * N_HEADS)))[:, :, None]
    wr = w_router.T.reshape(N_EXPERT_GROUPS, EXPERTS_PER_GROUP, D_MODEL).swapaxes(0, 1)
    wr = jnp.pad(wr, ((0, 0), (0, 8 - N_EXPERT_GROUPS), (0, 0))).reshape(ROUTER_ROWS, D_MODEL)
    br = b_router.reshape(N_EXPERT_GROUPS, EXPERTS_PER_GROUP).T
    br = jnp.pad(br, ((0, 0), (0, 8 - N_EXPERT_GROUPS))).reshape(ROUTER_ROWS, 1)
    layers = []
    for l in range(DEPTH):
        layers.append(dict(
            w_main=w_main[l], w_gate=w_gate_p[l], w_gate_t=w_gate_t[l], b_row=b_row[l], b_col=b_col[l],
            gain=hn_gain[l].reshape(1, D_MLSTM), w_pool=w_pool[l].astype(BF16),
            pool_scale=pool_scale[l].reshape(1, D_POOL), w_proj_a=w_proj_a[l].astype(BF16),
            w_proj_b=w_proj_b[l].astype(BF16), w_out=w_out[l].astype(BF16),
            ln1_g=ln1_g[l].reshape(1, D_MODEL), ln1_b=ln1_b[l].reshape(1, D_MODEL),
            ln2_g=ln2_g[l].reshape(1, D_MODEL), ln2_b=ln2_b[l].reshape(1, D_MODEL),
            w_router_t=wr, b_router_col=br,
            w_e_gate=w_e_gate[l].astype(BF16), w_e_up=w_e_up[l].astype(BF16), w_e_down=w_e_down[l].astype(BF16)))
    return layers


def _prompt_trunk(x_prompt, layers):
    n_seq, seq_len, _ = x_prompt.shape
    x = x_prompt.reshape(n_seq * seq_len, D_MODEL)
    xb = x.astype(BF16)
    cs, ns, ms, bufs = [], [], [], []
    for lw in layers:
        q, k, v, o, u, ga, gb, gcol, grow = _inproj(xb, lw["w_main"], lw["w_gate"], lw["w_gate_t"],
                                                    lw["b_row"], lw["b_col"])
        hg, c1, n1, m1 = _mlstm_prompt(q, k, v, gcol, grow, o, lw["gain"], n_seq, seq_len)
        pooled = _pool_prompt(u, seq_len)
        x1, x1b, gate = _mix(hg, pooled, ga, gb, x, lw)
        x, xb = _moe(x1b, gate, x1, lw)
        cs.append(c1)
        ns.append(n1)
        ms.append(m1[:, :, 0])
        bufs.append(u.reshape(n_seq, seq_len, D_POOL)[:, seq_len - POOL_BUF:])
    return x.reshape(n_seq, seq_len, D_MODEL), jnp.stack(cs), jnp.stack(ns), jnp.stack(ms), jnp.stack(bufs)


def _sample_trunk(x_sample, state_c, state_n, state_m, state_pool, layers):
    n_seq, seq_len, _ = x_sample.shape
    x = x_sample.reshape(n_seq * seq_len, D_MODEL)
    xb = x.astype(BF16)
    cs, ns, ms, bufs = [], [], [], []
    for l, lw in enumerate(layers):
        q, k, v, o, u, ga, gb, gcol, grow = _inproj(xb, lw["w_main"], lw["w_gate"], lw["w_gate_t"],
                                                    lw["b_row"], lw["b_col"])
        m0 = jnp.broadcast_to(state_m[l][:, :, None], (n_seq, N_HEADS, LANES))
        hg, c1, n1, m1 = _mlstm_sample(q, k, v, gcol, grow, o, lw["gain"], state_c[l], state_n[l], m0, seq_len)
        pooled, nbuf = _pool_sample(u.reshape(n_seq, seq_len, D_POOL), state_pool[l], PAST_LEN)
        x1, x1b, gate = _mix(hg, pooled.reshape(n_seq * seq_len, D_POOL), ga, gb, x, lw)
        x, xb = _moe(x1b, gate, x1, lw)
        cs.append(c1)
        ns.append(n1)
        ms.append(m1[:, :, 0])
        bufs.append(nbuf)
    return x.reshape(n_seq, seq_len, D_MODEL), jnp.stack(cs), jnp.stack(ns), jnp.stack(ms), jnp.stack(bufs)


def kernel(x_prompt, x_sample, state_C, state_n, state_m, state_pool, w_in, b_gate, hn_gain, w_pool, pool_scale,
           w_proj_a, w_proj_b, w_out, ln1_g, ln1_b, ln2_g, ln2_b, w_router, b_router, w_e_gate, w_e_up, w_e_down):
    layers = _prepare_weights(w_in, b_gate, hn_gain, w_pool, pool_scale, w_proj_a, w_proj_b, w_out, ln1_g, ln1_b,
                              ln2_g, ln2_b, w_router, b_router, w_e_gate, w_e_up, w_e_down)
    y_p, c_p, n_p, m_p, pool_p = _prompt_trunk(x_prompt, layers)
    y_s, c_s, n_s, m_s, pool_s = _sample_trunk(x_sample, state_C, state_n, state_m, state_pool, layers)
    return (y_p, y_s, c_p, n_p, m_p, pool_p, c_s, n_s, m_s, pool_s)
```

```python
import functools
import math

import jax
import jax.numpy as jnp
from jax import lax
from jax.experimental import pallas as pl
from jax.experimental.pallas import tpu as pltpu

F32 = jnp.float32
BF16 = jnp.bfloat16
I32 = jnp.int32

D_MODEL = 1024
N_HEADS = 4
HEAD_DIM = 256
D_MLSTM = N_HEADS * HEAD_DIM
POOL_WINDOWS = (2, 4, 8, 16)
POOL_GROUP_DIM = 128
D_POOL = len(POOL_WINDOWS) * POOL_GROUP_DIM
POOL_BUF = 15
N_EXPERTS = 16
N_EXPERT_GROUPS = 4
EXPERTS_PER_GROUP = 4
D_EXPERT = 512
DEPTH = 4
PAST_LEN = 16384
ALPHA = (2 * DEPTH) ** 0.25
LN_EPS = 1e-5
K_SCALE = HEAD_DIM ** -0.5

LANES = 128
GATE_ROWS = 16
ROUTER_ROWS = 32
VMEM_LIMIT = 52 * 1024 * 1024
MLSTM_PROMPT_CHUNK = 256
SAMPLE_BLOCK_ROWS = 128

PAIRS = ((0, 1), (0, 2), (0, 3), (1, 2), (1, 3), (2, 3))
N_CLASSES = N_EXPERT_GROUPS * len(PAIRS)
CLASS_ROWS = 32
MOE_TILE = 256
ROW_EXT = D_MODEL + LANES

_G0 = 4 * D_MLSTM
_U0 = _G0 + 2 * N_HEADS
_MAIN_SEGS = ((0, 1024), (1024, 2048), (2048, 3072), (3072, 4096), (4096, 4608), (4608, 5632), (5632, 6656))


def _params(sem, **kw):
    return pltpu.CompilerParams(dimension_semantics=sem, vmem_limit_bytes=VMEM_LIMIT, **kw)


def _const_spec(shape):
    nd = len(shape)
    return pl.BlockSpec(shape, lambda *_: (0,) * nd, pipeline_mode=pl.Buffered(1))


def _row_tile(n, cap=512):
    t = cap
    while n % t:
        t //= 2
    return t


def _inproj_kernel(x_ref, w_ref, wg_ref, wgt_ref, brow_ref, bcol_ref,
                   q_ref, k_ref, v_ref, o_ref, u_ref, ga_ref, gb_ref, gcol_ref, grow_ref):
    x = x_ref[...].astype(BF16)

    def seg(i):
        lo, hi = _MAIN_SEGS[i]
        return jnp.dot(x, w_ref[:, lo:hi], preferred_element_type=F32)

    q_ref[...] = seg(0).astype(BF16)
    k_ref[...] = (seg(1) * K_SCALE).astype(BF16)
    v_ref[...] = seg(2).astype(BF16)
    o_ref[...] = seg(3)
    u_ref[...] = seg(4)
    ga_ref[...] = seg(5)
    gb_ref[...] = seg(6)
    g = jnp.dot(x, wg_ref[...], preferred_element_type=F32) + brow_ref[...]
    lane = lax.broadcasted_iota(I32, g.shape, 1)
    gcol_ref[...] = jnp.where(lane < N_HEADS, g, jax.nn.log_sigmoid(g))
    gt = lax.dot_general(wgt_ref[...], x, (((1,), (1,)), ((), ())), preferred_element_type=F32) + bcol_ref[...]
    sub = lax.broadcasted_iota(I32, gt.shape, 0)
    grow_ref[...] = jnp.where(sub < N_HEADS, gt, jax.nn.log_sigmoid(gt))


def _inproj(x, w_main, w_gate, w_gate_t, b_row, b_col):
    n = x.shape[0]
    tm = _row_tile(n)
    row = lambda w: pl.BlockSpec((tm, w), lambda i: (i, 0))
    out_shape = (
        jax.ShapeDtypeStruct((n, D_MLSTM), BF16), jax.ShapeDtypeStruct((n, D_MLSTM), BF16),
        jax.ShapeDtypeStruct((n, D_MLSTM), BF16), jax.ShapeDtypeStruct((n, D_MLSTM), F32),
        jax.ShapeDtypeStruct((n, D_POOL), F32), jax.ShapeDtypeStruct((n, D_MODEL), F32),
        jax.ShapeDtypeStruct((n, D_MODEL), F32), jax.ShapeDtypeStruct((n, LANES), F32),
        jax.ShapeDtypeStruct((GATE_ROWS, n), F32))
    return pl.pallas_call(
        _inproj_kernel,
        grid=(n // tm,),
        in_specs=[row(D_MODEL), _const_spec(w_main.shape), _const_spec(w_gate.shape),
                  _const_spec(w_gate_t.shape), _const_spec(b_row.shape), _const_spec(b_col.shape)],
        out_specs=(row(D_MLSTM), row(D_MLSTM), row(D_MLSTM), row(D_MLSTM), row(D_POOL), row(D_MODEL),
                   row(D_MODEL), row(LANES), pl.BlockSpec((GATE_ROWS, tm), lambda i: (0, i))),
        out_shape=out_shape,
        compiler_params=_params(("parallel",)),
        name="inproj",
    )(x, w_main, w_gate, w_gate_t, b_row, b_col)


def _mlstm_kernel(*refs, rows, seq_rows, streamed):
    if streamed:
        (q_ref, k_ref, v_ref, gcol_ref, grow_ref, o_ref, gain_ref, c0_ref, n0_ref, m0_ref,
         hg_ref, c1_ref, n1_ref, m1_ref) = refs
    else:
        (q_ref, k_ref, v_ref, gcol_ref, grow_ref, o_ref, gain_ref,
         hg_ref, c1_ref, n1_ref, m1_ref) = refs
        c0_ref, n0_ref, m0_ref = c1_ref, n1_ref, m1_ref
    step = pl.program_id(1)
    t_idx = lax.broadcasted_iota(I32, (rows, 1), 0)
    s_idx = lax.broadcasted_iota(I32, (1, rows), 1)
    if streamed:
        shift = int(math.log2(seq_rows))
        rowm = jnp.right_shift(t_idx, shift) == step
        colm = jnp.right_shift(s_idx, shift) == step
        mask = (s_idx <= t_idx) & rowm & colm
        mask_t = (t_idx <= s_idx) & rowm & colm

        @pl.when(step == 0)
        def _():
            hg_ref[...] = jnp.zeros_like(hg_ref)
    else:
        mask = s_idx <= t_idx
        mask_t = t_idx <= s_idx

        @pl.when(step == 0)
        def _():
            c1_ref[...] = jnp.zeros_like(c1_ref)
            n1_ref[...] = jnp.zeros_like(n1_ref)
            m1_ref[...] = jnp.zeros_like(m1_ref)

    gcol = gcol_ref[...]
    grow = grow_ref[...]
    neg_inf = jnp.float32(-jnp.inf)
    for h in range(N_HEADS):
        sl = slice(h * HEAD_DIM, (h + 1) * HEAD_DIM)
        q = q_ref[:, sl]
        k = k_ref[:, sl]
        v = v_ref[:, sl]
        li_r = grow[h:h + 1, :]
        lf_r = grow[N_HEADS + h:N_HEADS + h + 1, :]
        li_c = gcol[:, h:h + 1]
        lf_c = gcol[:, N_HEADS + h:N_HEADS + h + 1]
        m_prev = m0_ref[0, h:h + 1, 0:1]
        c_prev = c0_ref[0, h]
        n_prev = n0_ref[0, h:h + 1, :]
        b_c = jnp.sum(jnp.where(mask, lf_r, 0.0), axis=1, keepdims=True)
        b_r = jnp.sum(jnp.where(mask_t, lf_c, 0.0), axis=0, keepdims=True)
        dmat = jnp.where(mask, b_c - b_r + li_r, neg_inf)
        inter = b_c + m_prev
        m_t = jnp.maximum(inter, jnp.max(dmat, axis=1, keepdims=True))
        s_inter = jnp.exp(inter - m_t)
        qk = lax.dot_general(q, k, (((1,), (1,)), ((), ())), preferred_element_type=F32)
        s = qk * jnp.exp(dmat - m_t)
        num = (s_inter * jnp.dot(q, c_prev.astype(BF16), preferred_element_type=F32)
               + jnp.dot(s.astype(BF16), v, preferred_element_type=F32))
        den = (s_inter * jnp.sum(q.astype(F32) * n_prev, axis=1, keepdims=True)
               + jnp.sum(s, axis=1, keepdims=True))
        hval = num / jnp.maximum(jnp.abs(den), jnp.exp(-m_t))
        if streamed:
            b_last = jnp.sum(jnp.where(colm, lf_r, 0.0), axis=1, keepdims=True)
            d_last = jnp.where(rowm, b_last - b_c + li_c, neg_inf)
        else:
            b_last = jnp.sum(lf_r, axis=1, keepdims=True)
            d_last = b_last - b_c + li_c
        m_new = jnp.maximum(b_last + m_prev, jnp.max(d_last, axis=0, keepdims=True))
        kw = k.astype(F32) * jnp.exp(d_last - m_new)
        s_last = jnp.exp(b_last + m_prev - m_new)
        upd = lax.dot_general(kw.astype(BF16), v, (((0,), (0,)), ((), ())), preferred_element_type=F32)
        c1_ref[0, h] = s_last * c_prev + upd
        n1_ref[0, h:h + 1, :] = s_last * n_prev + jnp.sum(kw, axis=0, keepdims=True)
        m1_ref[0, h:h + 1, :] = jnp.broadcast_to(m_new, (1, LANES))
        mu = jnp.mean(hval, axis=1, keepdims=True)
        xc = hval - mu
        var = jnp.mean(xc * xc, axis=1, keepdims=True)
        hn = xc * lax.rsqrt(var + LN_EPS) * gain_ref[:, sl]
        out = (jax.nn.sigmoid(o_ref[:, sl]) * hn).astype(BF16)
        if streamed:
            hg_ref[:, sl] = jnp.where(rowm, out, hg_ref[:, sl])
        else:
            hg_ref[:, sl] = out


def _state_shapes(n_seq):
    return (jax.ShapeDtypeStruct((n_seq, N_HEADS, HEAD_DIM, HEAD_DIM), F32),
            jax.ShapeDtypeStruct((n_seq, N_HEADS, HEAD_DIM), F32),
            jax.ShapeDtypeStruct((n_seq, N_HEADS, LANES), F32))


def _mlstm_prompt(q, k, v, gcol, grow, o, gain, n_seq, seq_len):
    n = n_seq * seq_len
    chunk = math.gcd(seq_len, MLSTM_PROMPT_CHUNK)
    nc = seq_len // chunk
    row = lambda w: pl.BlockSpec((chunk, w), lambda b, c: (b * nc + c, 0))
    st = lambda *tail: pl.BlockSpec((1,) + tail, lambda b, c: (b,) + (0,) * len(tail))
    return pl.pallas_call(
        functools.partial(_mlstm_kernel, rows=chunk, seq_rows=seq_len, streamed=False),
        grid=(n_seq, nc),
        in_specs=[row(D_MLSTM), row(D_MLSTM), row(D_MLSTM), row(LANES),
                  pl.BlockSpec((GATE_ROWS, chunk), lambda b, c: (0, b * nc + c)),
                  row(D_MLSTM), _const_spec(gain.shape)],
        out_specs=(row(D_MLSTM), st(N_HEADS, HEAD_DIM, HEAD_DIM), st(N_HEADS, HEAD_DIM), st(N_HEADS, LANES)),
        out_shape=(jax.ShapeDtypeStruct((n, D_MLSTM), BF16),) + _state_shapes(n_seq),
        compiler_params=_params(("parallel", "arbitrary")),
        name="mlstm_prompt",
    )(q, k, v, gcol, grow, o, gain)


def _mlstm_sample(q, k, v, gcol, grow, o, gain, c0, n0, m0, layer, first_row, n_seq, seq_len):
    n = n_seq * seq_len
    rows = min(SAMPLE_BLOCK_ROWS, n)
    per_block = rows // seq_len
    blk0 = first_row // rows
    row = lambda w: pl.BlockSpec((rows, w), lambda i, j: (blk0 + i, 0))
    st = lambda *tail: pl.BlockSpec((1,) + tail, lambda i, j: (i * per_block + j,) + (0,) * len(tail))
    st_in = lambda *tail: pl.BlockSpec((None, 1) + tail,
                                       lambda i, j: (layer, i * per_block + j) + (0,) * len(tail))
    state_specs = (st(N_HEADS, HEAD_DIM, HEAD_DIM), st(N_HEADS, HEAD_DIM), st(N_HEADS, LANES))
    return pl.pallas_call(
        functools.partial(_mlstm_kernel, rows=rows, seq_rows=seq_len, streamed=True),
        grid=(n // rows, per_block),
        in_specs=[row(D_MLSTM), row(D_MLSTM), row(D_MLSTM), row(LANES),
                  pl.BlockSpec((GATE_ROWS, rows), lambda i, j: (0, blk0 + i)),
                  row(D_MLSTM), _const_spec(gain.shape),
                  st_in(N_HEADS, HEAD_DIM, HEAD_DIM), st_in(N_HEADS, HEAD_DIM), st(N_HEADS, LANES)],
        out_specs=(pl.BlockSpec((rows, D_MLSTM), lambda i, j: (i, 0)),) + state_specs,
        out_shape=(jax.ShapeDtypeStruct((n, D_MLSTM), BF16),) + _state_shapes(n_seq),
        compiler_params=_params(("parallel", "arbitrary")),
        name="mlstm_sample",
    )(q, k, v, gcol, grow, o, gain, c0, n0, m0)


def _pool_prompt_kernel(u_ref, prev_ref, out_ref, ext_ref, *, tm, tiles_per_seq):
    tile = pl.program_id(0) % tiles_per_seq
    head = 16
    ext_ref[0:head, :] = jnp.where(tile == 0, 0.0, prev_ref[...])
    ext_ref[head:, :] = u_ref[...]
    pos = tile * tm + lax.broadcasted_iota(I32, (tm, 1), 0)
    for g, w in enumerate(POOL_WINDOWS):
        sl = slice(g * POOL_GROUP_DIM, (g + 1) * POOL_GROUP_DIM)
        acc = ext_ref[head:head + tm, sl]
        for d in range(1, w):
            acc = acc + ext_ref[head - d:head - d + tm, sl]
        cnt = jnp.minimum(pos + 1, w).astype(F32)
        out_ref[:, sl] = (acc / cnt - u_ref[:, sl]).astype(BF16)


def _pool_prompt(u, n_seq, seq_len):
    n = n_seq * seq_len
    tm = _row_tile(seq_len)
    head = 16
    return pl.pallas_call(
        functools.partial(_pool_prompt_kernel, tm=tm, tiles_per_seq=seq_len // tm),
        grid=(n // tm,),
        in_specs=[pl.BlockSpec((tm, D_POOL), lambda i: (i, 0)),
                  pl.BlockSpec((head, D_POOL), lambda i: (jnp.maximum(i * (tm // head) - 1, 0), 0))],
        out_specs=pl.BlockSpec((tm, D_POOL), lambda i: (i, 0)),
        out_shape=jax.ShapeDtypeStruct((n, D_POOL), BF16),
        scratch_shapes=[pltpu.VMEM((head + tm, D_POOL), F32)],
        compiler_params=_params(("parallel",)),
        name="pool_prompt",
    )(u, u)


def _pool_sample_kernel(u_ref, buf_ref, out_ref, nbuf_ref, ext_ref, *, seq_len, start):
    ext_ref[:, 0:POOL_BUF, :] = buf_ref[...]
    ext_ref[:, POOL_BUF:POOL_BUF + seq_len, :] = u_ref[...]
    pos = start + lax.broadcasted_iota(I32, (1, seq_len, 1), 1)
    for g, w in enumerate(POOL_WINDOWS):
        sl = slice(g * POOL_GROUP_DIM, (g + 1) * POOL_GROUP_DIM)
        acc = ext_ref[:, POOL_BUF:POOL_BUF + seq_len, sl]
        for d in range(1, w):
            acc = acc + ext_ref[:, POOL_BUF - d:POOL_BUF - d + seq_len, sl]
        cnt = jnp.minimum(pos + 1, w).astype(F32)
        out_ref[:, :, sl] = (acc / cnt - u_ref[:, :, sl]).astype(BF16)
    nbuf_ref[...] = ext_ref[:, seq_len:seq_len + POOL_BUF, :]


def _pool_sample(u3, buf, layer, start):
    n_seq, seq_len, _ = u3.shape
    bs = _row_tile(n_seq, 32)
    spec = lambda r: pl.BlockSpec((bs, r, D_POOL), lambda i: (i, 0, 0))
    return pl.pallas_call(
        functools.partial(_pool_sample_kernel, seq_len=seq_len, start=start),
        grid=(n_seq // bs,),
        in_specs=[spec(seq_len), pl.BlockSpec((None, bs, POOL_BUF, D_POOL), lambda i: (layer, i, 0, 0))],
        out_specs=(spec(seq_len), spec(POOL_BUF)),
        out_shape=(jax.ShapeDtypeStruct((n_seq, seq_len, D_POOL), BF16),
                   jax.ShapeDtypeStruct((n_seq, POOL_BUF, D_POOL), F32)),
        scratch_shapes=[pltpu.VMEM((bs, POOL_BUF + seq_len + 5, D_POOL), F32)],
        compiler_params=_params(("parallel",)),
        name="pool_sample",
    )(u3, buf)


def _layer_norm(y, g, b):
    mu = jnp.mean(y, axis=1, keepdims=True)
    yc = y - mu
    var = jnp.mean(yc * yc, axis=1, keepdims=True)
    return yc * lax.rsqrt(var + LN_EPS) * g + b


def _route(logits_t):
    tokens = logits_t.shape[1]
    grp = lax.broadcasted_iota(I32, (8, tokens), 0)
    live = grp < N_EXPERT_GROUPS
    neg_inf = jnp.float32(-jnp.inf)
    lm = [jnp.where(live, logits_t[8 * m:8 * m + 8, :], neg_inf) for m in range(EXPERTS_PER_GROUP)]
    mx = jnp.max(jnp.maximum(jnp.maximum(lm[0], lm[1]), jnp.maximum(lm[2], lm[3])), axis=0, keepdims=True)
    ex = [jnp.exp(l - mx) for l in lm]
    tot = jnp.sum(ex[0] + ex[1] + ex[2] + ex[3], axis=0, keepdims=True)
    p = [e / tot for e in ex]
    top1 = jnp.maximum(jnp.maximum(p[0], p[1]), jnp.maximum(p[2], p[3]))
    i1 = jnp.where(p[0] == top1, 0, jnp.where(p[1] == top1, 1, jnp.where(p[2] == top1, 2, 3)))
    r = [jnp.where(i1 == m, -1.0, p[m]) for m in range(EXPERTS_PER_GROUP)]
    top2 = jnp.maximum(jnp.maximum(r[0], r[1]), jnp.maximum(r[2], r[3]))
    i2 = jnp.where(r[0] == top2, 0, jnp.where(r[1] == top2, 1, jnp.where(r[2] == top2, 2, 3)))
    gscore = jnp.where(live, top1 + top2, neg_inf)
    gmax = jnp.max(gscore, axis=0, keepdims=True)
    gsel = jnp.min(jnp.where(gscore == gmax, grp, 8), axis=0, keepdims=True)
    chosen = grp == gsel
    tsum = top1 + top2
    w1 = top1 / tsum
    w2 = top2 / tsum
    first_is_lo = i1 < i2
    lo = jnp.minimum(i1, i2)
    hi = jnp.maximum(i1, i2)
    pair = jnp.where(lo == 0, hi - 1, jnp.where(lo == 1, hi + 1, 5))
    pick = lambda a: jnp.sum(jnp.where(chosen, a, jnp.zeros_like(a)), axis=0, keepdims=True)
    cls = pick(grp * len(PAIRS) + pair)
    w_lo = pick(jnp.where(first_is_lo, w1, w2))
    w_hi = pick(jnp.where(first_is_lo, w2, w1))
    return cls, w_lo, w_hi


def _mix_kernel(hgp_ref, hgs_ref, plp_ref, pls_ref, ga_ref, gb_ref, x_ref, wpool_ref, pscale_ref, wa_ref, wb_ref,
                wout_ref, g1_ref, b1_ref, wr_ref, br_ref,
                dest_ref, cnt_ref, xs_ref,
                rows_ref, dvm_ref, dsm_ref, carry_ref, row_sem, idx_sem,
                *, tm, n_prompt_tiles, n_tiles, capacity):
    i = pl.program_id(0)
    is_prompt = i < n_prompt_tiles
    hg = jnp.where(is_prompt, hgp_ref[...], hgs_ref[...])
    pooled = jnp.where(is_prompt, plp_ref[...], pls_ref[...])
    ya = jnp.dot(hg, wa_ref[...], preferred_element_type=F32)
    parts = []
    for g in range(len(POOL_WINDOWS)):
        sl = slice(g * POOL_GROUP_DIM, (g + 1) * POOL_GROUP_DIM)
        parts.append(jnp.dot(pooled[:, sl], wpool_ref[g], preferred_element_type=F32))
    pl_lin = jnp.concatenate(parts, axis=1) * pscale_ref[...]
    yb = jnp.dot(pl_lin.astype(BF16), wb_ref[...], preferred_element_type=F32)
    mix = jax.nn.sigmoid(ga_ref[...]) * ya + jax.nn.sigmoid(gb_ref[...]) * yb
    res = jnp.dot(mix.astype(BF16), wout_ref[...], preferred_element_type=F32)
    x1 = _layer_norm(ALPHA * x_ref[...] + res, g1_ref[...], b1_ref[...])

    logits_t = lax.dot_general(wr_ref[...], x1, (((1,), (1,)), ((), ())), preferred_element_type=F32,
                               precision=lax.Precision.HIGHEST) + br_ref[...]
    cls, w_lo, w_hi = _route(logits_t)

    @pl.when(i == 0)
    def _():
        carry_ref[...] = jnp.zeros_like(carry_ref)

    onehot = lax.broadcasted_iota(I32, (CLASS_ROWS, tm), 0) == cls
    earlier = lax.broadcasted_iota(I32, (tm, tm), 0) < lax.broadcasted_iota(I32, (tm, tm), 1)
    before = jnp.dot(jnp.where(onehot, 1.0, 0.0).astype(BF16), jnp.where(earlier, 1.0, 0.0).astype(BF16),
                     preferred_element_type=F32)
    seen = carry_ref[:, 0:1]
    rank = jnp.sum(jnp.where(onehot, before + seen, 0.0), axis=0, keepdims=True)
    carry_ref[...] = carry_ref[...] + jnp.sum(jnp.where(onehot, 1.0, 0.0), axis=1, keepdims=True)
    cnt_ref[...] = carry_ref[...]
    dest = cls * capacity + rank.astype(I32)
    dest_ref[...] = jnp.broadcast_to(dest, (8, tm))
    dvm_ref[...] = jnp.broadcast_to(dest, (8, tm))
    idx_copy = pltpu.make_async_copy(dvm_ref.at[0], dsm_ref, idx_sem)
    idx_copy.start()

    slot = i % 2
    rows_done = lambda s: pltpu.make_async_copy(rows_ref.at[s], xs_ref.at[pl.ds(0, tm), :], row_sem.at[s])

    @pl.when(i >= 2)
    def _():
        rows_done(slot).wait()

    wrows = jnp.concatenate([w_lo, w_hi, jnp.zeros((LANES - 2, tm), F32)], axis=0)
    rows_ref[slot, :, 0:D_MODEL] = x1
    rows_ref[slot, :, D_MODEL:ROW_EXT] = wrows.T
    idx_copy.wait()

    def send(r, carry):
        pltpu.make_async_copy(rows_ref.at[slot, pl.ds(r, 1), :], xs_ref.at[pl.ds(dsm_ref[r], 1), :],
                              row_sem.at[slot]).start()
        return carry

    lax.fori_loop(0, tm, send, 0, unroll=8)

    @pl.when(i == n_tiles - 1)
    def _():
        rows_done(slot).wait()
        if n_tiles > 1:
            rows_done(1 - slot).wait()


def _mix(hg_p, hg_s, pooled_p, pooled_s, ga, gb, x, lw, capacity):
    n = x.shape[0]
    n_p = hg_p.shape[0]
    tm = _row_tile(math.gcd(n_p, n - n_p))
    ntp = n_p // tm
    row = lambda w: pl.BlockSpec((tm, w), lambda i: (i, 0))
    prow = lambda w: pl.BlockSpec((tm, w), lambda i: (jnp.minimum(i, ntp - 1), 0))
    srow = lambda w: pl.BlockSpec((tm, w), lambda i: (jnp.maximum(i - ntp, 0), 0))
    consts = (lw["w_pool"], lw["pool_scale"], lw["w_proj_a"], lw["w_proj_b"], lw["w_out"],
              lw["ln1_g"], lw["ln1_b"], lw["w_router_t"], lw["b_router_col"])
    return pl.pallas_call(
        functools.partial(_mix_kernel, tm=tm, n_prompt_tiles=ntp, n_tiles=n // tm, capacity=capacity),
        grid=(n // tm,),
        in_specs=[prow(D_MLSTM), srow(D_MLSTM), prow(D_POOL), srow(D_POOL), row(D_MODEL), row(D_MODEL), row(D_MODEL)]
                 + [_const_spec(c.shape) for c in consts],
        out_specs=(pl.BlockSpec((8, tm), lambda i: (0, i)),
                   pl.BlockSpec((CLASS_ROWS, LANES), lambda i: (0, 0)),
                   pl.BlockSpec(memory_space=pl.ANY)),
        out_shape=(jax.ShapeDtypeStruct((8, n), I32),
                   jax.ShapeDtypeStruct((CLASS_ROWS, LANES), F32),
                   jax.ShapeDtypeStruct((N_CLASSES * capacity, ROW_EXT), F32)),
        scratch_shapes=[pltpu.VMEM((2, tm, ROW_EXT), F32), pltpu.VMEM((8, tm), I32), pltpu.SMEM((tm,), I32),
                        pltpu.VMEM((CLASS_ROWS, LANES), F32), pltpu.SemaphoreType.DMA((2,)),
                        pltpu.SemaphoreType.DMA],
        compiler_params=_params(("arbitrary",)),
        name="mix",
    )(hg_p, hg_s, pooled_p, pooled_s, ga, gb, x, *consts)


def _moe_kernel(blk_ref, elo_ref, ehi_ref, nvalid_ref, ntiles_ref,
                xs_ref, wg_lo, wu_lo, wd_lo, wg_hi, wu_hi, wd_hi, g2_ref, b2_ref, ys_ref):
    i = pl.program_id(0)

    @pl.when(i < ntiles_ref[0])
    def _():
        valid = lax.broadcasted_iota(I32, (MOE_TILE, 1), 0) < nvalid_ref[i]
        xe = xs_ref[...]
        x = jnp.where(valid, xe[:, 0:D_MODEL], 0.0)
        w_lo = jnp.where(valid, xe[:, D_MODEL:D_MODEL + 1], 0.0)
        w_hi = jnp.where(valid, xe[:, D_MODEL + 1:D_MODEL + 2], 0.0)
        xb = x.astype(BF16)

        def expert(wg, wu, wd, w):
            g = jnp.dot(xb, wg[0], preferred_element_type=F32)
            u = jnp.dot(xb, wu[0], preferred_element_type=F32)
            hid = (g * jax.nn.sigmoid(g)) * u * w
            return jnp.dot(hid.astype(BF16), wd[0], preferred_element_type=F32)

        y = expert(wg_lo, wu_lo, wd_lo, w_lo) + expert(wg_hi, wu_hi, wd_hi, w_hi)
        ys_ref[...] = _layer_norm(ALPHA * x + y, g2_ref[...], b2_ref[...])


def _moe(xs, tables, lw, max_tiles):
    blk, e_lo, e_hi, n_valid, n_tiles = tables
    up = lambda sel: pl.BlockSpec((1, D_MODEL, D_EXPERT), lambda i, b, lo, hi, nv, nt: ((lo, hi)[sel][i], 0, 0))
    down = lambda sel: pl.BlockSpec((1, D_EXPERT, D_MODEL), lambda i, b, lo, hi, nv, nt: ((lo, hi)[sel][i], 0, 0))
    const = lambda shape: pl.BlockSpec(shape, lambda i, *_: (0,) * len(shape), pipeline_mode=pl.Buffered(1))
    grid_spec = pltpu.PrefetchScalarGridSpec(
        num_scalar_prefetch=5,
        grid=(max_tiles,),
        in_specs=[pl.BlockSpec((MOE_TILE, ROW_EXT), lambda i, b, *_: (b[i], 0)),
                  up(0), up(0), down(0), up(1), up(1), down(1),
                  const(lw["ln2_g"].shape), const(lw["ln2_b"].shape)],
        out_specs=pl.BlockSpec((MOE_TILE, D_MODEL), lambda i, b, *_: (b[i], 0)))
    return pl.pallas_call(
        _moe_kernel,
        grid_spec=grid_spec,
        out_shape=jax.ShapeDtypeStruct((xs.shape[0], D_MODEL), F32),
        compiler_params=_params(("arbitrary",)),
        name="moe",
    )(blk, e_lo, e_hi, n_valid, n_tiles, xs, lw["w_e_gate"], lw["w_e_up"], lw["w_e_down"],
      lw["w_e_gate"], lw["w_e_up"], lw["w_e_down"], lw["ln2_g"], lw["ln2_b"])


def _tile_tables(counts, capacity, max_tiles):
    cnt = counts[:N_CLASSES, 0].astype(I32)
    tiles = (cnt + MOE_TILE - 1) // MOE_TILE
    ends = jnp.cumsum(tiles)
    starts = ends - tiles
    n_tiles = ends[-1]
    t = jnp.minimum(jnp.arange(max_tiles, dtype=I32), n_tiles - 1)
    cls = jnp.sum((ends[None, :] <= t[:, None]).astype(I32), axis=1)
    onehot = (jnp.arange(N_CLASSES, dtype=I32)[None, :] == cls[:, None]).astype(I32)
    within = t - jnp.sum(onehot * starts[None, :], axis=1)
    n_valid = jnp.clip(jnp.sum(onehot * cnt[None, :], axis=1) - within * MOE_TILE, 0, MOE_TILE)
    blk = cls * (capacity // MOE_TILE) + within
    grp = cls // len(PAIRS)
    pair = cls % len(PAIRS)
    lo = jnp.where(pair < 3, 0, jnp.where(pair < 5, 1, 2))
    hi = jnp.where(pair < 3, pair + 1, jnp.where(pair < 5, pair - 1, 3))
    return (blk, grp * EXPERTS_PER_GROUP + lo, grp * EXPERTS_PER_GROUP + hi, n_valid,
            n_tiles.reshape(1).astype(I32))


def _unpermute_kernel(dest_ref, ys_ref, out_ref, sem, *, tm):
    base = pl.program_id(0) * tm

    def fetch(r, carry):
        pltpu.make_async_copy(ys_ref.at[pl.ds(dest_ref[base + r], 1), :], out_ref.at[pl.ds(r, 1), :], sem).start()
        return carry

    lax.fori_loop(0, tm, fetch, 0, unroll=8)
    pltpu.make_async_copy(ys_ref.at[pl.ds(0, tm), :], out_ref, sem).wait()


def _unpermute(ys, dest):
    n = dest.shape[0]
    tm = _row_tile(n)
    grid_spec = pltpu.PrefetchScalarGridSpec(
        num_scalar_prefetch=1,
        grid=(n // tm,),
        in_specs=[pl.BlockSpec(memory_space=pl.ANY)],
        out_specs=pl.BlockSpec((tm, D_MODEL), lambda i, d: (i, 0)),
        scratch_shapes=[pltpu.SemaphoreType.DMA])
    return pl.pallas_call(
        functools.partial(_unpermute_kernel, tm=tm),
        grid_spec=grid_spec,
        out_shape=jax.ShapeDtypeStruct((n, D_MODEL), F32),
        compiler_params=_params(("arbitrary",)),
        name="unpermute",
    )(dest, ys)


def _prepare_weights(w_in, b_gate, hn_gain, w_pool, pool_scale, w_proj_a, w_proj_b, w_out, ln1_g, ln1_b,
                     ln2_g, ln2_b, w_router, b_router, w_e_gate, w_e_up, w_e_down):
    w_main = jnp.concatenate([w_in[:, :, :_G0], w_in[:, :, _U0:]], axis=2).astype(BF16)
    w_gate = w_in[:, :, _G0:_U0]
    w_gate_p = jnp.pad(w_gate, ((0, 0), (0, 0), (0, LANES - 2 * N_HEADS))).astype(BF16)
    w_gate_t = jnp.pad(jnp.swapaxes(w_gate, 1, 2), ((0, 0), (0, GATE_ROWS - 2 * N_HEADS), (0, 0))).astype(BF16)
    b_row = jnp.pad(b_gate, ((0, 0), (0, LANES - 2 * N_HEADS)))[:, None, :]
    b_col = jnp.pad(b_gate, ((0, 0), (0, GATE_ROWS - 2 * N_HEADS)))[:, :, None]
    wr = w_router.T.reshape(N_EXPERT_GROUPS, EXPERTS_PER_GROUP, D_MODEL).swapaxes(0, 1)
    wr = jnp.pad(wr, ((0, 0), (0, 8 - N_EXPERT_GROUPS), (0, 0))).reshape(ROUTER_ROWS, D_MODEL)
    br = b_router.reshape(N_EXPERT_GROUPS, EXPERTS_PER_GROUP).T
    br = jnp.pad(br, ((0, 0), (0, 8 - N_EXPERT_GROUPS))).reshape(ROUTER_ROWS, 1)
    layers = []
    for l in range(DEPTH):
        layers.append(dict(
            w_main=w_main[l], w_gate=w_gate_p[l], w_gate_t=w_gate_t[l], b_row=b_row[l], b_col=b_col[l],
            gain=hn_gain[l].reshape(1, D_MLSTM), w_pool=w_pool[l].astype(BF16),
            pool_scale=pool_scale[l].reshape(1, D_POOL), w_proj_a=w_proj_a[l].astype(BF16),
            w_proj_b=w_proj_b[l].astype(BF16), w_out=w_out[l].astype(BF16),
            ln1_g=ln1_g[l].reshape(1, D_MODEL), ln1_b=ln1_b[l].reshape(1, D_MODEL),
            ln2_g=ln2_g[l].reshape(1, D_MODEL), ln2_b=ln2_b[l].reshape(1, D_MODEL),
            w_router_t=wr, b_router_col=br,
            w_e_gate=w_e_gate[l].astype(BF16), w_e_up=w_e_up[l].astype(BF16), w_e_down=w_e_down[l].astype(BF16)))
    return layers


def kernel(x_prompt, x_sample, state_C, state_n, state_m, state_pool, w_in, b_gate, hn_gain, w_pool, pool_scale,
           w_proj_a, w_proj_b, w_out, ln1_g, ln1_b, ln2_g, ln2_b, w_router, b_router, w_e_gate, w_e_up, w_e_down):
    layers = _prepare_weights(w_in, b_gate, hn_gain, w_pool, pool_scale, w_proj_a, w_proj_b, w_out, ln1_g, ln1_b,
                              ln2_g, ln2_b, w_router, b_router, w_e_gate, w_e_up, w_e_down)
    n_pseq, p_len, _ = x_prompt.shape
    n_sseq, s_len, _ = x_sample.shape
    n_p = n_pseq * p_len
    n_s = n_sseq * s_len
    n = n_p + n_s
    capacity = -(-n // MOE_TILE) * MOE_TILE
    max_tiles = n // MOE_TILE + N_CLASSES
    x = jnp.concatenate([x_prompt.reshape(n_p, D_MODEL), x_sample.reshape(n_s, D_MODEL)], axis=0)
    cp, np_, mp, bp, cs, ns, ms, bs = [], [], [], [], [], [], [], []
    for l, lw in enumerate(layers):
        q, k, v, o, u, ga, gb, gcol, grow = _inproj(x, lw["w_main"], lw["w_gate"], lw["w_gate_t"],
                                                    lw["b_row"], lw["b_col"])
        hg_p, c1p, n1p, m1p = _mlstm_prompt(q, k, v, gcol, grow, o, lw["gain"], n_pseq, p_len)
        m0 = jnp.broadcast_to(state_m[l][:, :, None], (n_sseq, N_HEADS, LANES))
        hg_s, c1s, n1s, m1s = _mlstm_sample(q, k, v, gcol, grow, o, lw["gain"], state_C, state_n, m0,
                                            l, n_p, n_sseq, s_len)
        pooled_p = _pool_prompt(u, n_pseq, p_len)
        pooled_s, nbuf = _pool_sample(u[n_p:].reshape(n_sseq, s_len, D_POOL), state_pool, l, PAST_LEN)
        dest, counts, xs = _mix(hg_p, hg_s, pooled_p, pooled_s.reshape(n_s, D_POOL), ga, gb, x, lw, capacity)
        ys = _moe(xs, _tile_tables(counts, capacity, max_tiles), lw, max_tiles)
        x = _unpermute(ys, dest[0])
        cp.append(c1p)
        np_.append(n1p)
        mp.append(m1p[:, :, 0])
        bp.append(u[:n_p].reshape(n_pseq, p_len, D_POOL)[:, p_len - POOL_BUF:])
        cs.append(c1s)
        ns.append(n1s)
        ms.append(m1s[:, :, 0])
        bs.append(nbuf)
    st = jnp.stack
    return (x[:n_p].reshape(n_pseq, p_len, D_MODEL), x[n_p:].reshape(n_sseq, s_len, D_MODEL),
            st(cp), st(np_), st(mp), st(bp), st(cs), st(ns), st(ms), st(bs))
```

```python
import functools
import math

import jax
import jax.numpy as jnp
from jax import lax
from jax.experimental import pallas as pl
from jax.experimental.pallas import tpu as pltpu

F32 = jnp.float32
BF16 = jnp.bfloat16
I32 = jnp.int32

D_MODEL = 1024
N_HEADS = 4
HEAD_DIM = 256
D_MLSTM = N_HEADS * HEAD_DIM
POOL_WINDOWS = (2, 4, 8, 16)
POOL_GROUP_DIM = 128
D_POOL = len(POOL_WINDOWS) * POOL_GROUP_DIM
POOL_BUF = 15
N_EXPERTS = 16
N_EXPERT_GROUPS = 4
EXPERTS_PER_GROUP = 4
D_EXPERT = 512
DEPTH = 4
PAST_LEN = 16384
ALPHA = (2 * DEPTH) ** 0.25
LN_EPS = 1e-5
K_SCALE = HEAD_DIM ** -0.5

LANES = 128
GATE_ROWS = 16
ROUTER_ROWS = 32
VMEM_LIMIT = 52 * 1024 * 1024
MLSTM_PROMPT_CHUNK = 256
MLSTM_STRIP = 256
SAMPLE_BLOCK_ROWS = 128
SAMPLE_WINDOW = 16

PAIRS = ((0, 1), (0, 2), (0, 3), (1, 2), (1, 3), (2, 3))
N_CLASSES = N_EXPERT_GROUPS * len(PAIRS)
CLASS_ROWS = 32
MOE_TILE = 256
ROW_EXT = D_MODEL + LANES

_G0 = 4 * D_MLSTM
_U0 = _G0 + 2 * N_HEADS
_MAIN_SEGS = ((0, 1024), (1024, 2048), (2048, 3072), (3072, 4096), (4096, 4608), (4608, 5632), (5632, 6656))


def _params(sem, **kw):
    return pltpu.CompilerParams(dimension_semantics=sem, vmem_limit_bytes=VMEM_LIMIT, **kw)


def _const_spec(shape):
    nd = len(shape)
    return pl.BlockSpec(shape, lambda *_: (0,) * nd, pipeline_mode=pl.Buffered(1))


def _row_tile(n, cap=512):
    t = cap
    while n % t:
        t //= 2
    return t


def _inproj_kernel(x_ref, w_ref, wg_ref, wgt_ref, wkt_ref, brow_ref, bcol_ref,
                   q_ref, k_ref, v_ref, o_ref, u_ref, ga_ref, gb_ref, gcol_ref, grow_ref, kt_ref,
                   *, n_prompt_tiles):
    x = x_ref[...].astype(BF16)

    @pl.when(pl.program_id(0) >= n_prompt_tiles)
    def _():
        kt = lax.dot_general(wkt_ref[...], x, (((1,), (1,)), ((), ())), preferred_element_type=F32)
        kt_ref[...] = (kt * K_SCALE).astype(BF16)

    def seg(i):
        lo, hi = _MAIN_SEGS[i]
        return jnp.dot(x, w_ref[:, lo:hi], preferred_element_type=F32)

    q_ref[...] = seg(0).astype(BF16)
    k_ref[...] = (seg(1) * K_SCALE).astype(BF16)
    v_ref[...] = seg(2).astype(BF16)
    o_ref[...] = seg(3)
    u_ref[...] = seg(4)
    ga_ref[...] = seg(5)
    gb_ref[...] = seg(6)
    g = jnp.dot(x, wg_ref[...], preferred_element_type=F32) + brow_ref[...]
    lane = lax.broadcasted_iota(I32, g.shape, 1)
    gcol_ref[...] = jnp.where(lane < N_HEADS, g, jax.nn.log_sigmoid(g))
    gt = lax.dot_general(wgt_ref[...], x, (((1,), (1,)), ((), ())), preferred_element_type=F32) + bcol_ref[...]
    sub = lax.broadcasted_iota(I32, gt.shape, 0)
    grow_ref[...] = jnp.where(sub < N_HEADS, gt, jax.nn.log_sigmoid(gt))


def _inproj(x, lw, n_prompt):
    n = x.shape[0]
    tm = _row_tile(math.gcd(n_prompt, n - n_prompt))
    ntp = n_prompt // tm
    row = lambda w: pl.BlockSpec((tm, w), lambda i: (i, 0))
    consts = (lw["w_main"], lw["w_gate"], lw["w_gate_t"], lw["w_key_t"], lw["b_row"], lw["b_col"])
    out_shape = (
        jax.ShapeDtypeStruct((n, D_MLSTM), BF16), jax.ShapeDtypeStruct((n, D_MLSTM), BF16),
        jax.ShapeDtypeStruct((n, D_MLSTM), BF16), jax.ShapeDtypeStruct((n, D_MLSTM), F32),
        jax.ShapeDtypeStruct((n, D_POOL), F32), jax.ShapeDtypeStruct((n, D_MODEL), F32),
        jax.ShapeDtypeStruct((n, D_MODEL), F32), jax.ShapeDtypeStruct((n, LANES), F32),
        jax.ShapeDtypeStruct((GATE_ROWS, n), F32), jax.ShapeDtypeStruct((D_MLSTM, n - n_prompt), BF16))
    return pl.pallas_call(
        functools.partial(_inproj_kernel, n_prompt_tiles=ntp),
        grid=(n // tm,),
        in_specs=[row(D_MODEL)] + [_const_spec(c.shape) for c in consts],
        out_specs=(row(D_MLSTM), row(D_MLSTM), row(D_MLSTM), row(D_MLSTM), row(D_POOL), row(D_MODEL),
                   row(D_MODEL), row(LANES), pl.BlockSpec((GATE_ROWS, tm), lambda i: (0, i)),
                   pl.BlockSpec((D_MLSTM, tm), lambda i: (0, jnp.maximum(i - ntp, 0)))),
        out_shape=out_shape,
        compiler_params=_params(("arbitrary",)),
        name="inproj",
    )(x, *consts)


def _head_out(hval, o, gain):
    mu = jnp.mean(hval, axis=1, keepdims=True)
    xc = hval - mu
    var = jnp.mean(xc * xc, axis=1, keepdims=True)
    return (jax.nn.sigmoid(o) * (xc * lax.rsqrt(var + LN_EPS) * gain)).astype(BF16)


def _split3(x):
    x1 = x.astype(BF16)
    r1 = x - x1.astype(F32)
    x2 = r1.astype(BF16)
    x3 = (r1 - x2.astype(F32)).astype(BF16)
    return x1, x2, x3


def _mlstm_prompt_kernel(*refs, rows, strip, chained):
    if chained:
        q_ref, k_ref, v_ref, gcol_ref, grow_ref, o_ref, gain_ref, _, hg_ref, c_ref, n_ref, m_ref = refs
    else:
        q_ref, k_ref, v_ref, gcol_ref, grow_ref, o_ref, gain_ref, hg_ref, c_ref, n_ref, m_ref = refs

    @pl.when(pl.program_id(1) == 0)
    def _():
        c_ref[...] = jnp.zeros_like(c_ref)
        n_ref[...] = jnp.zeros_like(n_ref)
        m_ref[...] = jnp.zeros_like(m_ref)

    t_idx = lax.broadcasted_iota(I32, (rows, 1), 0)
    s_idx = lax.broadcasted_iota(I32, (1, rows), 1)
    mask = s_idx <= t_idx
    lower = jnp.where(mask, 1.0, 0.0).astype(BF16)
    upper = jnp.where(t_idx <= s_idx, 1.0, 0.0).astype(BF16)
    gcol = gcol_ref[...]
    grow = grow_ref[...]
    bcol = sum(jnp.dot(lower, p, preferred_element_type=F32) for p in _split3(gcol))
    brow = sum(jnp.dot(p, upper, preferred_element_type=F32) for p in _split3(grow))
    ones = jnp.ones((rows, LANES), BF16)
    neg_inf = jnp.float32(-jnp.inf)
    for h in range(N_HEADS):
        sl = slice(h * HEAD_DIM, (h + 1) * HEAD_DIM)
        k = k_ref[:, sl]
        v = v_ref[:, sl]
        li_c = gcol[:, h:h + 1]
        b_c = bcol[:, N_HEADS + h:N_HEADS + h + 1]
        b_r = brow[N_HEADS + h:N_HEADS + h + 1, :]
        g_r = grow[h:h + 1, :] - b_r
        m_prev = m_ref[0, h:h + 1, 0:1]
        c_prev = c_ref[0, h]
        n_prev = n_ref[0, h]
        cn = jnp.concatenate([c_prev.astype(BF16), n_prev.astype(BF16)], axis=1)
        vo = jnp.concatenate([v, ones], axis=1)
        for r0 in range(0, rows, strip):
            rs = slice(r0, r0 + strip)
            mk = mask[rs]
            q = q_ref[rs, sl]
            top = jnp.maximum(m_prev, jnp.max(jnp.where(mk, g_r, neg_inf), axis=1, keepdims=True))
            s_inter = jnp.exp(m_prev - top)
            qk = lax.dot_general(q, k, (((1,), (1,)), ((), ())), preferred_element_type=F32)
            s = qk * jnp.exp(jnp.where(mk, g_r - top, neg_inf))
            inter = jnp.dot(q, cn, preferred_element_type=F32)
            intra = jnp.dot(s.astype(BF16), vo, preferred_element_type=F32)
            num = s_inter * inter[:, 0:HEAD_DIM] + intra[:, 0:HEAD_DIM]
            den = s_inter * inter[:, HEAD_DIM:HEAD_DIM + 1] + intra[:, HEAD_DIM:HEAD_DIM + 1]
            hval = num / jnp.maximum(jnp.abs(den), jnp.exp(-(b_c[rs] + top)))
            hg_ref[rs, sl] = _head_out(hval, o_ref[rs, sl], gain_ref[:, sl])
        b_last = b_r[:, rows - 1:rows]
        d_last = b_last - b_c + li_c
        m_new = jnp.maximum(b_last + m_prev, jnp.max(d_last, axis=0, keepdims=True))
        kw = (k.astype(F32) * jnp.exp(d_last - m_new)).astype(BF16)
        s_last = jnp.exp(b_last + m_prev - m_new)
        upd = lax.dot_general(kw, vo, (((0,), (0,)), ((), ())), preferred_element_type=F32)
        c_ref[0, h] = s_last * c_prev + upd[:, 0:HEAD_DIM]
        n_ref[0, h] = s_last * n_prev + upd[:, HEAD_DIM:]
        m_ref[0, h:h + 1, :] = jnp.broadcast_to(m_new, (1, LANES))


def _mlstm_sample_kernel(*refs, seq_len, window, block, chained):
    if chained:
        (q_ref, k_ref, kt_ref, v_ref, gcol_ref, grow_ref, mtok_ref, o_ref, gain_ref, c0_ref, n0_ref, _,
         hg_ref, c1_ref, n1_ref, m1_ref) = refs
    else:
        (q_ref, k_ref, kt_ref, v_ref, gcol_ref, grow_ref, mtok_ref, o_ref, gain_ref, c0_ref, n0_ref,
         hg_ref, c1_ref, n1_ref, m1_ref) = refs
    n_seq = window // seq_len
    shift = int(math.log2(seq_len))
    w0 = (pl.program_id(0) % (block // window)) * window
    r_idx = lax.broadcasted_iota(I32, (window, 1), 0)
    t_idx = w0 + r_idx
    s_idx = lax.broadcasted_iota(I32, (1, block), 1)
    same = jnp.right_shift(t_idx, shift) == jnp.right_shift(s_idx, shift)
    mask = same & (s_idx <= t_idx)
    mask_t = same & (t_idx <= s_idx)
    r_seq = jnp.right_shift(r_idx, shift)
    l_seq = jnp.right_shift(s_idx - w0, shift)
    gcol = gcol_ref[...]
    grow = grow_ref[...]
    mtok = mtok_ref[...]
    neg_inf = jnp.float32(-jnp.inf)
    for h in range(N_HEADS):
        sl = slice(h * HEAD_DIM, (h + 1) * HEAD_DIM)
        q = q_ref[:, sl]
        k = k_ref[:, sl]
        kt = kt_ref[sl, :]
        v = v_ref[:, sl]
        li_r = grow[h:h + 1, :]
        lf_r = grow[N_HEADS + h:N_HEADS + h + 1, :]
        li_c = gcol[:, h:h + 1]
        lf_c = gcol[:, N_HEADS + h:N_HEADS + h + 1]
        m_c = mtok[:, h:h + 1]
        b_c = jnp.sum(jnp.where(mask, lf_r, 0.0), axis=1, keepdims=True)
        b_r = jnp.sum(jnp.where(mask_t, lf_c, 0.0), axis=0, keepdims=True)
        dmat = jnp.where(mask, b_c - b_r + li_r, neg_inf)
        inter = b_c + m_c
        m_t = jnp.maximum(inter, jnp.max(dmat, axis=1, keepdims=True))
        s_inter = jnp.exp(inter - m_t)
        s = jnp.dot(q, kt, preferred_element_type=F32) * jnp.exp(dmat - m_t)
        intra = jnp.dot(s.astype(BF16), v, preferred_element_type=F32)
        qf = q.astype(F32)
        qc = qn = None
        for j in range(n_seq):
            qc_j = jnp.dot(q, c0_ref[j, h].astype(BF16), preferred_element_type=F32)
            qn_j = jnp.sum(qf * n0_ref[j, h:h + 1, :], axis=1, keepdims=True)
            qc = qc_j if j == 0 else jnp.where(r_seq == j, qc_j, qc)
            qn = qn_j if j == 0 else jnp.where(r_seq == j, qn_j, qn)
        num = s_inter * qc + intra
        den = s_inter * qn + jnp.sum(s, axis=1, keepdims=True)
        hval = num / jnp.maximum(jnp.abs(den), jnp.exp(-m_t))
        hg_ref[:, sl] = _head_out(hval, o_ref[:, sl], gain_ref[:, sl])
        kf = k.astype(F32)
        ktf = kt.astype(F32)
        for j in range(n_seq):
            lsel = l_seq == j
            m_j = mtok[j * seq_len:j * seq_len + 1, h:h + 1]
            b_last = jnp.sum(jnp.where(lsel, lf_r, 0.0), axis=1, keepdims=True)
            d_r = jnp.where(lsel, b_last - b_r + li_r, neg_inf)
            d_c = jnp.where(r_seq == j, b_last - b_c + li_c, neg_inf)
            m_new = jnp.maximum(b_last + m_j, jnp.max(d_r, axis=1, keepdims=True))
            s_last = jnp.exp(b_last + m_j - m_new)
            kwt = (ktf * jnp.exp(d_r - m_new)).astype(BF16)
            c1_ref[j, h] = s_last * c0_ref[j, h] + jnp.dot(kwt, v, preferred_element_type=F32)
            n1_ref[j, h:h + 1, :] = (s_last * n0_ref[j, h:h + 1, :]
                                     + jnp.sum(kf * jnp.exp(d_c - m_new), axis=0, keepdims=True))
            m1_ref[j, h:h + 1, :] = jnp.broadcast_to(m_new, (1, LANES))


def _chain(c_all):
    if c_all is None:
        return [], []
    return [c_all], [pl.BlockSpec(memory_space=pl.ANY)]


def _mlstm_prompt(q, k, v, gcol, grow, o, gain, c_all, layer, n_seq, seq_len):
    n = n_seq * seq_len
    chunk = math.gcd(seq_len, MLSTM_PROMPT_CHUNK)
    nc = seq_len // chunk
    row = lambda w: pl.BlockSpec((chunk, w), lambda b, c: (b * nc + c, 0))
    st = lambda *tail: pl.BlockSpec((1,) + tail, lambda b, c: (b,) + (0,) * len(tail))
    extra, extra_specs = _chain(c_all)
    n_in = 7
    return pl.pallas_call(
        functools.partial(_mlstm_prompt_kernel, rows=chunk, strip=min(chunk, MLSTM_STRIP), chained=bool(extra)),
        grid=(n_seq, nc),
        in_specs=[row(D_MLSTM), row(D_MLSTM), row(D_MLSTM), row(LANES),
                  pl.BlockSpec((GATE_ROWS, chunk), lambda b, c: (0, b * nc + c)),
                  row(D_MLSTM), _const_spec(gain.shape)] + extra_specs,
        out_specs=(row(D_MLSTM),
                   pl.BlockSpec((None, 1, N_HEADS, HEAD_DIM, HEAD_DIM), lambda b, c: (layer, b, 0, 0, 0)),
                   st(N_HEADS, HEAD_DIM, LANES), st(N_HEADS, LANES)),
        out_shape=(jax.ShapeDtypeStruct((n, D_MLSTM), BF16),
                   jax.ShapeDtypeStruct((DEPTH, n_seq, N_HEADS, HEAD_DIM, HEAD_DIM), F32),
                   jax.ShapeDtypeStruct((n_seq, N_HEADS, HEAD_DIM, LANES), F32),
                   jax.ShapeDtypeStruct((n_seq, N_HEADS, LANES), F32)),
        input_output_aliases={n_in: 1} if extra else {},
        compiler_params=_params(("parallel", "arbitrary")),
        name="mlstm_prompt",
    )(q, k, v, gcol, grow, o, gain, *extra)


def _mlstm_sample(q, k, kt, v, gcol, grow, mtok, o, gain, c0, n0, c_all, layer, first_row, n_seq, seq_len):
    n = n_seq * seq_len
    window, block = SAMPLE_WINDOW, SAMPLE_BLOCK_ROWS
    assert n % block == 0 and first_row % block == 0 and window % seq_len == 0
    per_win = window // seq_len
    sub = block // window
    row = lambda w: pl.BlockSpec((window, w), lambda i: (first_row // window + i, 0))
    extra, extra_specs = _chain(c_all)
    n_in = 11
    return pl.pallas_call(
        functools.partial(_mlstm_sample_kernel, seq_len=seq_len, window=window, block=block, chained=bool(extra)),
        grid=(n // window,),
        in_specs=[row(D_MLSTM), row(D_MLSTM),
                  pl.BlockSpec((D_MLSTM, block), lambda i: (0, i // sub)),
                  pl.BlockSpec((block, D_MLSTM), lambda i: (first_row // block + i // sub, 0)),
                  row(LANES),
                  pl.BlockSpec((GATE_ROWS, block), lambda i: (0, first_row // block + i // sub)),
                  pl.BlockSpec((window, LANES), lambda i: (i, 0)),
                  row(D_MLSTM), _const_spec(gain.shape),
                  pl.BlockSpec((None, per_win, N_HEADS, HEAD_DIM, HEAD_DIM), lambda i: (layer, i, 0, 0, 0)),
                  pl.BlockSpec((None, per_win, N_HEADS, HEAD_DIM), lambda i: (layer, i, 0, 0))] + extra_specs,
        out_specs=(pl.BlockSpec((window, D_MLSTM), lambda i: (i, 0)),
                   pl.BlockSpec((None, per_win, N_HEADS, HEAD_DIM, HEAD_DIM), lambda i: (layer, i, 0, 0, 0)),
                   pl.BlockSpec((per_win, N_HEADS, HEAD_DIM), lambda i: (i, 0, 0)),
                   pl.BlockSpec((per_win, N_HEADS, LANES), lambda i: (i, 0, 0))),
        out_shape=(jax.ShapeDtypeStruct((n, D_MLSTM), BF16),
                   jax.ShapeDtypeStruct((DEPTH, n_seq, N_HEADS, HEAD_DIM, HEAD_DIM), F32),
                   jax.ShapeDtypeStruct((n_seq, N_HEADS, HEAD_DIM), F32),
                   jax.ShapeDtypeStruct((n_seq, N_HEADS, LANES), F32)),
        input_output_aliases={n_in: 1} if extra else {},
        compiler_params=_params(("parallel",)),
        name="mlstm_sample",
    )(q, k, kt, v, gcol, grow, mtok, o, gain, c0, n0, *extra)


def _pool_prompt_kernel(u_ref, prev_ref, out_ref, ext_ref, *, tm, tiles_per_seq):
    tile = pl.program_id(0) % tiles_per_seq
    head = 16
    ext_ref[0:head, :] = jnp.where(tile == 0, 0.0, prev_ref[...])
    ext_ref[head:, :] = u_ref[...]
    pos = tile * tm + lax.broadcasted_iota(I32, (tm, 1), 0)
    for g, w in enumerate(POOL_WINDOWS):
        sl = slice(g * POOL_GROUP_DIM, (g + 1) * POOL_GROUP_DIM)
        acc = ext_ref[head:head + tm, sl]
        for d in range(1, w):
            acc = acc + ext_ref[head - d:head - d + tm, sl]
        cnt = jnp.minimum(pos + 1, w).astype(F32)
        out_ref[:, sl] = (acc / cnt - u_ref[:, sl]).astype(BF16)


def _pool_prompt(u, n_seq, seq_len):
    n = n_seq * seq_len
    tm = _row_tile(seq_len)
    head = 16
    return pl.pallas_call(
        functools.partial(_pool_prompt_kernel, tm=tm, tiles_per_seq=seq_len // tm),
        grid=(n // tm,),
        in_specs=[pl.BlockSpec((tm, D_POOL), lambda i: (i, 0)),
                  pl.BlockSpec((head, D_POOL), lambda i: (jnp.maximum(i * (tm // head) - 1, 0), 0))],
        out_specs=pl.BlockSpec((tm, D_POOL), lambda i: (i, 0)),
        out_shape=jax.ShapeDtypeStruct((n, D_POOL), BF16),
        scratch_shapes=[pltpu.VMEM((head + tm, D_POOL), F32)],
        compiler_params=_params(("parallel",)),
        name="pool_prompt",
    )(u, u)


def _pool_sample_kernel(u_ref, buf_ref, out_ref, nbuf_ref, ext_ref, *, seq_len, start):
    ext_ref[:, 0:POOL_BUF, :] = buf_ref[...]
    ext_ref[:, POOL_BUF:POOL_BUF + seq_len, :] = u_ref[...]
    pos = start + lax.broadcasted_iota(I32, (1, seq_len, 1), 1)
    for g, w in enumerate(POOL_WINDOWS):
        sl = slice(g * POOL_GROUP_DIM, (g + 1) * POOL_GROUP_DIM)
        acc = ext_ref[:, POOL_BUF:POOL_BUF + seq_len, sl]
        for d in range(1, w):
            acc = acc + ext_ref[:, POOL_BUF - d:POOL_BUF - d + seq_len, sl]
        cnt = jnp.minimum(pos + 1, w).astype(F32)
        out_ref[:, :, sl] = (acc / cnt - u_ref[:, :, sl]).astype(BF16)
    nbuf_ref[...] = ext_ref[:, seq_len:seq_len + POOL_BUF, :]


def _pool_sample(u3, buf, layer, start):
    n_seq, seq_len, _ = u3.shape
    bs = _row_tile(n_seq, 32)
    spec = lambda r: pl.BlockSpec((bs, r, D_POOL), lambda i: (i, 0, 0))
    return pl.pallas_call(
        functools.partial(_pool_sample_kernel, seq_len=seq_len, start=start),
        grid=(n_seq // bs,),
        in_specs=[spec(seq_len), pl.BlockSpec((None, bs, POOL_BUF, D_POOL), lambda i: (layer, i, 0, 0))],
        out_specs=(spec(seq_len), spec(POOL_BUF)),
        out_shape=(jax.ShapeDtypeStruct((n_seq, seq_len, D_POOL), BF16),
                   jax.ShapeDtypeStruct((n_seq, POOL_BUF, D_POOL), F32)),
        scratch_shapes=[pltpu.VMEM((bs, POOL_BUF + seq_len + 5, D_POOL), F32)],
        compiler_params=_params(("parallel",)),
        name="pool_sample",
    )(u3, buf)


def _layer_norm(y, g, b):
    mu = jnp.mean(y, axis=1, keepdims=True)
    yc = y - mu
    var = jnp.mean(yc * yc, axis=1, keepdims=True)
    return yc * lax.rsqrt(var + LN_EPS) * g + b


def _route(logits_t):
    tokens = logits_t.shape[1]
    grp = lax.broadcasted_iota(I32, (8, tokens), 0)
    live = grp < N_EXPERT_GROUPS
    neg_inf = jnp.float32(-jnp.inf)
    lm = [jnp.where(live, logits_t[8 * m:8 * m + 8, :], neg_inf) for m in range(EXPERTS_PER_GROUP)]
    mx = jnp.max(jnp.maximum(jnp.maximum(lm[0], lm[1]), jnp.maximum(lm[2], lm[3])), axis=0, keepdims=True)
    ex = [jnp.exp(l - mx) for l in lm]
    tot = jnp.sum(ex[0] + ex[1] + ex[2] + ex[3], axis=0, keepdims=True)
    p = [e / tot for e in ex]
    top1 = jnp.maximum(jnp.maximum(p[0], p[1]), jnp.maximum(p[2], p[3]))
    i1 = jnp.where(p[0] == top1, 0, jnp.where(p[1] == top1, 1, jnp.where(p[2] == top1, 2, 3)))
    r = [jnp.where(i1 == m, -1.0, p[m]) for m in range(EXPERTS_PER_GROUP)]
    top2 = jnp.maximum(jnp.maximum(r[0], r[1]), jnp.maximum(r[2], r[3]))
    i2 = jnp.where(r[0] == top2, 0, jnp.where(r[1] == top2, 1, jnp.where(r[2] == top2, 2, 3)))
    gscore = jnp.where(live, top1 + top2, neg_inf)
    gmax = jnp.max(gscore, axis=0, keepdims=True)
    gsel = jnp.min(jnp.where(gscore == gmax, grp, 8), axis=0, keepdims=True)
    chosen = grp == gsel
    tsum = top1 + top2
    w1 = top1 / tsum
    w2 = top2 / tsum
    first_is_lo = i1 < i2
    lo = jnp.minimum(i1, i2)
    hi = jnp.maximum(i1, i2)
    pair = jnp.where(lo == 0, hi - 1, jnp.where(lo == 1, hi + 1, 5))
    pick = lambda a: jnp.sum(jnp.where(chosen, a, jnp.zeros_like(a)), axis=0, keepdims=True)
    cls = pick(grp * len(PAIRS) + pair)
    w_lo = pick(jnp.where(first_is_lo, w1, w2))
    w_hi = pick(jnp.where(first_is_lo, w2, w1))
    return cls, w_lo, w_hi


def _mix_kernel(hgp_ref, hgs_ref, plp_ref, pls_ref, ga_ref, gb_ref, x_ref, wpool_ref, pscale_ref, wa_ref, wb_ref,
                wout_ref, g1_ref, b1_ref, wr_ref, br_ref,
                dest_ref, cnt_ref, xs_ref,
                rows_ref, dvm_ref, dsm_ref, carry_ref, row_sem, idx_sem,
                *, tm, n_prompt_tiles, n_tiles, capacity):
    i = pl.program_id(0)
    is_prompt = i < n_prompt_tiles
    hg = jnp.where(is_prompt, hgp_ref[...], hgs_ref[...])
    pooled = jnp.where(is_prompt, plp_ref[...], pls_ref[...])
    ya = jnp.dot(hg, wa_ref[...], preferred_element_type=F32)
    parts = []
    for g in range(len(POOL_WINDOWS)):
        sl = slice(g * POOL_GROUP_DIM, (g + 1) * POOL_GROUP_DIM)
        parts.append(jnp.dot(pooled[:, sl], wpool_ref[g], preferred_element_type=F32))
    pl_lin = jnp.concatenate(parts, axis=1) * pscale_ref[...]
    yb = jnp.dot(pl_lin.astype(BF16), wb_ref[...], preferred_element_type=F32)
    mix = jax.nn.sigmoid(ga_ref[...]) * ya + jax.nn.sigmoid(gb_ref[...]) * yb
    res = jnp.dot(mix.astype(BF16), wout_ref[...], preferred_element_type=F32)
    x1 = _layer_norm(ALPHA * x_ref[...] + res, g1_ref[...], b1_ref[...])

    logits_t = lax.dot_general(wr_ref[...], x1, (((1,), (1,)), ((), ())), preferred_element_type=F32,
                               precision=lax.Precision.HIGHEST) + br_ref[...]
    cls, w_lo, w_hi = _route(logits_t)

    @pl.when(i == 0)
    def _():
        carry_ref[...] = jnp.zeros_like(carry_ref)

    onehot = lax.broadcasted_iota(I32, (CLASS_ROWS, tm), 0) == cls
    earlier = lax.broadcasted_iota(I32, (tm, tm), 0) < lax.broadcasted_iota(I32, (tm, tm), 1)
    before = jnp.dot(jnp.where(onehot, 1.0, 0.0).astype(BF16), jnp.where(earlier, 1.0, 0.0).astype(BF16),
                     preferred_element_type=F32)
    seen = carry_ref[:, 0:1]
    rank = jnp.sum(jnp.where(onehot, before + seen, 0.0), axis=0, keepdims=True)
    carry_ref[...] = carry_ref[...] + jnp.sum(jnp.where(onehot, 1.0, 0.0), axis=1, keepdims=True)
    cnt_ref[...] = carry_ref[...]
    dest = cls * capacity + rank.astype(I32)
    dest_ref[...] = jnp.broadcast_to(dest, (8, tm))
    dvm_ref[...] = jnp.broadcast_to(dest, (8, tm))
    idx_copy = pltpu.make_async_copy(dvm_ref.at[0], dsm_ref, idx_sem)
    idx_copy.start()

    slot = i % 2
    rows_done = lambda s: pltpu.make_async_copy(rows_ref.at[s], xs_ref.at[pl.ds(0, tm), :], row_sem.at[s])

    @pl.when(i >= 2)
    def _():
        rows_done(slot).wait()

    wrows = jnp.concatenate([w_lo, w_hi, jnp.zeros((LANES - 2, tm), F32)], axis=0)
    rows_ref[slot, :, 0:D_MODEL] = x1
    rows_ref[slot, :, D_MODEL:ROW_EXT] = wrows.T
    idx_copy.wait()

    def send(r, carry):
        pltpu.make_async_copy(rows_ref.at[slot, pl.ds(r, 1), :], xs_ref.at[pl.ds(dsm_ref[r], 1), :],
                              row_sem.at[slot]).start()
        return carry

    lax.fori_loop(0, tm, send, 0, unroll=8)

    @pl.when(i == n_tiles - 1)
    def _():
        rows_done(slot).wait()
        if n_tiles > 1:
            rows_done(1 - slot).wait()


def _mix(hg_p, hg_s, pooled_p, pooled_s, ga, gb, x, lw, capacity):
    n = x.shape[0]
    n_p = hg_p.shape[0]
    tm = _row_tile(math.gcd(n_p, n - n_p))
    ntp = n_p // tm
    row = lambda w: pl.BlockSpec((tm, w), lambda i: (i, 0))
    prow = lambda w: pl.BlockSpec((tm, w), lambda i: (jnp.minimum(i, ntp - 1), 0))
    srow = lambda w: pl.BlockSpec((tm, w), lambda i: (jnp.maximum(i - ntp, 0), 0))
    consts = (lw["w_pool"], lw["pool_scale"], lw["w_proj_a"], lw["w_proj_b"], lw["w_out"],
              lw["ln1_g"], lw["ln1_b"], lw["w_router_t"], lw["b_router_col"])
    return pl.pallas_call(
        functools.partial(_mix_kernel, tm=tm, n_prompt_tiles=ntp, n_tiles=n // tm, capacity=capacity),
        grid=(n // tm,),
        in_specs=[prow(D_MLSTM), srow(D_MLSTM), prow(D_POOL), srow(D_POOL), row(D_MODEL), row(D_MODEL), row(D_MODEL)]
                 + [_const_spec(c.shape) for c in consts],
        out_specs=(pl.BlockSpec((8, tm), lambda i: (0, i)),
                   pl.BlockSpec((CLASS_ROWS, LANES), lambda i: (0, 0)),
                   pl.BlockSpec(memory_space=pl.ANY)),
        out_shape=(jax.ShapeDtypeStruct((8, n), I32),
                   jax.ShapeDtypeStruct((CLASS_ROWS, LANES), F32),
                   jax.ShapeDtypeStruct((N_CLASSES * capacity, ROW_EXT), F32)),
        scratch_shapes=[pltpu.VMEM((2, tm, ROW_EXT), F32), pltpu.VMEM((8, tm), I32), pltpu.SMEM((tm,), I32),
                        pltpu.VMEM((CLASS_ROWS, LANES), F32), pltpu.SemaphoreType.DMA((2,)),
                        pltpu.SemaphoreType.DMA],
        compiler_params=_params(("arbitrary",)),
        name="mix",
    )(hg_p, hg_s, pooled_p, pooled_s, ga, gb, x, *consts)


def _moe_kernel(blk_ref, elo_ref, ehi_ref, nvalid_ref, ntiles_ref,
                xs_ref, wg_lo, wu_lo, wd_lo, wg_hi, wu_hi, wd_hi, g2_ref, b2_ref, ys_ref):
    i = pl.program_id(0)

    @pl.when(i < ntiles_ref[0])
    def _():
        valid = lax.broadcasted_iota(I32, (MOE_TILE, 1), 0) < nvalid_ref[i]
        xe = xs_ref[...]
        x = jnp.where(valid, xe[:, 0:D_MODEL], 0.0)
        w_lo = jnp.where(valid, xe[:, D_MODEL:D_MODEL + 1], 0.0)
        w_hi = jnp.where(valid, xe[:, D_MODEL + 1:D_MODEL + 2], 0.0)
        xb = x.astype(BF16)

        def expert(wg, wu, wd, w):
            g = jnp.dot(xb, wg[0], preferred_element_type=F32)
            u = jnp.dot(xb, wu[0], preferred_element_type=F32)
            hid = (g * jax.nn.sigmoid(g)) * u * w
            return jnp.dot(hid.astype(BF16), wd[0], preferred_element_type=F32)

        y = expert(wg_lo, wu_lo, wd_lo, w_lo) + expert(wg_hi, wu_hi, wd_hi, w_hi)
        ys_ref[...] = _layer_norm(ALPHA * x + y, g2_ref[...], b2_ref[...])


def _moe(xs, tables, lw, max_tiles):
    blk, e_lo, e_hi, n_valid, n_tiles = tables
    up = lambda sel: pl.BlockSpec((1, D_MODEL, D_EXPERT), lambda i, b, lo, hi, nv, nt: ((lo, hi)[sel][i], 0, 0))
    down = lambda sel: pl.BlockSpec((1, D_EXPERT, D_MODEL), lambda i, b, lo, hi, nv, nt: ((lo, hi)[sel][i], 0, 0))
    const = lambda shape: pl.BlockSpec(shape, lambda i, *_: (0,) * len(shape), pipeline_mode=pl.Buffered(1))
    grid_spec = pltpu.PrefetchScalarGridSpec(
        num_scalar_prefetch=5,
        grid=(max_tiles,),
        in_specs=[pl.BlockSpec((MOE_TILE, ROW_EXT), lambda i, b, *_: (b[i], 0)),
                  up(0), up(0), down(0), up(1), up(1), down(1),
                  const(lw["ln2_g"].shape), const(lw["ln2_b"].shape)],
        out_specs=pl.BlockSpec((MOE_TILE, D_MODEL), lambda i, b, *_: (b[i], 0)))
    return pl.pallas_call(
        _moe_kernel,
        grid_spec=grid_spec,
        out_shape=jax.ShapeDtypeStruct((xs.shape[0], D_MODEL), F32),
        compiler_params=_params(("arbitrary",)),
        name="moe",
    )(blk, e_lo, e_hi, n_valid, n_tiles, xs, lw["w_e_gate"], lw["w_e_up"], lw["w_e_down"],
      lw["w_e_gate"], lw["w_e_up"], lw["w_e_down"], lw["ln2_g"], lw["ln2_b"])


def _tile_tables(counts, capacity, max_tiles):
    cnt = counts[:N_CLASSES, 0].astype(I32)
    tiles = (cnt + MOE_TILE - 1) // MOE_TILE
    ends = jnp.cumsum(tiles)
    starts = ends - tiles
    n_tiles = ends[-1]
    t = jnp.minimum(jnp.arange(max_tiles, dtype=I32), n_tiles - 1)
    cls = jnp.sum((ends[None, :] <= t[:, None]).astype(I32), axis=1)
    onehot = (jnp.arange(N_CLASSES, dtype=I32)[None, :] == cls[:, None]).astype(I32)
    within = t - jnp.sum(onehot * starts[None, :], axis=1)
    n_valid = jnp.clip(jnp.sum(onehot * cnt[None, :], axis=1) - within * MOE_TILE, 0, MOE_TILE)
    blk = cls * (capacity // MOE_TILE) + within
    grp = cls // len(PAIRS)
    pair = cls % len(PAIRS)
    lo = jnp.where(pair < 3, 0, jnp.where(pair < 5, 1, 2))
    hi = jnp.where(pair < 3, pair + 1, jnp.where(pair < 5, pair - 1, 3))
    return (blk, grp * EXPERTS_PER_GROUP + lo, grp * EXPERTS_PER_GROUP + hi, n_valid,
            n_tiles.reshape(1).astype(I32))


def _unpermute_kernel(dest_ref, ys_ref, out_ref, sem, *, tm):
    base = pl.program_id(0) * tm

    def fetch(r, carry):
        pltpu.make_async_copy(ys_ref.at[pl.ds(dest_ref[base + r], 1), :], out_ref.at[pl.ds(r, 1), :], sem).start()
        return carry

    lax.fori_loop(0, tm, fetch, 0, unroll=8)
    pltpu.make_async_copy(ys_ref.at[pl.ds(0, tm), :], out_ref, sem).wait()


def _unpermute(ys, dest):
    n = dest.shape[0]
    tm = _row_tile(n)
    grid_spec = pltpu.PrefetchScalarGridSpec(
        num_scalar_prefetch=1,
        grid=(n // tm,),
        in_specs=[pl.BlockSpec(memory_space=pl.ANY)],
        out_specs=pl.BlockSpec((tm, D_MODEL), lambda i, d: (i, 0)),
        scratch_shapes=[pltpu.SemaphoreType.DMA])
    return pl.pallas_call(
        functools.partial(_unpermute_kernel, tm=tm),
        grid_spec=grid_spec,
        out_shape=jax.ShapeDtypeStruct((n, D_MODEL), F32),
        compiler_params=_params(("arbitrary",)),
        name="unpermute",
    )(dest, ys)


def _prepare_weights(w_in, b_gate, hn_gain, w_pool, pool_scale, w_proj_a, w_proj_b, w_out, ln1_g, ln1_b,
                     ln2_g, ln2_b, w_router, b_router, w_e_gate, w_e_up, w_e_down):
    w_main = jnp.concatenate([w_in[:, :, :_G0], w_in[:, :, _U0:]], axis=2).astype(BF16)
    w_gate = w_in[:, :, _G0:_U0]
    w_gate_p = jnp.pad(w_gate, ((0, 0), (0, 0), (0, LANES - 2 * N_HEADS))).astype(BF16)
    w_gate_t = jnp.pad(jnp.swapaxes(w_gate, 1, 2), ((0, 0), (0, GATE_ROWS - 2 * N_HEADS), (0, 0))).astype(BF16)
    w_key_t = jnp.swapaxes(w_in[:, :, D_MLSTM:2 * D_MLSTM], 1, 2).astype(BF16)
    b_row = jnp.pad(b_gate, ((0, 0), (0, LANES - 2 * N_HEADS)))[:, None, :]
    b_col = jnp.pad(b_gate, ((0, 0), (0, GATE_ROWS - 2 * N_HEADS)))[:, :, None]
    wr = w_router.T.reshape(N_EXPERT_GROUPS, EXPERTS_PER_GROUP, D_MODEL).swapaxes(0, 1)
    wr = jnp.pad(wr, ((0, 0), (0, 8 - N_EXPERT_GROUPS), (0, 0))).reshape(ROUTER_ROWS, D_MODEL)
    br = b_router.reshape(N_EXPERT_GROUPS, EXPERTS_PER_GROUP).T
    br = jnp.pad(br, ((0, 0), (0, 8 - N_EXPERT_GROUPS))).reshape(ROUTER_ROWS, 1)
    layers = []
    for l in range(DEPTH):
        layers.append(dict(
            w_main=w_main[l], w_gate=w_gate_p[l], w_gate_t=w_gate_t[l], w_key_t=w_key_t[l],
            b_row=b_row[l], b_col=b_col[l],
            gain=hn_gain[l].reshape(1, D_MLSTM), w_pool=w_pool[l].astype(BF16),
            pool_scale=pool_scale[l].reshape(1, D_POOL), w_proj_a=w_proj_a[l].astype(BF16),
            w_proj_b=w_proj_b[l].astype(BF16), w_out=w_out[l].astype(BF16),
            ln1_g=ln1_g[l].reshape(1, D_MODEL), ln1_b=ln1_b[l].reshape(1, D_MODEL),
            ln2_g=ln2_g[l].reshape(1, D_MODEL), ln2_b=ln2_b[l].reshape(1, D_MODEL),
            w_router_t=wr, b_router_col=br,
            w_e_gate=w_e_gate[l].astype(BF16), w_e_up=w_e_up[l].astype(BF16), w_e_down=w_e_down[l].astype(BF16)))
    return layers


def kernel(x_prompt, x_sample, state_C, state_n, state_m, state_pool, w_in, b_gate, hn_gain, w_pool, pool_scale,
           w_proj_a, w_proj_b, w_out, ln1_g, ln1_b, ln2_g, ln2_b, w_router, b_router, w_e_gate, w_e_up, w_e_down):
    layers = _prepare_weights(w_in, b_gate, hn_gain, w_pool, pool_scale, w_proj_a, w_proj_b, w_out, ln1_g, ln1_b,
                              ln2_g, ln2_b, w_router, b_router, w_e_gate, w_e_up, w_e_down)
    n_pseq, p_len, _ = x_prompt.shape
    n_sseq, s_len, _ = x_sample.shape
    n_p = n_pseq * p_len
    n_s = n_sseq * s_len
    n = n_p + n_s
    capacity = -(-n // MOE_TILE) * MOE_TILE
    max_tiles = n // MOE_TILE + N_CLASSES
    x = jnp.concatenate([x_prompt.reshape(n_p, D_MODEL), x_sample.reshape(n_s, D_MODEL)], axis=0)
    np_, mp, bp, ns, ms, bs = [], [], [], [], [], []
    c_p = c_s = None
    for l, lw in enumerate(layers):
        q, k, v, o, u, ga, gb, gcol, grow, kt = _inproj(x, lw, n_p)
        hg_p, c_p, n1p, m1p = _mlstm_prompt(q, k, v, gcol, grow, o, lw["gain"], c_p, l, n_pseq, p_len)
        mtok = jnp.pad(jnp.repeat(state_m[l], s_len, axis=0), ((0, 0), (0, LANES - N_HEADS)))
        hg_s, c_s, n1s, m1s = _mlstm_sample(q, k, kt, v, gcol, grow, mtok, o, lw["gain"], state_C, state_n, c_s,
                                            l, n_p, n_sseq, s_len)
        pooled_p = _pool_prompt(u, n_pseq, p_len)
        pooled_s, nbuf = _pool_sample(u[n_p:].reshape(n_sseq, s_len, D_POOL), state_pool, l, PAST_LEN)
        dest, counts, xs = _mix(hg_p, hg_s, pooled_p, pooled_s.reshape(n_s, D_POOL), ga, gb, x, lw, capacity)
        ys = _moe(xs, _tile_tables(counts, capacity, max_tiles), lw, max_tiles)
        x = _unpermute(ys, dest[0])
        np_.append(n1p[:, :, :, 0])
        mp.append(m1p[:, :, 0])
        bp.append(u[:n_p].reshape(n_pseq, p_len, D_POOL)[:, p_len - POOL_BUF:])
        ns.append(n1s)
        ms.append(m1s[:, :, 0])
        bs.append(nbuf)
    st = jnp.stack
    return (x[:n_p].reshape(n_pseq, p_len, D_MODEL), x[n_p:].reshape(n_sseq, s_len, D_MODEL),
            c_p, st(np_), st(mp), st(bp), c_s, st(ns), st(ms), st(bs))
```

```python
import functools
import math

import jax
import jax.numpy as jnp
from jax import lax
from jax.experimental import pallas as pl
from jax.experimental.pallas import tpu as pltpu

F32 = jnp.float32
BF16 = jnp.bfloat16
I32 = jnp.int32

D_MODEL = 1024
N_HEADS = 4
HEAD_DIM = 256
D_MLSTM = N_HEADS * HEAD_DIM
POOL_WINDOWS = (2, 4, 8, 16)
POOL_GROUP_DIM = 128
D_POOL = len(POOL_WINDOWS) * POOL_GROUP_DIM
POOL_BUF = 15
N_EXPERTS = 16
N_EXPERT_GROUPS = 4
EXPERTS_PER_GROUP = 4
D_EXPERT = 512
DEPTH = 4
PAST_LEN = 16384
ALPHA = (2 * DEPTH) ** 0.25
LN_EPS = 1e-5
K_SCALE = HEAD_DIM ** -0.5

LANES = 128
GATE_ROWS = 16
ROUTER_ROWS = 32
VMEM_LIMIT = 52 * 1024 * 1024
MLSTM_PROMPT_CHUNK = 256
MLSTM_STRIP = 256
SAMPLE_BLOCK_ROWS = 128
SAMPLE_WINDOW = 16

PAIRS = ((0, 1), (0, 2), (0, 3), (1, 2), (1, 3), (2, 3))
N_CLASSES = N_EXPERT_GROUPS * len(PAIRS)
CLASS_ROWS = 32
MOE_TILE = 256
ROW_EXT = D_MODEL + LANES

_G0 = 4 * D_MLSTM
_U0 = _G0 + 2 * N_HEADS
_MAIN_SEGS = ((0, 1024), (1024, 2048), (2048, 3072), (3072, 4096), (4096, 4608), (4608, 5632), (5632, 6656))


def _params(sem, **kw):
    return pltpu.CompilerParams(dimension_semantics=sem, vmem_limit_bytes=VMEM_LIMIT, **kw)


def _const_spec(shape):
    nd = len(shape)
    return pl.BlockSpec(shape, lambda *_: (0,) * nd, pipeline_mode=pl.Buffered(1))


def _row_tile(n, cap=512):
    t = cap
    while n % t:
        t //= 2
    return t


def _inproj_kernel(x_ref, w_ref, wg_ref, wgt_ref, wkt_ref, brow_ref, bcol_ref,
                   q_ref, k_ref, v_ref, o_ref, u_ref, ga_ref, gb_ref, gcol_ref, grow_ref, kt_ref,
                   *, n_prompt_tiles):
    _inproj_body(x_ref[...].astype(BF16), w_ref, wg_ref, wgt_ref, wkt_ref, brow_ref, bcol_ref,
                 q_ref, k_ref, v_ref, o_ref, u_ref, ga_ref, gb_ref, gcol_ref, grow_ref, kt_ref, n_prompt_tiles)


def _inproj_gather_kernel(dest_ref, ys_ref, w_ref, wg_ref, wgt_ref, wkt_ref, brow_ref, bcol_ref,
                          q_ref, k_ref, v_ref, o_ref, u_ref, ga_ref, gb_ref, gcol_ref, grow_ref, kt_ref, x_ref,
                          xbuf0_ref, xbuf1_ref, sem, *, n_prompt_tiles, n_tiles, tm):
    i = pl.program_id(0)
    bufs = (xbuf0_ref, xbuf1_ref)

    def request(tile, slot, unrolled):
        def one(r, carry=0):
            pltpu.make_async_copy(ys_ref.at[pl.ds(dest_ref[tile * tm + r], 1), :],
                                  bufs[slot].at[pl.ds(r, 1), :], sem.at[slot]).start()
            return carry
        if unrolled:
            for r in range(tm):
                one(r)
        else:
            lax.fori_loop(0, tm, one, 0, unroll=8)

    arrived = lambda slot: pltpu.make_async_copy(ys_ref.at[pl.ds(0, tm), :], bufs[slot], sem.at[slot])

    @pl.when(i == 0)
    def _():
        request(0, 0, False)

    def step(cur):
        arrived(cur).wait()
        request(jnp.minimum(i + 1, n_tiles - 1), 1 - cur, True)
        x32 = bufs[cur][...]
        x_ref[...] = x32
        _inproj_body(x32.astype(BF16), w_ref, wg_ref, wgt_ref, wkt_ref, brow_ref, bcol_ref, q_ref, k_ref, v_ref,
                     o_ref, u_ref, ga_ref, gb_ref, gcol_ref, grow_ref, kt_ref, n_prompt_tiles)

        @pl.when(i == n_tiles - 1)
        def _():
            arrived(1 - cur).wait()

    for parity in (0, 1):
        pl.when(i % 2 == parity)(functools.partial(step, parity))


def _inproj_body(x, w_ref, wg_ref, wgt_ref, wkt_ref, brow_ref, bcol_ref,
                 q_ref, k_ref, v_ref, o_ref, u_ref, ga_ref, gb_ref, gcol_ref, grow_ref, kt_ref, n_prompt_tiles):
    @pl.when(pl.program_id(0) >= n_prompt_tiles)
    def _():
        kt = lax.dot_general(wkt_ref[...], x, (((1,), (1,)), ((), ())), preferred_element_type=F32)
        kt_ref[...] = (kt * K_SCALE).astype(BF16)

    def seg(i):
        lo, hi = _MAIN_SEGS[i]
        return jnp.dot(x, w_ref[:, lo:hi], preferred_element_type=F32)

    q_ref[...] = seg(0).astype(BF16)
    k_ref[...] = (seg(1) * K_SCALE).astype(BF16)
    v_ref[...] = seg(2).astype(BF16)
    o_ref[...] = seg(3)
    u_ref[...] = seg(4)
    ga_ref[...] = seg(5)
    gb_ref[...] = seg(6)
    g = jnp.dot(x, wg_ref[...], preferred_element_type=F32) + brow_ref[...]
    lane = lax.broadcasted_iota(I32, g.shape, 1)
    gcol_ref[...] = jnp.where(lane < N_HEADS, g, jax.nn.log_sigmoid(g))
    gt = lax.dot_general(wgt_ref[...], x, (((1,), (1,)), ((), ())), preferred_element_type=F32) + bcol_ref[...]
    sub = lax.broadcasted_iota(I32, gt.shape, 0)
    grow_ref[...] = jnp.where(sub < N_HEADS, gt, jax.nn.log_sigmoid(gt))


def _inproj(x, lw, n_prompt, dest=None):
    n = x.shape[0] if dest is None else dest.shape[0]
    tm = _row_tile(math.gcd(n_prompt, n - n_prompt))
    ntp = n_prompt // tm
    row = lambda w: pl.BlockSpec((tm, w), lambda i, *_: (i, 0))
    consts = (lw["w_main"], lw["w_gate"], lw["w_gate_t"], lw["w_key_t"], lw["b_row"], lw["b_col"])
    const_specs = [pl.BlockSpec(c.shape, lambda i, *_, nd=c.ndim: (0,) * nd, pipeline_mode=pl.Buffered(1))
                   for c in consts]
    out_shape = [
        jax.ShapeDtypeStruct((n, D_MLSTM), BF16), jax.ShapeDtypeStruct((n, D_MLSTM), BF16),
        jax.ShapeDtypeStruct((n, D_MLSTM), BF16), jax.ShapeDtypeStruct((n, D_MLSTM), F32),
        jax.ShapeDtypeStruct((n, D_POOL), F32), jax.ShapeDtypeStruct((n, D_MODEL), F32),
        jax.ShapeDtypeStruct((n, D_MODEL), F32), jax.ShapeDtypeStruct((n, LANES), F32),
        jax.ShapeDtypeStruct((GATE_ROWS, n), F32), jax.ShapeDtypeStruct((D_MLSTM, n - n_prompt), BF16)]
    out_specs = [row(D_MLSTM), row(D_MLSTM), row(D_MLSTM), row(D_MLSTM), row(D_POOL), row(D_MODEL),
                 row(D_MODEL), row(LANES), pl.BlockSpec((GATE_ROWS, tm), lambda i, *_: (0, i)),
                 pl.BlockSpec((D_MLSTM, tm), lambda i, *_: (0, jnp.maximum(i - ntp, 0)))]
    if dest is None:
        return pl.pallas_call(
            functools.partial(_inproj_kernel, n_prompt_tiles=ntp),
            grid=(n // tm,),
            in_specs=[row(D_MODEL)] + const_specs,
            out_specs=out_specs,
            out_shape=out_shape,
            compiler_params=_params(("arbitrary",)),
            name="inproj",
        )(x, *consts)
    grid_spec = pltpu.PrefetchScalarGridSpec(
        num_scalar_prefetch=1,
        grid=(n // tm,),
        in_specs=[pl.BlockSpec(memory_space=pl.ANY)] + const_specs,
        out_specs=out_specs + [row(D_MODEL)],
        scratch_shapes=[pltpu.VMEM((tm, D_MODEL), F32), pltpu.VMEM((tm, D_MODEL), F32),
                        pltpu.SemaphoreType.DMA((2,))])
    return pl.pallas_call(
        functools.partial(_inproj_gather_kernel, n_prompt_tiles=ntp, n_tiles=n // tm, tm=tm),
        grid_spec=grid_spec,
        out_shape=out_shape + [jax.ShapeDtypeStruct((n, D_MODEL), F32)],
        compiler_params=_params(("arbitrary",)),
        name="inproj_gather",
    )(dest, x, *consts)


def _head_out(hval, o, gain):
    mu = jnp.mean(hval, axis=1, keepdims=True)
    xc = hval - mu
    var = jnp.mean(xc * xc, axis=1, keepdims=True)
    return (jax.nn.sigmoid(o) * (xc * lax.rsqrt(var + LN_EPS) * gain)).astype(BF16)


def _split3(x):
    x1 = x.astype(BF16)
    r1 = x - x1.astype(F32)
    x2 = r1.astype(BF16)
    x3 = (r1 - x2.astype(F32)).astype(BF16)
    return x1, x2, x3


def _mlstm_prompt_kernel(*refs, rows, strip, chained):
    if chained:
        q_ref, k_ref, v_ref, gcol_ref, grow_ref, o_ref, gain_ref, _, hg_ref, c_ref, n_ref, m_ref = refs
    else:
        q_ref, k_ref, v_ref, gcol_ref, grow_ref, o_ref, gain_ref, hg_ref, c_ref, n_ref, m_ref = refs

    @pl.when(pl.program_id(1) == 0)
    def _():
        c_ref[...] = jnp.zeros_like(c_ref)
        n_ref[...] = jnp.zeros_like(n_ref)
        m_ref[...] = jnp.zeros_like(m_ref)

    t_idx = lax.broadcasted_iota(I32, (rows, 1), 0)
    s_idx = lax.broadcasted_iota(I32, (1, rows), 1)
    mask = s_idx <= t_idx
    lower = jnp.where(mask, 1.0, 0.0).astype(BF16)
    upper = jnp.where(t_idx <= s_idx, 1.0, 0.0).astype(BF16)
    gcol = gcol_ref[...]
    grow = grow_ref[...]
    bcol = sum(jnp.dot(lower, p, preferred_element_type=F32) for p in _split3(gcol))
    brow = sum(jnp.dot(p, upper, preferred_element_type=F32) for p in _split3(grow))
    ones = jnp.ones((rows, LANES), BF16)
    neg_inf = jnp.float32(-jnp.inf)
    for h in range(N_HEADS):
        sl = slice(h * HEAD_DIM, (h + 1) * HEAD_DIM)
        k = k_ref[:, sl]
        v = v_ref[:, sl]
        li_c = gcol[:, h:h + 1]
        b_c = bcol[:, N_HEADS + h:N_HEADS + h + 1]
        b_r = brow[N_HEADS + h:N_HEADS + h + 1, :]
        g_r = grow[h:h + 1, :] - b_r
        m_prev = m_ref[0, h:h + 1, 0:1]
        c_prev = c_ref[0, h]
        n_prev = n_ref[0, h]
        cn = jnp.concatenate([c_prev.astype(BF16), n_prev.astype(BF16)], axis=1)
        vo = jnp.concatenate([v, ones], axis=1)
        for r0 in range(0, rows, strip):
            rs = slice(r0, r0 + strip)
            mk = mask[rs]
            q = q_ref[rs, sl]
            top = jnp.maximum(m_prev, jnp.max(jnp.where(mk, g_r, neg_inf), axis=1, keepdims=True))
            s_inter = jnp.exp(m_prev - top)
            qk = lax.dot_general(q, k, (((1,), (1,)), ((), ())), preferred_element_type=F32)
            s = qk * jnp.exp(jnp.where(mk, g_r - top, neg_inf))
            inter = jnp.dot(q, cn, preferred_element_type=F32)
            intra = jnp.dot(s.astype(BF16), vo, preferred_element_type=F32)
            num = s_inter * inter[:, 0:HEAD_DIM] + intra[:, 0:HEAD_DIM]
            den = s_inter * inter[:, HEAD_DIM:HEAD_DIM + 1] + intra[:, HEAD_DIM:HEAD_DIM + 1]
            hval = num / jnp.maximum(jnp.abs(den), jnp.exp(-(b_c[rs] + top)))
            hg_ref[rs, sl] = _head_out(hval, o_ref[rs, sl], gain_ref[:, sl])
        b_last = b_r[:, rows - 1:rows]
        d_last = b_last - b_c + li_c
        m_new = jnp.maximum(b_last + m_prev, jnp.max(d_last, axis=0, keepdims=True))
        kw = (k.astype(F32) * jnp.exp(d_last - m_new)).astype(BF16)
        s_last = jnp.exp(b_last + m_prev - m_new)
        upd = lax.dot_general(kw, vo, (((0,), (0,)), ((), ())), preferred_element_type=F32)
        c_ref[0, h] = s_last * c_prev + upd[:, 0:HEAD_DIM]
        n_ref[0, h] = s_last * n_prev + upd[:, HEAD_DIM:]
        m_ref[0, h:h + 1, :] = jnp.broadcast_to(m_new, (1, LANES))


def _mlstm_sample_kernel(*refs, seq_len, window, block, chained):
    if chained:
        (q_ref, k_ref, kt_ref, v_ref, gcol_ref, grow_ref, mtok_ref, o_ref, gain_ref, c0_ref, n0_ref, _,
         hg_ref, c1_ref, n1_ref, m1_ref) = refs
    else:
        (q_ref, k_ref, kt_ref, v_ref, gcol_ref, grow_ref, mtok_ref, o_ref, gain_ref, c0_ref, n0_ref,
         hg_ref, c1_ref, n1_ref, m1_ref) = refs
    n_seq = window // seq_len
    shift = int(math.log2(seq_len))
    w0 = (pl.program_id(0) % (block // window)) * window
    r_idx = lax.broadcasted_iota(I32, (window, 1), 0)
    t_idx = w0 + r_idx
    s_idx = lax.broadcasted_iota(I32, (1, block), 1)
    same = jnp.right_shift(t_idx, shift) == jnp.right_shift(s_idx, shift)
    mask = same & (s_idx <= t_idx)
    mask_t = same & (t_idx <= s_idx)
    r_seq = jnp.right_shift(r_idx, shift)
    l_seq = jnp.right_shift(s_idx - w0, shift)
    gcol = gcol_ref[...]
    grow = grow_ref[...]
    mtok = mtok_ref[...]
    neg_inf = jnp.float32(-jnp.inf)
    for h in range(N_HEADS):
        sl = slice(h * HEAD_DIM, (h + 1) * HEAD_DIM)
        q = q_ref[:, sl]
        k = k_ref[:, sl]
        kt = kt_ref[sl, :]
        v = v_ref[:, sl]
        li_r = grow[h:h + 1, :]
        lf_r = grow[N_HEADS + h:N_HEADS + h + 1, :]
        li_c = gcol[:, h:h + 1]
        lf_c = gcol[:, N_HEADS + h:N_HEADS + h + 1]
        m_c = mtok[:, h:h + 1]
        b_c = jnp.sum(jnp.where(mask, lf_r, 0.0), axis=1, keepdims=True)
        b_r = jnp.sum(jnp.where(mask_t, lf_c, 0.0), axis=0, keepdims=True)
        dmat = jnp.where(mask, b_c - b_r + li_r, neg_inf)
        inter = b_c + m_c
        m_t = jnp.maximum(inter, jnp.max(dmat, axis=1, keepdims=True))
        s_inter = jnp.exp(inter - m_t)
        s = jnp.dot(q, kt, preferred_element_type=F32) * jnp.exp(dmat - m_t)
        intra = jnp.dot(s.astype(BF16), v, preferred_element_type=F32)
        qf = q.astype(F32)
        qc = qn = None
        for j in range(n_seq):
            qc_j = jnp.dot(q, c0_ref[j, h].astype(BF16), preferred_element_type=F32)
            qn_j = jnp.sum(qf * n0_ref[j, h:h + 1, :], axis=1, keepdims=True)
            qc = qc_j if j == 0 else jnp.where(r_seq == j, qc_j, qc)
            qn = qn_j if j == 0 else jnp.where(r_seq == j, qn_j, qn)
        num = s_inter * qc + intra
        den = s_inter * qn + jnp.sum(s, axis=1, keepdims=True)
        hval = num / jnp.maximum(jnp.abs(den), jnp.exp(-m_t))
        hg_ref[:, sl] = _head_out(hval, o_ref[:, sl], gain_ref[:, sl])
        kf = k.astype(F32)
        ktf = kt.astype(F32)
        for j in range(n_seq):
            lsel = l_seq == j
            m_j = mtok[j * seq_len:j * seq_len + 1, h:h + 1]
            b_last = jnp.sum(jnp.where(lsel, lf_r, 0.0), axis=1, keepdims=True)
            d_r = jnp.where(lsel, b_last - b_r + li_r, neg_inf)
            d_c = jnp.where(r_seq == j, b_last - b_c + li_c, neg_inf)
            m_new = jnp.maximum(b_last + m_j, jnp.max(d_r, axis=1, keepdims=True))
            s_last = jnp.exp(b_last + m_j - m_new)
            kwt = (ktf * jnp.exp(d_r - m_new)).astype(BF16)
            c1_ref[j, h] = s_last * c0_ref[j, h] + jnp.dot(kwt, v, preferred_element_type=F32)
            n1_ref[j, h:h + 1, :] = (s_last * n0_ref[j, h:h + 1, :]
                                     + jnp.sum(kf * jnp.exp(d_c - m_new), axis=0, keepdims=True))
            m1_ref[j, h:h + 1, :] = jnp.broadcast_to(m_new, (1, LANES))


def _chain(c_all):
    if c_all is None:
        return [], []
    return [c_all], [pl.BlockSpec(memory_space=pl.ANY)]


def _mlstm_prompt(q, k, v, gcol, grow, o, gain, c_all, layer, n_seq, seq_len):
    n = n_seq * seq_len
    chunk = math.gcd(seq_len, MLSTM_PROMPT_CHUNK)
    nc = seq_len // chunk
    row = lambda w: pl.BlockSpec((chunk, w), lambda b, c: (b * nc + c, 0))
    st = lambda *tail: pl.BlockSpec((1,) + tail, lambda b, c: (b,) + (0,) * len(tail))
    extra, extra_specs = _chain(c_all)
    n_in = 7
    return pl.pallas_call(
        functools.partial(_mlstm_prompt_kernel, rows=chunk, strip=min(chunk, MLSTM_STRIP), chained=bool(extra)),
        grid=(n_seq, nc),
        in_specs=[row(D_MLSTM), row(D_MLSTM), row(D_MLSTM), row(LANES),
                  pl.BlockSpec((GATE_ROWS, chunk), lambda b, c: (0, b * nc + c)),
                  row(D_MLSTM), _const_spec(gain.shape)] + extra_specs,
        out_specs=(row(D_MLSTM),
                   pl.BlockSpec((None, 1, N_HEADS, HEAD_DIM, HEAD_DIM), lambda b, c: (layer, b, 0, 0, 0)),
                   st(N_HEADS, HEAD_DIM, LANES), st(N_HEADS, LANES)),
        out_shape=(jax.ShapeDtypeStruct((n, D_MLSTM), BF16),
                   jax.ShapeDtypeStruct((DEPTH, n_seq, N_HEADS, HEAD_DIM, HEAD_DIM), F32),
                   jax.ShapeDtypeStruct((n_seq, N_HEADS, HEAD_DIM, LANES), F32),
                   jax.ShapeDtypeStruct((n_seq, N_HEADS, LANES), F32)),
        input_output_aliases={n_in: 1} if extra else {},
        compiler_params=_params(("parallel", "arbitrary")),
        name="mlstm_prompt",
    )(q, k, v, gcol, grow, o, gain, *extra)


def _mlstm_sample(q, k, kt, v, gcol, grow, mtok, o, gain, c0, n0, c_all, layer, first_row, n_seq, seq_len):
    n = n_seq * seq_len
    window, block = SAMPLE_WINDOW, SAMPLE_BLOCK_ROWS
    assert n % block == 0 and first_row % block == 0 and window % seq_len == 0
    per_win = window // seq_len
    sub = block // window
    row = lambda w: pl.BlockSpec((window, w), lambda i: (first_row // window + i, 0))
    extra, extra_specs = _chain(c_all)
    n_in = 11
    return pl.pallas_call(
        functools.partial(_mlstm_sample_kernel, seq_len=seq_len, window=window, block=block, chained=bool(extra)),
        grid=(n // window,),
        in_specs=[row(D_MLSTM), row(D_MLSTM),
                  pl.BlockSpec((D_MLSTM, block), lambda i: (0, i // sub)),
                  pl.BlockSpec((block, D_MLSTM), lambda i: (first_row // block + i // sub, 0)),
                  row(LANES),
                  pl.BlockSpec((GATE_ROWS, block), lambda i: (0, first_row // block + i // sub)),
                  pl.BlockSpec((window, LANES), lambda i: (i, 0)),
                  row(D_MLSTM), _const_spec(gain.shape),
                  pl.BlockSpec((None, per_win, N_HEADS, HEAD_DIM, HEAD_DIM), lambda i: (layer, i, 0, 0, 0)),
                  pl.BlockSpec((None, per_win, N_HEADS, HEAD_DIM), lambda i: (layer, i, 0, 0))] + extra_specs,
        out_specs=(pl.BlockSpec((window, D_MLSTM), lambda i: (i, 0)),
                   pl.BlockSpec((None, per_win, N_HEADS, HEAD_DIM, HEAD_DIM), lambda i: (layer, i, 0, 0, 0)),
                   pl.BlockSpec((per_win, N_HEADS, HEAD_DIM), lambda i: (i, 0, 0)),
                   pl.BlockSpec((per_win, N_HEADS, LANES), lambda i: (i, 0, 0))),
        out_shape=(jax.ShapeDtypeStruct((n, D_MLSTM), BF16),
                   jax.ShapeDtypeStruct((DEPTH, n_seq, N_HEADS, HEAD_DIM, HEAD_DIM), F32),
                   jax.ShapeDtypeStruct((n_seq, N_HEADS, HEAD_DIM), F32),
                   jax.ShapeDtypeStruct((n_seq, N_HEADS, LANES), F32)),
        input_output_aliases={n_in: 1} if extra else {},
        compiler_params=_params(("parallel",)),
        name="mlstm_sample",
    )(q, k, kt, v, gcol, grow, mtok, o, gain, c0, n0, *extra)


def _pool_prompt_kernel(u_ref, prev_ref, out_ref, ext_ref, *, tm, tiles_per_seq):
    tile = pl.program_id(0) % tiles_per_seq
    head = 16
    ext_ref[0:head, :] = jnp.where(tile == 0, 0.0, prev_ref[...])
    ext_ref[head:, :] = u_ref[...]
    pos = tile * tm + lax.broadcasted_iota(I32, (tm, 1), 0)
    for g, w in enumerate(POOL_WINDOWS):
        sl = slice(g * POOL_GROUP_DIM, (g + 1) * POOL_GROUP_DIM)
        acc = ext_ref[head:head + tm, sl]
        for d in range(1, w):
            acc = acc + ext_ref[head - d:head - d + tm, sl]
        cnt = jnp.minimum(pos + 1, w).astype(F32)
        out_ref[:, sl] = (acc / cnt - u_ref[:, sl]).astype(BF16)


def _pool_prompt(u, n_seq, seq_len):
    n = n_seq * seq_len
    tm = _row_tile(seq_len)
    head = 16
    return pl.pallas_call(
        functools.partial(_pool_prompt_kernel, tm=tm, tiles_per_seq=seq_len // tm),
        grid=(n // tm,),
        in_specs=[pl.BlockSpec((tm, D_POOL), lambda i: (i, 0)),
                  pl.BlockSpec((head, D_POOL), lambda i: (jnp.maximum(i * (tm // head) - 1, 0), 0))],
        out_specs=pl.BlockSpec((tm, D_POOL), lambda i: (i, 0)),
        out_shape=jax.ShapeDtypeStruct((n, D_POOL), BF16),
        scratch_shapes=[pltpu.VMEM((head + tm, D_POOL), F32)],
        compiler_params=_params(("parallel",)),
        name="pool_prompt",
    )(u, u)


def _pool_sample_kernel(u_ref, buf_ref, out_ref, nbuf_ref, ext_ref, *, seq_len, start):
    ext_ref[:, 0:POOL_BUF, :] = buf_ref[...]
    ext_ref[:, POOL_BUF:POOL_BUF + seq_len, :] = u_ref[...]
    pos = start + lax.broadcasted_iota(I32, (1, seq_len, 1), 1)
    for g, w in enumerate(POOL_WINDOWS):
        sl = slice(g * POOL_GROUP_DIM, (g + 1) * POOL_GROUP_DIM)
        acc = ext_ref[:, POOL_BUF:POOL_BUF + seq_len, sl]
        for d in range(1, w):
            acc = acc + ext_ref[:, POOL_BUF - d:POOL_BUF - d + seq_len, sl]
        cnt = jnp.minimum(pos + 1, w).astype(F32)
        out_ref[:, :, sl] = (acc / cnt - u_ref[:, :, sl]).astype(BF16)
    nbuf_ref[...] = ext_ref[:, seq_len:seq_len + POOL_BUF, :]


def _pool_sample(u3, buf, layer, start):
    n_seq, seq_len, _ = u3.shape
    bs = _row_tile(n_seq, 32)
    spec = lambda r: pl.BlockSpec((bs, r, D_POOL), lambda i: (i, 0, 0))
    return pl.pallas_call(
        functools.partial(_pool_sample_kernel, seq_len=seq_len, start=start),
        grid=(n_seq // bs,),
        in_specs=[spec(seq_len), pl.BlockSpec((None, bs, POOL_BUF, D_POOL), lambda i: (layer, i, 0, 0))],
        out_specs=(spec(seq_len), spec(POOL_BUF)),
        out_shape=(jax.ShapeDtypeStruct((n_seq, seq_len, D_POOL), BF16),
                   jax.ShapeDtypeStruct((n_seq, POOL_BUF, D_POOL), F32)),
        scratch_shapes=[pltpu.VMEM((bs, POOL_BUF + seq_len + 5, D_POOL), F32)],
        compiler_params=_params(("parallel",)),
        name="pool_sample",
    )(u3, buf)


def _layer_norm(y, g, b):
    mu = jnp.mean(y, axis=1, keepdims=True)
    yc = y - mu
    var = jnp.mean(yc * yc, axis=1, keepdims=True)
    return yc * lax.rsqrt(var + LN_EPS) * g + b


def _route(logits_t):
    tokens = logits_t.shape[1]
    grp = lax.broadcasted_iota(I32, (8, tokens), 0)
    live = grp < N_EXPERT_GROUPS
    neg_inf = jnp.float32(-jnp.inf)
    lm = [jnp.where(live, logits_t[8 * m:8 * m + 8, :], neg_inf) for m in range(EXPERTS_PER_GROUP)]
    mx = jnp.max(jnp.maximum(jnp.maximum(lm[0], lm[1]), jnp.maximum(lm[2], lm[3])), axis=0, keepdims=True)
    ex = [jnp.exp(l - mx) for l in lm]
    tot = jnp.sum(ex[0] + ex[1] + ex[2] + ex[3], axis=0, keepdims=True)
    p = [e / tot for e in ex]
    top1 = jnp.maximum(jnp.maximum(p[0], p[1]), jnp.maximum(p[2], p[3]))
    i1 = jnp.where(p[0] == top1, 0, jnp.where(p[1] == top1, 1, jnp.where(p[2] == top1, 2, 3)))
    r = [jnp.where(i1 == m, -1.0, p[m]) for m in range(EXPERTS_PER_GROUP)]
    top2 = jnp.maximum(jnp.maximum(r[0], r[1]), jnp.maximum(r[2], r[3]))
    i2 = jnp.where(r[0] == top2, 0, jnp.where(r[1] == top2, 1, jnp.where(r[2] == top2, 2, 3)))
    gscore = jnp.where(live, top1 + top2, neg_inf)
    gmax = jnp.max(gscore, axis=0, keepdims=True)
    gsel = jnp.min(jnp.where(gscore == gmax, grp, 8), axis=0, keepdims=True)
    chosen = grp == gsel
    tsum = top1 + top2
    w1 = top1 / tsum
    w2 = top2 / tsum
    first_is_lo = i1 < i2
    lo = jnp.minimum(i1, i2)
    hi = jnp.maximum(i1, i2)
    pair = jnp.where(lo == 0, hi - 1, jnp.where(lo == 1, hi + 1, 5))
    pick = lambda a: jnp.sum(jnp.where(chosen, a, jnp.zeros_like(a)), axis=0, keepdims=True)
    cls = pick(grp * len(PAIRS) + pair)
    w_lo = pick(jnp.where(first_is_lo, w1, w2))
    w_hi = pick(jnp.where(first_is_lo, w2, w1))
    return cls, w_lo, w_hi


def _mix_kernel(hgp_ref, hgs_ref, plp_ref, pls_ref, ga_ref, gb_ref, x_ref, wpool_ref, pscale_ref, wa_ref, wb_ref,
                wout_ref, g1_ref, b1_ref, wr_ref, br_ref,
                dest_ref, cnt_ref, xs_ref,
                rows0_ref, rows1_ref, dvm_ref, dsm0_ref, dsm1_ref, carry_ref, row_sem, idx_sem,
                *, tm, n_prompt_tiles, n_tiles, capacity):
    i = pl.program_id(0)
    rows = (rows0_ref, rows1_ref)
    dsm = (dsm0_ref, dsm1_ref)
    spare = N_CLASSES * capacity
    sent = lambda s: pltpu.make_async_copy(rows[s], xs_ref.at[pl.ds(0, tm), :], row_sem.at[s])

    def send(s, unrolled):
        def one(r, carry=0):
            pltpu.make_async_copy(rows[s].at[pl.ds(r, 1), :], xs_ref.at[pl.ds(dsm[s][r], 1), :],
                                  row_sem.at[s]).start()
            return carry
        if unrolled:
            for r in range(tm):
                one(r)
        else:
            lax.fori_loop(0, tm, one, 0, unroll=8)

    @pl.when(i == 0)
    def _():
        carry_ref[...] = jnp.zeros_like(carry_ref)
        rows0_ref[...] = jnp.zeros_like(rows0_ref)
        rows1_ref[...] = jnp.zeros_like(rows1_ref)

        def spare_rows(r, carry):
            dsm0_ref[r] = spare + r
            dsm1_ref[r] = spare + tm + r
            return carry

        lax.fori_loop(0, tm, spare_rows, 0)
        send(0, False)

    for parity in (0, 1):
        pl.when(i % 2 == parity)(functools.partial(
            _mix_step, parity, i, hgp_ref, hgs_ref, plp_ref, pls_ref, ga_ref, gb_ref, x_ref, wpool_ref, pscale_ref,
            wa_ref, wb_ref, wout_ref, g1_ref, b1_ref, wr_ref, br_ref, dest_ref, cnt_ref, rows, dvm_ref, dsm,
            carry_ref, idx_sem, send, sent, tm, n_prompt_tiles, n_tiles, capacity))


def _mix_step(cur, i, hgp_ref, hgs_ref, plp_ref, pls_ref, ga_ref, gb_ref, x_ref, wpool_ref, pscale_ref, wa_ref,
              wb_ref, wout_ref, g1_ref, b1_ref, wr_ref, br_ref, dest_ref, cnt_ref, rows, dvm_ref, dsm, carry_ref,
              idx_sem, send, sent, tm, n_prompt_tiles, n_tiles, capacity):
    send(1 - cur, True)
    is_prompt = i < n_prompt_tiles
    hg = jnp.where(is_prompt, hgp_ref[...], hgs_ref[...])
    pooled = jnp.where(is_prompt, plp_ref[...], pls_ref[...])
    ya = jnp.dot(hg, wa_ref[...], preferred_element_type=F32)
    parts = []
    for g in range(len(POOL_WINDOWS)):
        sl = slice(g * POOL_GROUP_DIM, (g + 1) * POOL_GROUP_DIM)
        parts.append(jnp.dot(pooled[:, sl], wpool_ref[g], preferred_element_type=F32))
    pl_lin = jnp.concatenate(parts, axis=1) * pscale_ref[...]
    yb = jnp.dot(pl_lin.astype(BF16), wb_ref[...], preferred_element_type=F32)
    mix = jax.nn.sigmoid(ga_ref[...]) * ya + jax.nn.sigmoid(gb_ref[...]) * yb
    res = jnp.dot(mix.astype(BF16), wout_ref[...], preferred_element_type=F32)
    x1 = _layer_norm(ALPHA * x_ref[...] + res, g1_ref[...], b1_ref[...])

    logits_t = lax.dot_general(wr_ref[...], x1, (((1,), (1,)), ((), ())), preferred_element_type=F32,
                               precision=lax.Precision.HIGHEST) + br_ref[...]
    cls, w_lo, w_hi = _route(logits_t)

    onehot = lax.broadcasted_iota(I32, (CLASS_ROWS, tm), 0) == cls
    earlier = lax.broadcasted_iota(I32, (tm, tm), 0) < lax.broadcasted_iota(I32, (tm, tm), 1)
    before = jnp.dot(jnp.where(onehot, 1.0, 0.0).astype(BF16), jnp.where(earlier, 1.0, 0.0).astype(BF16),
                     preferred_element_type=F32)
    seen = carry_ref[:, 0:1]
    rank = jnp.sum(jnp.where(onehot, before + seen, 0.0), axis=0, keepdims=True)
    carry_ref[...] = carry_ref[...] + jnp.sum(jnp.where(onehot, 1.0, 0.0), axis=1, keepdims=True)
    cnt_ref[...] = carry_ref[...]
    dest = cls * capacity + rank.astype(I32)
    dest_ref[...] = jnp.broadcast_to(dest, (8, tm))

    sent(cur).wait()
    wrows = jnp.concatenate([w_lo, w_hi, jnp.zeros((LANES - 2, tm), F32)], axis=0)
    rows[cur][:, 0:D_MODEL] = x1
    rows[cur][:, D_MODEL:ROW_EXT] = wrows.T
    dvm_ref[...] = jnp.broadcast_to(dest, (8, tm))
    idx_copy = pltpu.make_async_copy(dvm_ref.at[0], dsm[cur], idx_sem)
    idx_copy.start()
    idx_copy.wait()

    @pl.when(i == n_tiles - 1)
    def _():
        send(cur, False)
        sent(cur).wait()
        sent(1 - cur).wait()


def _mix(hg_p, hg_s, pooled_p, pooled_s, ga, gb, x, lw, capacity):
    n = x.shape[0]
    n_p = hg_p.shape[0]
    tm = _row_tile(math.gcd(n_p, n - n_p))
    ntp = n_p // tm
    row = lambda w: pl.BlockSpec((tm, w), lambda i: (i, 0))
    prow = lambda w: pl.BlockSpec((tm, w), lambda i: (jnp.minimum(i, ntp - 1), 0))
    srow = lambda w: pl.BlockSpec((tm, w), lambda i: (jnp.maximum(i - ntp, 0), 0))
    consts = (lw["w_pool"], lw["pool_scale"], lw["w_proj_a"], lw["w_proj_b"], lw["w_out"],
              lw["ln1_g"], lw["ln1_b"], lw["w_router_t"], lw["b_router_col"])
    return pl.pallas_call(
        functools.partial(_mix_kernel, tm=tm, n_prompt_tiles=ntp, n_tiles=n // tm, capacity=capacity),
        grid=(n // tm,),
        in_specs=[prow(D_MLSTM), srow(D_MLSTM), prow(D_POOL), srow(D_POOL), row(D_MODEL), row(D_MODEL), row(D_MODEL)]
                 + [_const_spec(c.shape) for c in consts],
        out_specs=(pl.BlockSpec((8, tm), lambda i: (0, i)),
                   pl.BlockSpec((CLASS_ROWS, LANES), lambda i: (0, 0)),
                   pl.BlockSpec(memory_space=pl.ANY)),
        out_shape=(jax.ShapeDtypeStruct((8, n), I32),
                   jax.ShapeDtypeStruct((CLASS_ROWS, LANES), F32),
                   jax.ShapeDtypeStruct((N_CLASSES * capacity + 2 * tm, ROW_EXT), F32)),
        scratch_shapes=[pltpu.VMEM((tm, ROW_EXT), F32), pltpu.VMEM((tm, ROW_EXT), F32), pltpu.VMEM((8, tm), I32),
                        pltpu.SMEM((tm,), I32), pltpu.SMEM((tm,), I32), pltpu.VMEM((CLASS_ROWS, LANES), F32),
                        pltpu.SemaphoreType.DMA((2,)), pltpu.SemaphoreType.DMA],
        compiler_params=_params(("arbitrary",)),
        name="mix",
    )(hg_p, hg_s, pooled_p, pooled_s, ga, gb, x, *consts)


def _moe_kernel(blk_ref, elo_ref, ehi_ref, nvalid_ref, ntiles_ref,
                xs_ref, wg_lo, wu_lo, wd_lo, wg_hi, wu_hi, wd_hi, g2_ref, b2_ref, ys_ref):
    i = pl.program_id(0)

    @pl.when(i < ntiles_ref[0])
    def _():
        valid = lax.broadcasted_iota(I32, (MOE_TILE, 1), 0) < nvalid_ref[i]
        xe = xs_ref[...]
        x = jnp.where(valid, xe[:, 0:D_MODEL], 0.0)
        w_lo = jnp.where(valid, xe[:, D_MODEL:D_MODEL + 1], 0.0)
        w_hi = jnp.where(valid, xe[:, D_MODEL + 1:D_MODEL + 2], 0.0)
        xb = x.astype(BF16)

        def expert(wg, wu, wd, w):
            g = jnp.dot(xb, wg[0], preferred_element_type=F32)
            u = jnp.dot(xb, wu[0], preferred_element_type=F32)
            hid = (g * jax.nn.sigmoid(g)) * u * w
            return jnp.dot(hid.astype(BF16), wd[0], preferred_element_type=F32)

        y = expert(wg_lo, wu_lo, wd_lo, w_lo) + expert(wg_hi, wu_hi, wd_hi, w_hi)
        ys_ref[...] = _layer_norm(ALPHA * x + y, g2_ref[...], b2_ref[...])


def _moe(xs, tables, lw, max_tiles):
    blk, e_lo, e_hi, n_valid, n_tiles = tables
    up = lambda sel: pl.BlockSpec((1, D_MODEL, D_EXPERT), lambda i, b, lo, hi, nv, nt: ((lo, hi)[sel][i], 0, 0))
    down = lambda sel: pl.BlockSpec((1, D_EXPERT, D_MODEL), lambda i, b, lo, hi, nv, nt: ((lo, hi)[sel][i], 0, 0))
    const = lambda shape: pl.BlockSpec(shape, lambda i, *_: (0,) * len(shape), pipeline_mode=pl.Buffered(1))
    grid_spec = pltpu.PrefetchScalarGridSpec(
        num_scalar_prefetch=5,
        grid=(max_tiles,),
        in_specs=[pl.BlockSpec((MOE_TILE, ROW_EXT), lambda i, b, *_: (b[i], 0)),
                  up(0), up(0), down(0), up(1), up(1), down(1),
                  const(lw["ln2_g"].shape), const(lw["ln2_b"].shape)],
        out_specs=pl.BlockSpec((MOE_TILE, D_MODEL), lambda i, b, *_: (b[i], 0)))
    return pl.pallas_call(
        _moe_kernel,
        grid_spec=grid_spec,
        out_shape=jax.ShapeDtypeStruct((xs.shape[0], D_MODEL), F32),
        compiler_params=_params(("arbitrary",)),
        name="moe",
    )(blk, e_lo, e_hi, n_valid, n_tiles, xs, lw["w_e_gate"], lw["w_e_up"], lw["w_e_down"],
      lw["w_e_gate"], lw["w_e_up"], lw["w_e_down"], lw["ln2_g"], lw["ln2_b"])


def _tile_tables(counts, capacity, max_tiles):
    cnt = counts[:N_CLASSES, 0].astype(I32)
    tiles = (cnt + MOE_TILE - 1) // MOE_TILE
    ends = jnp.cumsum(tiles)
    starts = ends - tiles
    n_tiles = ends[-1]
    t = jnp.minimum(jnp.arange(max_tiles, dtype=I32), jnp.maximum(n_tiles - 1, 0))
    cls = jnp.sum((ends[None, :] <= t[:, None]).astype(I32), axis=1)
    onehot = (jnp.arange(N_CLASSES, dtype=I32)[None, :] == cls[:, None]).astype(I32)
    within = t - jnp.sum(onehot * starts[None, :], axis=1)
    n_valid = jnp.clip(jnp.sum(onehot * cnt[None, :], axis=1) - within * MOE_TILE, 0, MOE_TILE)
    blk = cls * (capacity // MOE_TILE) + within
    grp = cls // len(PAIRS)
    pair = cls % len(PAIRS)
    lo = jnp.where(pair < 3, 0, jnp.where(pair < 5, 1, 2))
    hi = jnp.where(pair < 3, pair + 1, jnp.where(pair < 5, pair - 1, 3))
    return (blk, grp * EXPERTS_PER_GROUP + lo, grp * EXPERTS_PER_GROUP + hi, n_valid,
            n_tiles.reshape(1).astype(I32))


def _unpermute_kernel(dest_ref, ys_ref, out_ref, sem, *, tm):
    base = pl.program_id(0) * tm

    def fetch(r, carry):
        pltpu.make_async_copy(ys_ref.at[pl.ds(dest_ref[base + r], 1), :], out_ref.at[pl.ds(r, 1), :], sem).start()
        return carry

    lax.fori_loop(0, tm, fetch, 0, unroll=8)
    pltpu.make_async_copy(ys_ref.at[pl.ds(0, tm), :], out_ref, sem).wait()


def _unpermute(ys, dest):
    n = dest.shape[0]
    tm = _row_tile(n)
    grid_spec = pltpu.PrefetchScalarGridSpec(
        num_scalar_prefetch=1,
        grid=(n // tm,),
        in_specs=[pl.BlockSpec(memory_space=pl.ANY)],
        out_specs=pl.BlockSpec((tm, D_MODEL), lambda i, d: (i, 0)),
        scratch_shapes=[pltpu.SemaphoreType.DMA])
    return pl.pallas_call(
        functools.partial(_unpermute_kernel, tm=tm),
        grid_spec=grid_spec,
        out_shape=jax.ShapeDtypeStruct((n, D_MODEL), F32),
        compiler_params=_params(("arbitrary",)),
        name="unpermute",
    )(dest, ys)


def _prepare_weights(w_in, b_gate, hn_gain, w_pool, pool_scale, w_proj_a, w_proj_b, w_out, ln1_g, ln1_b,
                     ln2_g, ln2_b, w_router, b_router, w_e_gate, w_e_up, w_e_down):
    w_main = jnp.concatenate([w_in[:, :, :_G0], w_in[:, :, _U0:]], axis=2).astype(BF16)
    w_gate = w_in[:, :, _G0:_U0]
    w_gate_p = jnp.pad(w_gate, ((0, 0), (0, 0), (0, LANES - 2 * N_HEADS))).astype(BF16)
    w_gate_t = jnp.pad(jnp.swapaxes(w_gate, 1, 2), ((0, 0), (0, GATE_ROWS - 2 * N_HEADS), (0, 0))).astype(BF16)
    w_key_t = jnp.swapaxes(w_in[:, :, D_MLSTM:2 * D_MLSTM], 1, 2).astype(BF16)
    b_row = jnp.pad(b_gate, ((0, 0), (0, LANES - 2 * N_HEADS)))[:, None, :]
    b_col = jnp.pad(b_gate, ((0, 0), (0, GATE_ROWS - 2 * N_HEADS)))[:, :, None]
    wr = w_router.T.reshape(N_EXPERT_GROUPS, EXPERTS_PER_GROUP, D_MODEL).swapaxes(0, 1)
    wr = jnp.pad(wr, ((0, 0), (0, 8 - N_EXPERT_GROUPS), (0, 0))).reshape(ROUTER_ROWS, D_MODEL)
    br = b_router.reshape(N_EXPERT_GROUPS, EXPERTS_PER_GROUP).T
    br = jnp.pad(br, ((0, 0), (0, 8 - N_EXPERT_GROUPS))).reshape(ROUTER_ROWS, 1)
    layers = []
    for l in range(DEPTH):
        layers.append(dict(
            w_main=w_main[l], w_gate=w_gate_p[l], w_gate_t=w_gate_t[l], w_key_t=w_key_t[l],
            b_row=b_row[l], b_col=b_col[l],
            gain=hn_gain[l].reshape(1, D_MLSTM), w_pool=w_pool[l].astype(BF16),
            pool_scale=pool_scale[l].reshape(1, D_POOL), w_proj_a=w_proj_a[l].astype(BF16),
            w_proj_b=w_proj_b[l].astype(BF16), w_out=w_out[l].astype(BF16),
            ln1_g=ln1_g[l].reshape(1, D_MODEL), ln1_b=ln1_b[l].reshape(1, D_MODEL),
            ln2_g=ln2_g[l].reshape(1, D_MODEL), ln2_b=ln2_b[l].reshape(1, D_MODEL),
            w_router_t=wr, b_router_col=br,
            w_e_gate=w_e_gate[l].astype(BF16), w_e_up=w_e_up[l].astype(BF16), w_e_down=w_e_down[l].astype(BF16)))
    return layers


def kernel(x_prompt, x_sample, state_C, state_n, state_m, state_pool, w_in, b_gate, hn_gain, w_pool, pool_scale,
           w_proj_a, w_proj_b, w_out, ln1_g, ln1_b, ln2_g, ln2_b, w_router, b_router, w_e_gate, w_e_up, w_e_down):
    layers = _prepare_weights(w_in, b_gate, hn_gain, w_pool, pool_scale, w_proj_a, w_proj_b, w_out, ln1_g, ln1_b,
                              ln2_g, ln2_b, w_router, b_router, w_e_gate, w_e_up, w_e_down)
    n_pseq, p_len, _ = x_prompt.shape
    n_sseq, s_len, _ = x_sample.shape
    n_p = n_pseq * p_len
    n_s = n_sseq * s_len
    n = n_p + n_s
    capacity = -(-n // MOE_TILE) * MOE_TILE
    max_tiles = n // MOE_TILE + N_CLASSES
    x = jnp.concatenate([x_prompt.reshape(n_p, D_MODEL), x_sample.reshape(n_s, D_MODEL)], axis=0)
    np_, mp, bp, ns, ms, bs = [], [], [], [], [], []
    c_p = c_s = None
    ys = dest = None
    for l, lw in enumerate(layers):
        if l == 0:
            q, k, v, o, u, ga, gb, gcol, grow, kt = _inproj(x, lw, n_p)
        else:
            q, k, v, o, u, ga, gb, gcol, grow, kt, x = _inproj(ys, lw, n_p, dest[0])
        hg_p, c_p, n1p, m1p = _mlstm_prompt(q, k, v, gcol, grow, o, lw["gain"], c_p, l, n_pseq, p_len)
        mtok = jnp.pad(jnp.repeat(state_m[l], s_len, axis=0), ((0, 0), (0, LANES - N_HEADS)))
        hg_s, c_s, n1s, m1s = _mlstm_sample(q, k, kt, v, gcol, grow, mtok, o, lw["gain"], state_C, state_n, c_s,
                                            l, n_p, n_sseq, s_len)
        pooled_p = _pool_prompt(u, n_pseq, p_len)
        pooled_s, nbuf = _pool_sample(u[n_p:].reshape(n_sseq, s_len, D_POOL), state_pool, l, PAST_LEN)
        dest, counts, xs = _mix(hg_p, hg_s, pooled_p, pooled_s.reshape(n_s, D_POOL), ga, gb, x, lw, capacity)
        ys = _moe(xs, _tile_tables(counts, capacity, max_tiles), lw, max_tiles)
        np_.append(n1p[:, :, :, 0])
        mp.append(m1p[:, :, 0])
        bp.append(u[:n_p].reshape(n_pseq, p_len, D_POOL)[:, p_len - POOL_BUF:])
        ns.append(n1s)
        ms.append(m1s[:, :, 0])
        bs.append(nbuf)
    x = _unpermute(ys, dest[0])
    st = jnp.stack
    return (x[:n_p].reshape(n_pseq, p_len, D_MODEL), x[n_p:].reshape(n_sseq, s_len, D_MODEL),
            c_p, st(np_), st(mp), st(bp), c_s, st(ns), st(ms), st(bs))
```

```python
import functools
import math

import jax
import jax.numpy as jnp
from jax import lax
from jax.experimental import pallas as pl
from jax.experimental.pallas import tpu as pltpu

F32 = jnp.float32
BF16 = jnp.bfloat16
I32 = jnp.int32

D_MODEL = 1024
N_HEADS = 4
HEAD_DIM = 256
D_MLSTM = N_HEADS * HEAD_DIM
POOL_WINDOWS = (2, 4, 8, 16)
POOL_GROUP_DIM = 128
D_POOL = len(POOL_WINDOWS) * POOL_GROUP_DIM
POOL_BUF = 15
N_EXPERTS = 16
N_EXPERT_GROUPS = 4
EXPERTS_PER_GROUP = 4
D_EXPERT = 512
DEPTH = 4
PAST_LEN = 16384
ALPHA = (2 * DEPTH) ** 0.25
LN_EPS = 1e-5
K_SCALE = HEAD_DIM ** -0.5

LANES = 128
GATE_ROWS = 16
ROUTER_ROWS = 32
VMEM_LIMIT = 52 * 1024 * 1024
MLSTM_PROMPT_CHUNK = 256
MLSTM_STRIP = 256
SAMPLE_BLOCK_ROWS = 128
SAMPLE_WINDOW = 16

PAIRS = ((0, 1), (0, 2), (0, 3), (1, 2), (1, 3), (2, 3))
N_CLASSES = N_EXPERT_GROUPS * len(PAIRS)
CLASS_ROWS = 32
MOE_TILE = 256
ROW_EXT = D_MODEL + LANES

_G0 = 4 * D_MLSTM
_U0 = _G0 + 2 * N_HEADS
_MAIN_SEGS = ((0, 1024), (1024, 2048), (2048, 3072), (3072, 4096), (4096, 4608), (4608, 5632), (5632, 6656))


def _params(sem, **kw):
    return pltpu.CompilerParams(dimension_semantics=sem, vmem_limit_bytes=VMEM_LIMIT, **kw)


def _const_spec(shape):
    nd = len(shape)
    return pl.BlockSpec(shape, lambda *_: (0,) * nd, pipeline_mode=pl.Buffered(1))


def _layer_spec(stacked, layer):
    tail = stacked.shape[1:]
    return pl.BlockSpec((None,) + tail, lambda *_: (layer,) + (0,) * len(tail), pipeline_mode=pl.Buffered(1))


def _row_tile(n, cap=512):
    t = cap
    while n % t:
        t //= 2
    return t


def _inproj_kernel(x_ref, w_ref, wg_ref, wgt_ref, wkt_ref, brow_ref, bcol_ref,
                   q_ref, k_ref, v_ref, o_ref, u_ref, ga_ref, gb_ref, gcol_ref, grow_ref, kt_ref,
                   *, n_prompt_tiles):
    _inproj_body(x_ref[...].astype(BF16), w_ref, wg_ref, wgt_ref, wkt_ref, brow_ref, bcol_ref,
                 q_ref, k_ref, v_ref, o_ref, u_ref, ga_ref, gb_ref, gcol_ref, grow_ref, kt_ref, n_prompt_tiles)


def _inproj_gather_kernel(dest_ref, ys_ref, w_ref, wg_ref, wgt_ref, wkt_ref, brow_ref, bcol_ref,
                          q_ref, k_ref, v_ref, o_ref, u_ref, ga_ref, gb_ref, gcol_ref, grow_ref, kt_ref, x_ref,
                          xbuf0_ref, xbuf1_ref, sem, *, n_prompt_tiles, n_tiles, tm):
    i = pl.program_id(0)
    bufs = (xbuf0_ref, xbuf1_ref)

    def request(tile, slot, unrolled):
        def one(r, carry=0):
            pltpu.make_async_copy(ys_ref.at[pl.ds(dest_ref[tile * tm + r], 1), :],
                                  bufs[slot].at[pl.ds(r, 1), :], sem.at[slot]).start()
            return carry
        if unrolled:
            for r in range(tm):
                one(r)
        else:
            lax.fori_loop(0, tm, one, 0, unroll=8)

    arrived = lambda slot: pltpu.make_async_copy(ys_ref.at[pl.ds(0, tm), :], bufs[slot], sem.at[slot])

    @pl.when(i == 0)
    def _():
        request(0, 0, False)

    def step(cur):
        arrived(cur).wait()
        request(jnp.minimum(i + 1, n_tiles - 1), 1 - cur, True)
        x32 = bufs[cur][...]
        x_ref[...] = x32
        _inproj_body(x32.astype(BF16), w_ref, wg_ref, wgt_ref, wkt_ref, brow_ref, bcol_ref, q_ref, k_ref, v_ref,
                     o_ref, u_ref, ga_ref, gb_ref, gcol_ref, grow_ref, kt_ref, n_prompt_tiles)

        @pl.when(i == n_tiles - 1)
        def _():
            arrived(1 - cur).wait()

    for parity in (0, 1):
        pl.when(i % 2 == parity)(functools.partial(step, parity))


def _inproj_body(x, w_ref, wg_ref, wgt_ref, wkt_ref, brow_ref, bcol_ref,
                 q_ref, k_ref, v_ref, o_ref, u_ref, ga_ref, gb_ref, gcol_ref, grow_ref, kt_ref, n_prompt_tiles):
    @pl.when(pl.program_id(0) >= n_prompt_tiles)
    def _():
        kt = lax.dot_general(wkt_ref[...], x, (((1,), (1,)), ((), ())), preferred_element_type=F32)
        kt_ref[...] = (kt * K_SCALE).astype(BF16)

    def seg(i):
        lo, hi = _MAIN_SEGS[i]
        return jnp.dot(x, w_ref[:, lo:hi], preferred_element_type=F32)

    q_ref[...] = seg(0).astype(BF16)
    k_ref[...] = (seg(1) * K_SCALE).astype(BF16)
    v_ref[...] = seg(2).astype(BF16)
    o_ref[...] = seg(3)
    u_ref[...] = seg(4)
    ga_ref[...] = seg(5)
    gb_ref[...] = seg(6)
    g = jnp.dot(x, wg_ref[...], preferred_element_type=F32) + brow_ref[...]
    lane = lax.broadcasted_iota(I32, g.shape, 1)
    gcol_ref[...] = jnp.where(lane < N_HEADS, g, jax.nn.log_sigmoid(g))
    gt = lax.dot_general(wgt_ref[...], x, (((1,), (1,)), ((), ())), preferred_element_type=F32) + bcol_ref[...]
    sub = lax.broadcasted_iota(I32, gt.shape, 0)
    grow_ref[...] = jnp.where(sub < N_HEADS, gt, jax.nn.log_sigmoid(gt))


def _inproj(x, w, layer, n_prompt, dest=None):
    n = x.shape[0] if dest is None else dest.shape[0]
    tm = _row_tile(math.gcd(n_prompt, n - n_prompt))
    ntp = n_prompt // tm
    row = lambda w: pl.BlockSpec((tm, w), lambda i, *_: (i, 0))
    consts = (w["w_main"], w["w_gate"], w["w_gate_t"], w["w_key_t"], w["b_row"], w["b_col"])
    const_specs = [_layer_spec(c, layer) for c in consts]
    out_shape = [
        jax.ShapeDtypeStruct((n, D_MLSTM), BF16), jax.ShapeDtypeStruct((n, D_MLSTM), BF16),
        jax.ShapeDtypeStruct((n, D_MLSTM), BF16), jax.ShapeDtypeStruct((n, D_MLSTM), F32),
        jax.ShapeDtypeStruct((n, D_POOL), F32), jax.ShapeDtypeStruct((n, D_MODEL), F32),
        jax.ShapeDtypeStruct((n, D_MODEL), F32), jax.ShapeDtypeStruct((n, LANES), F32),
        jax.ShapeDtypeStruct((GATE_ROWS, n), F32), jax.ShapeDtypeStruct((D_MLSTM, n - n_prompt), BF16)]
    out_specs = [row(D_MLSTM), row(D_MLSTM), row(D_MLSTM), row(D_MLSTM), row(D_POOL), row(D_MODEL),
                 row(D_MODEL), row(LANES), pl.BlockSpec((GATE_ROWS, tm), lambda i, *_: (0, i)),
                 pl.BlockSpec((D_MLSTM, tm), lambda i, *_: (0, jnp.maximum(i - ntp, 0)))]
    if dest is None:
        return pl.pallas_call(
            functools.partial(_inproj_kernel, n_prompt_tiles=ntp),
            grid=(n // tm,),
            in_specs=[row(D_MODEL)] + const_specs,
            out_specs=out_specs,
            out_shape=out_shape,
            compiler_params=_params(("arbitrary",)),
            name="inproj",
        )(x, *consts)
    grid_spec = pltpu.PrefetchScalarGridSpec(
        num_scalar_prefetch=1,
        grid=(n // tm,),
        in_specs=[pl.BlockSpec(memory_space=pl.ANY)] + const_specs,
        out_specs=out_specs + [row(D_MODEL)],
        scratch_shapes=[pltpu.VMEM((tm, D_MODEL), F32), pltpu.VMEM((tm, D_MODEL), F32),
                        pltpu.SemaphoreType.DMA((2,))])
    return pl.pallas_call(
        functools.partial(_inproj_gather_kernel, n_prompt_tiles=ntp, n_tiles=n // tm, tm=tm),
        grid_spec=grid_spec,
        out_shape=out_shape + [jax.ShapeDtypeStruct((n, D_MODEL), F32)],
        compiler_params=_params(("arbitrary",)),
        name="inproj_gather",
    )(dest, x, *consts)


def _head_out(hval, o, gain):
    mu = jnp.mean(hval, axis=1, keepdims=True)
    xc = hval - mu
    var = jnp.mean(xc * xc, axis=1, keepdims=True)
    return (jax.nn.sigmoid(o) * (xc * lax.rsqrt(var + LN_EPS) * gain)).astype(BF16)


def _split3(x):
    x1 = x.astype(BF16)
    r1 = x - x1.astype(F32)
    x2 = r1.astype(BF16)
    x3 = (r1 - x2.astype(F32)).astype(BF16)
    return x1, x2, x3


def _mlstm_prompt_kernel(*refs, rows, strip, chained):
    if chained:
        q_ref, k_ref, v_ref, gcol_ref, grow_ref, o_ref, gain_ref, _, hg_ref, c_ref, n_ref, m_ref = refs
    else:
        q_ref, k_ref, v_ref, gcol_ref, grow_ref, o_ref, gain_ref, hg_ref, c_ref, n_ref, m_ref = refs

    @pl.when(pl.program_id(1) == 0)
    def _():
        c_ref[...] = jnp.zeros_like(c_ref)
        n_ref[...] = jnp.zeros_like(n_ref)
        m_ref[...] = jnp.zeros_like(m_ref)

    t_idx = lax.broadcasted_iota(I32, (rows, 1), 0)
    s_idx = lax.broadcasted_iota(I32, (1, rows), 1)
    mask = s_idx <= t_idx
    lower = jnp.where(mask, 1.0, 0.0).astype(BF16)
    upper = jnp.where(t_idx <= s_idx, 1.0, 0.0).astype(BF16)
    gcol = gcol_ref[...]
    grow = grow_ref[...]
    bcol = sum(jnp.dot(lower, p, preferred_element_type=F32) for p in _split3(gcol))
    brow = sum(jnp.dot(p, upper, preferred_element_type=F32) for p in _split3(grow))
    ones = jnp.ones((rows, LANES), BF16)
    neg_inf = jnp.float32(-jnp.inf)
    heads = range(N_HEADS)
    sls = [slice(h * HEAD_DIM, (h + 1) * HEAD_DIM) for h in heads]
    qs = [q_ref[:, sl] for sl in sls]
    ks = [k_ref[:, sl] for sl in sls]
    vos = [jnp.concatenate([v_ref[:, sl], ones], axis=1) for sl in sls]
    b_cs = [bcol[:, N_HEADS + h:N_HEADS + h + 1] for h in heads]
    b_rs = [brow[N_HEADS + h:N_HEADS + h + 1, :] for h in heads]
    g_rs = [grow[h:h + 1, :] - b_rs[h] for h in heads]
    m_prevs = [m_ref[0, h:h + 1, 0:1] for h in heads]
    c_prevs = [c_ref[0, h] for h in heads]
    n_prevs = [n_ref[0, h] for h in heads]
    qks = [lax.dot_general(qs[h], ks[h], (((1,), (1,)), ((), ())), preferred_element_type=F32) for h in heads]
    inters = [jnp.dot(qs[h], jnp.concatenate([c_prevs[h].astype(BF16), n_prevs[h].astype(BF16)], axis=1),
                      preferred_element_type=F32) for h in heads]
    gms = [jnp.where(mask, g_rs[h], neg_inf) for h in heads]
    tops = [jnp.maximum(m_prevs[h], jnp.max(gms[h], axis=1, keepdims=True)) for h in heads]
    ss = [(qks[h] * jnp.exp(gms[h] - tops[h])).astype(BF16) for h in heads]
    intras = [jnp.dot(ss[h], vos[h], preferred_element_type=F32) for h in heads]
    b_lasts = [b_rs[h][:, rows - 1:rows] for h in heads]
    d_lasts = [b_lasts[h] - b_cs[h] + gcol[:, h:h + 1] for h in heads]
    m_news = [jnp.maximum(b_lasts[h] + m_prevs[h], jnp.max(d_lasts[h], axis=0, keepdims=True)) for h in heads]
    kws = [(ks[h].astype(F32) * jnp.exp(d_lasts[h] - m_news[h])).astype(BF16) for h in heads]
    upds = [lax.dot_general(kws[h], vos[h], (((0,), (0,)), ((), ())), preferred_element_type=F32) for h in heads]
    s_inters = [jnp.exp(m_prevs[h] - tops[h]) for h in heads]
    nums = [s_inters[h] * inters[h][:, 0:HEAD_DIM] + intras[h][:, 0:HEAD_DIM] for h in heads]
    dens = [s_inters[h] * inters[h][:, HEAD_DIM:HEAD_DIM + 1] + intras[h][:, HEAD_DIM:HEAD_DIM + 1] for h in heads]
    hvals = [nums[h] / jnp.maximum(jnp.abs(dens[h]), jnp.exp(-(b_cs[h] + tops[h]))) for h in heads]
    mus = [jnp.mean(hvals[h], axis=1, keepdims=True) for h in heads]
    xcs = [hvals[h] - mus[h] for h in heads]
    vars_ = [jnp.mean(xcs[h] * xcs[h], axis=1, keepdims=True) for h in heads]
    for h in heads:
        hn = xcs[h] * lax.rsqrt(vars_[h] + LN_EPS) * gain_ref[:, sls[h]]
        hg_ref[:, sls[h]] = (jax.nn.sigmoid(o_ref[:, sls[h]]) * hn).astype(BF16)
    for h in heads:
        s_last = jnp.exp(b_lasts[h] + m_prevs[h] - m_news[h])
        c_ref[0, h] = s_last * c_prevs[h] + upds[h][:, 0:HEAD_DIM]
        n_ref[0, h] = s_last * n_prevs[h] + upds[h][:, HEAD_DIM:]
        m_ref[0, h:h + 1, :] = jnp.broadcast_to(m_news[h], (1, LANES))


def _mlstm_sample_kernel(*refs, seq_len, window, block, chained):
    if chained:
        (q_ref, k_ref, kt_ref, v_ref, gcol_ref, grow_ref, mtok_ref, o_ref, gain_ref, c0_ref, n0_ref, _,
         hg_ref, c1_ref, n1_ref, m1_ref) = refs
    else:
        (q_ref, k_ref, kt_ref, v_ref, gcol_ref, grow_ref, mtok_ref, o_ref, gain_ref, c0_ref, n0_ref,
         hg_ref, c1_ref, n1_ref, m1_ref) = refs
    n_seq = window // seq_len
    shift = int(math.log2(seq_len))
    w0 = (pl.program_id(0) % (block // window)) * window
    r_idx = lax.broadcasted_iota(I32, (window, 1), 0)
    t_idx = w0 + r_idx
    s_idx = lax.broadcasted_iota(I32, (1, block), 1)
    same = jnp.right_shift(t_idx, shift) == jnp.right_shift(s_idx, shift)
    mask = same & (s_idx <= t_idx)
    mask_t = same & (t_idx <= s_idx)
    r_seq = jnp.right_shift(r_idx, shift)
    l_seq = jnp.right_shift(s_idx - w0, shift)
    gcol = gcol_ref[...]
    grow = grow_ref[...]
    mtok = mtok_ref[...]
    neg_inf = jnp.float32(-jnp.inf)
    for h in range(N_HEADS):
        sl = slice(h * HEAD_DIM, (h + 1) * HEAD_DIM)
        q = q_ref[:, sl]
        k = k_ref[:, sl]
        kt = kt_ref[sl, :]
        v = v_ref[:, sl]
        li_r = grow[h:h + 1, :]
        lf_r = grow[N_HEADS + h:N_HEADS + h + 1, :]
        li_c = gcol[:, h:h + 1]
        lf_c = gcol[:, N_HEADS + h:N_HEADS + h + 1]
        m_c = mtok[:, h:h + 1]
        b_c = jnp.sum(jnp.where(mask, lf_r, 0.0), axis=1, keepdims=True)
        b_r = jnp.sum(jnp.where(mask_t, lf_c, 0.0), axis=0, keepdims=True)
        dmat = jnp.where(mask, b_c - b_r + li_r, neg_inf)
        inter = b_c + m_c
        m_t = jnp.maximum(inter, jnp.max(dmat, axis=1, keepdims=True))
        s_inter = jnp.exp(inter - m_t)
        s = jnp.dot(q, kt, preferred_element_type=F32) * jnp.exp(dmat - m_t)
        intra = jnp.dot(s.astype(BF16), v, preferred_element_type=F32)
        qf = q.astype(F32)
        qc = qn = None
        for j in range(n_seq):
            qc_j = jnp.dot(q, c0_ref[j, h].astype(BF16), preferred_element_type=F32)
            qn_j = jnp.sum(qf * n0_ref[j, h:h + 1, :], axis=1, keepdims=True)
            qc = qc_j if j == 0 else jnp.where(r_seq == j, qc_j, qc)
            qn = qn_j if j == 0 else jnp.where(r_seq == j, qn_j, qn)
        num = s_inter * qc + intra
        den = s_inter * qn + jnp.sum(s, axis=1, keepdims=True)
        hval = num / jnp.maximum(jnp.abs(den), jnp.exp(-m_t))
        hg_ref[:, sl] = _head_out(hval, o_ref[:, sl], gain_ref[:, sl])
        kf = k.astype(F32)
        ktf = kt.astype(F32)
        for j in range(n_seq):
            lsel = l_seq == j
            m_j = mtok[j * seq_len:j * seq_len + 1, h:h + 1]
            b_last = jnp.sum(jnp.where(lsel, lf_r, 0.0), axis=1, keepdims=True)
            d_r = jnp.where(lsel, b_last - b_r + li_r, neg_inf)
            d_c = jnp.where(r_seq == j, b_last - b_c + li_c, neg_inf)
            m_new = jnp.maximum(b_last + m_j, jnp.max(d_r, axis=1, keepdims=True))
            s_last = jnp.exp(b_last + m_j - m_new)
            kwt = (ktf * jnp.exp(d_r - m_new)).astype(BF16)
            c1_ref[j, h] = s_last * c0_ref[j, h] + jnp.dot(kwt, v, preferred_element_type=F32)
            n1_ref[j, h:h + 1, :] = (s_last * n0_ref[j, h:h + 1, :]
                                     + jnp.sum(kf * jnp.exp(d_c - m_new), axis=0, keepdims=True))
            m1_ref[j, h:h + 1, :] = jnp.broadcast_to(m_new, (1, LANES))


def _chain(c_all):
    if c_all is None:
        return [], []
    return [c_all], [pl.BlockSpec(memory_space=pl.ANY)]


def _mlstm_prompt(q, k, v, gcol, grow, o, gain, c_all, layer, n_seq, seq_len):
    n = n_seq * seq_len
    chunk = math.gcd(seq_len, MLSTM_PROMPT_CHUNK)
    nc = seq_len // chunk
    row = lambda w: pl.BlockSpec((chunk, w), lambda b, c: (b * nc + c, 0))
    st = lambda *tail: pl.BlockSpec((1,) + tail, lambda b, c: (b,) + (0,) * len(tail))
    extra, extra_specs = _chain(c_all)
    n_in = 7
    return pl.pallas_call(
        functools.partial(_mlstm_prompt_kernel, rows=chunk, strip=min(chunk, MLSTM_STRIP), chained=bool(extra)),
        grid=(n_seq, nc),
        in_specs=[row(D_MLSTM), row(D_MLSTM), row(D_MLSTM), row(LANES),
                  pl.BlockSpec((GATE_ROWS, chunk), lambda b, c: (0, b * nc + c)),
                  row(D_MLSTM), _layer_spec(gain, layer)] + extra_specs,
        out_specs=(row(D_MLSTM),
                   pl.BlockSpec((None, 1, N_HEADS, HEAD_DIM, HEAD_DIM), lambda b, c: (layer, b, 0, 0, 0)),
                   st(N_HEADS, HEAD_DIM, LANES), st(N_HEADS, LANES)),
        out_shape=(jax.ShapeDtypeStruct((n, D_MLSTM), BF16),
                   jax.ShapeDtypeStruct((DEPTH, n_seq, N_HEADS, HEAD_DIM, HEAD_DIM), F32),
                   jax.ShapeDtypeStruct((n_seq, N_HEADS, HEAD_DIM, LANES), F32),
                   jax.ShapeDtypeStruct((n_seq, N_HEADS, LANES), F32)),
        input_output_aliases={n_in: 1} if extra else {},
        compiler_params=_params(("parallel", "arbitrary")),
        name="mlstm_prompt",
    )(q, k, v, gcol, grow, o, gain, *extra)


def _mlstm_sample(q, k, kt, v, gcol, grow, mtok, o, gain, c0, n0, c_all, layer, first_row, n_seq, seq_len):
    n = n_seq * seq_len
    window, block = SAMPLE_WINDOW, SAMPLE_BLOCK_ROWS
    assert n % block == 0 and first_row % block == 0 and window % seq_len == 0
    per_win = window // seq_len
    sub = block // window
    row = lambda w: pl.BlockSpec((window, w), lambda i: (first_row // window + i, 0))
    extra, extra_specs = _chain(c_all)
    n_in = 11
    return pl.pallas_call(
        functools.partial(_mlstm_sample_kernel, seq_len=seq_len, window=window, block=block, chained=bool(extra)),
        grid=(n // window,),
        in_specs=[row(D_MLSTM), row(D_MLSTM),
                  pl.BlockSpec((D_MLSTM, block), lambda i: (0, i // sub)),
                  pl.BlockSpec((block, D_MLSTM), lambda i: (first_row // block + i // sub, 0)),
                  row(LANES),
                  pl.BlockSpec((GATE_ROWS, block), lambda i: (0, first_row // block + i // sub)),
                  pl.BlockSpec((window, LANES), lambda i: (i, 0)),
                  row(D_MLSTM), _layer_spec(gain, layer),
                  pl.BlockSpec((None, per_win, N_HEADS, HEAD_DIM, HEAD_DIM), lambda i: (layer, i, 0, 0, 0)),
                  pl.BlockSpec((None, per_win, N_HEADS, HEAD_DIM), lambda i: (layer, i, 0, 0))] + extra_specs,
        out_specs=(pl.BlockSpec((window, D_MLSTM), lambda i: (i, 0)),
                   pl.BlockSpec((None, per_win, N_HEADS, HEAD_DIM, HEAD_DIM), lambda i: (layer, i, 0, 0, 0)),
                   pl.BlockSpec((per_win, N_HEADS, HEAD_DIM), lambda i: (i, 0, 0)),
                   pl.BlockSpec((per_win, N_HEADS, LANES), lambda i: (i, 0, 0))),
        out_shape=(jax.ShapeDtypeStruct((n, D_MLSTM), BF16),
                   jax.ShapeDtypeStruct((DEPTH, n_seq, N_HEADS, HEAD_DIM, HEAD_DIM), F32),
                   jax.ShapeDtypeStruct((n_seq, N_HEADS, HEAD_DIM), F32),
                   jax.ShapeDtypeStruct((n_seq, N_HEADS, LANES), F32)),
        input_output_aliases={n_in: 1} if extra else {},
        compiler_params=_params(("parallel",)),
        name="mlstm_sample",
    )(q, k, kt, v, gcol, grow, mtok, o, gain, c0, n0, *extra)


def _pool_prompt_kernel(u_ref, prev_ref, out_ref, ext_ref, *, tm, tiles_per_seq):
    tile = pl.program_id(0) % tiles_per_seq
    head = 16
    ext_ref[0:head, :] = jnp.where(tile == 0, 0.0, prev_ref[...])
    ext_ref[head:, :] = u_ref[...]
    pos = tile * tm + lax.broadcasted_iota(I32, (tm, 1), 0)
    for g, w in enumerate(POOL_WINDOWS):
        sl = slice(g * POOL_GROUP_DIM, (g + 1) * POOL_GROUP_DIM)
        acc = ext_ref[head:head + tm, sl]
        for d in range(1, w):
            acc = acc + ext_ref[head - d:head - d + tm, sl]
        cnt = jnp.minimum(pos + 1, w).astype(F32)
        out_ref[:, sl] = (acc / cnt - u_ref[:, sl]).astype(BF16)


def _pool_prompt(u, n_seq, seq_len):
    n = n_seq * seq_len
    tm = _row_tile(seq_len)
    head = 16
    return pl.pallas_call(
        functools.partial(_pool_prompt_kernel, tm=tm, tiles_per_seq=seq_len // tm),
        grid=(n // tm,),
        in_specs=[pl.BlockSpec((tm, D_POOL), lambda i: (i, 0)),
                  pl.BlockSpec((head, D_POOL), lambda i: (jnp.maximum(i * (tm // head) - 1, 0), 0))],
        out_specs=pl.BlockSpec((tm, D_POOL), lambda i: (i, 0)),
        out_shape=jax.ShapeDtypeStruct((n, D_POOL), BF16),
        scratch_shapes=[pltpu.VMEM((head + tm, D_POOL), F32)],
        compiler_params=_params(("parallel",)),
        name="pool_prompt",
    )(u, u)


def _pool_sample_kernel(u_ref, buf_ref, out_ref, nbuf_ref, ext_ref, *, seq_len, start):
    ext_ref[:, 0:POOL_BUF, :] = buf_ref[...]
    ext_ref[:, POOL_BUF:POOL_BUF + seq_len, :] = u_ref[...]
    pos = start + lax.broadcasted_iota(I32, (1, seq_len, 1), 1)
    for g, w in enumerate(POOL_WINDOWS):
        sl = slice(g * POOL_GROUP_DIM, (g + 1) * POOL_GROUP_DIM)
        acc = ext_ref[:, POOL_BUF:POOL_BUF + seq_len, sl]
        for d in range(1, w):
            acc = acc + ext_ref[:, POOL_BUF - d:POOL_BUF - d + seq_len, sl]
        cnt = jnp.minimum(pos + 1, w).astype(F32)
        out_ref[:, :, sl] = (acc / cnt - u_ref[:, :, sl]).astype(BF16)
    nbuf_ref[...] = ext_ref[:, seq_len:seq_len + POOL_BUF, :]


def _pool_sample(u3, buf, layer, start):
    n_seq, seq_len, _ = u3.shape
    bs = _row_tile(n_seq, 32)
    spec = lambda r: pl.BlockSpec((bs, r, D_POOL), lambda i: (i, 0, 0))
    return pl.pallas_call(
        functools.partial(_pool_sample_kernel, seq_len=seq_len, start=start),
        grid=(n_seq // bs,),
        in_specs=[spec(seq_len), pl.BlockSpec((None, bs, POOL_BUF, D_POOL), lambda i: (layer, i, 0, 0))],
        out_specs=(spec(seq_len), spec(POOL_BUF)),
        out_shape=(jax.ShapeDtypeStruct((n_seq, seq_len, D_POOL), BF16),
                   jax.ShapeDtypeStruct((n_seq, POOL_BUF, D_POOL), F32)),
        scratch_shapes=[pltpu.VMEM((bs, POOL_BUF + seq_len + 5, D_POOL), F32)],
        compiler_params=_params(("parallel",)),
        name="pool_sample",
    )(u3, buf)


def _layer_norm(y, g, b):
    mu = jnp.mean(y, axis=1, keepdims=True)
    yc = y - mu
    var = jnp.mean(yc * yc, axis=1, keepdims=True)
    return yc * lax.rsqrt(var + LN_EPS) * g + b


def _route(logits_t):
    tokens = logits_t.shape[1]
    grp = lax.broadcasted_iota(I32, (8, tokens), 0)
    live = grp < N_EXPERT_GROUPS
    neg_inf = jnp.float32(-jnp.inf)
    lm = [jnp.where(live, logits_t[8 * m:8 * m + 8, :], neg_inf) for m in range(EXPERTS_PER_GROUP)]
    mx = jnp.max(jnp.maximum(jnp.maximum(lm[0], lm[1]), jnp.maximum(lm[2], lm[3])), axis=0, keepdims=True)
    ex = [jnp.exp(l - mx) for l in lm]
    tot = jnp.sum(ex[0] + ex[1] + ex[2] + ex[3], axis=0, keepdims=True)
    p = [e / tot for e in ex]
    top1 = jnp.maximum(jnp.maximum(p[0], p[1]), jnp.maximum(p[2], p[3]))
    i1 = jnp.where(p[0] == top1, 0, jnp.where(p[1] == top1, 1, jnp.where(p[2] == top1, 2, 3)))
    r = [jnp.where(i1 == m, -1.0, p[m]) for m in range(EXPERTS_PER_GROUP)]
    top2 = jnp.maximum(jnp.maximum(r[0], r[1]), jnp.maximum(r[2], r[3]))
    i2 = jnp.where(r[0] == top2, 0, jnp.where(r[1] == top2, 1, jnp.where(r[2] == top2, 2, 3)))
    gscore = jnp.where(live, top1 + top2, neg_inf)
    gmax = jnp.max(gscore, axis=0, keepdims=True)
    gsel = jnp.min(jnp.where(gscore == gmax, grp, 8), axis=0, keepdims=True)
    chosen = grp == gsel
    tsum = top1 + top2
    w1 = top1 / tsum
    w2 = top2 / tsum
    first_is_lo = i1 < i2
    lo = jnp.minimum(i1, i2)
    hi = jnp.maximum(i1, i2)
    pair = jnp.where(lo == 0, hi - 1, jnp.where(lo == 1, hi + 1, 5))
    pick = lambda a: jnp.sum(jnp.where(chosen, a, jnp.zeros_like(a)), axis=0, keepdims=True)
    cls = pick(grp * len(PAIRS) + pair)
    w_lo = pick(jnp.where(first_is_lo, w1, w2))
    w_hi = pick(jnp.where(first_is_lo, w2, w1))
    return cls, w_lo, w_hi


def _mix_kernel(hgp_ref, hgs_ref, plp_ref, pls_ref, ga_ref, gb_ref, x_ref, wpool_ref, pscale_ref, wa_ref, wb_ref,
                wout_ref, g1_ref, b1_ref, wr_ref, br_ref,
                dest_ref, cnt_ref, xs_ref,
                rows0_ref, rows1_ref, dvm_ref, dsm0_ref, dsm1_ref, carry_ref, row_sem, idx_sem,
                *, tm, n_prompt_tiles, n_tiles, capacity):
    i = pl.program_id(0)
    rows = (rows0_ref, rows1_ref)
    dsm = (dsm0_ref, dsm1_ref)
    spare = N_CLASSES * capacity
    sent = lambda s: pltpu.make_async_copy(rows[s], xs_ref.at[pl.ds(0, tm), :], row_sem.at[s])

    def send(s, unrolled):
        def one(r, carry=0):
            pltpu.make_async_copy(rows[s].at[pl.ds(r, 1), :], xs_ref.at[pl.ds(dsm[s][r], 1), :],
                                  row_sem.at[s]).start()
            return carry
        if unrolled:
            for r in range(tm):
                one(r)
        else:
            lax.fori_loop(0, tm, one, 0, unroll=8)

    @pl.when(i == 0)
    def _():
        carry_ref[...] = jnp.zeros_like(carry_ref)
        rows0_ref[...] = jnp.zeros_like(rows0_ref)
        rows1_ref[...] = jnp.zeros_like(rows1_ref)

        def spare_rows(r, carry):
            dsm0_ref[r] = spare + r
            return carry

        lax.fori_loop(0, tm, spare_rows, 0)
        send(0, False)
        dvm_ref[...] = spare + tm + lax.broadcasted_iota(I32, (8, tm), 1)
        pltpu.make_async_copy(dvm_ref.at[0], dsm1_ref, idx_sem).start()

    for parity in (0, 1):
        pl.when(i % 2 == parity)(functools.partial(
            _mix_step, parity, i, hgp_ref, hgs_ref, plp_ref, pls_ref, ga_ref, gb_ref, x_ref, wpool_ref, pscale_ref,
            wa_ref, wb_ref, wout_ref, g1_ref, b1_ref, wr_ref, br_ref, dest_ref, cnt_ref, rows, dvm_ref, dsm,
            carry_ref, idx_sem, send, sent, tm, n_prompt_tiles, n_tiles, capacity))


def _mix_step(cur, i, hgp_ref, hgs_ref, plp_ref, pls_ref, ga_ref, gb_ref, x_ref, wpool_ref, pscale_ref, wa_ref,
              wb_ref, wout_ref, g1_ref, b1_ref, wr_ref, br_ref, dest_ref, cnt_ref, rows, dvm_ref, dsm, carry_ref,
              idx_sem, send, sent, tm, n_prompt_tiles, n_tiles, capacity):
    slots_arrived = lambda s: pltpu.make_async_copy(dvm_ref.at[0], dsm[s], idx_sem)
    slots_arrived(1 - cur).wait()
    send(1 - cur, True)
    is_prompt = i < n_prompt_tiles
    hg = jnp.where(is_prompt, hgp_ref[...], hgs_ref[...])
    pooled = jnp.where(is_prompt, plp_ref[...], pls_ref[...])
    ya = jnp.dot(hg, wa_ref[...], preferred_element_type=F32)
    parts = []
    for g in range(len(POOL_WINDOWS)):
        sl = slice(g * POOL_GROUP_DIM, (g + 1) * POOL_GROUP_DIM)
        parts.append(jnp.dot(pooled[:, sl], wpool_ref[g], preferred_element_type=F32))
    pl_lin = jnp.concatenate(parts, axis=1) * pscale_ref[...]
    yb = jnp.dot(pl_lin.astype(BF16), wb_ref[...], preferred_element_type=F32)
    mix = jax.nn.sigmoid(ga_ref[...]) * ya + jax.nn.sigmoid(gb_ref[...]) * yb
    res = jnp.dot(mix.astype(BF16), wout_ref[...], preferred_element_type=F32)
    x1 = _layer_norm(ALPHA * x_ref[...] + res, g1_ref[...], b1_ref[...])

    logits_t = lax.dot_general(wr_ref[...], x1, (((1,), (1,)), ((), ())), preferred_element_type=F32,
                               precision=lax.Precision.HIGHEST) + br_ref[...]
    cls, w_lo, w_hi = _route(logits_t)

    onehot = lax.broadcasted_iota(I32, (CLASS_ROWS, tm), 0) == cls
    earlier = lax.broadcasted_iota(I32, (tm, tm), 0) < lax.broadcasted_iota(I32, (tm, tm), 1)
    before = jnp.dot(jnp.where(onehot, 1.0, 0.0).astype(BF16), jnp.where(earlier, 1.0, 0.0).astype(BF16),
                     preferred_element_type=F32)
    seen = carry_ref[:, 0:1]
    rank = jnp.sum(jnp.where(onehot, before + seen, 0.0), axis=0, keepdims=True)
    carry_ref[...] = carry_ref[...] + jnp.sum(jnp.where(onehot, 1.0, 0.0), axis=1, keepdims=True)
    cnt_ref[...] = carry_ref[...]
    dest = cls * capacity + rank.astype(I32)
    dest_ref[...] = jnp.broadcast_to(dest, (8, tm))

    sent(cur).wait()
    wrows = jnp.concatenate([w_lo, w_hi, jnp.zeros((LANES - 2, tm), F32)], axis=0)
    rows[cur][:, 0:D_MODEL] = x1
    rows[cur][:, D_MODEL:ROW_EXT] = wrows.T
    dvm_ref[...] = jnp.broadcast_to(dest, (8, tm))
    slots_arrived(cur).start()

    @pl.when(i == n_tiles - 1)
    def _():
        slots_arrived(cur).wait()
        send(cur, False)
        sent(cur).wait()
        sent(1 - cur).wait()


def _mix(hg_p, hg_s, pooled_p, pooled_s, ga, gb, x, w, layer, capacity):
    n = x.shape[0]
    n_p = hg_p.shape[0]
    tm = _row_tile(math.gcd(n_p, n - n_p))
    ntp = n_p // tm
    row = lambda w: pl.BlockSpec((tm, w), lambda i: (i, 0))
    prow = lambda w: pl.BlockSpec((tm, w), lambda i: (jnp.minimum(i, ntp - 1), 0))
    srow = lambda w: pl.BlockSpec((tm, w), lambda i: (jnp.maximum(i - ntp, 0), 0))
    per_layer = (w["w_pool"], w["pool_scale"], w["w_proj_a"], w["w_proj_b"], w["w_out"], w["ln1_g"], w["ln1_b"])
    shared = (w["w_router_t"], w["b_router_col"])
    consts = per_layer + shared
    return pl.pallas_call(
        functools.partial(_mix_kernel, tm=tm, n_prompt_tiles=ntp, n_tiles=n // tm, capacity=capacity),
        grid=(n // tm,),
        in_specs=[prow(D_MLSTM), srow(D_MLSTM), prow(D_POOL), srow(D_POOL), row(D_MODEL), row(D_MODEL), row(D_MODEL)]
                 + [_layer_spec(c, layer) for c in per_layer] + [_const_spec(c.shape) for c in shared],
        out_specs=(pl.BlockSpec((8, tm), lambda i: (0, i)),
                   pl.BlockSpec((CLASS_ROWS, LANES), lambda i: (0, 0)),
                   pl.BlockSpec(memory_space=pl.ANY)),
        out_shape=(jax.ShapeDtypeStruct((8, n), I32),
                   jax.ShapeDtypeStruct((CLASS_ROWS, LANES), F32),
                   jax.ShapeDtypeStruct((N_CLASSES * capacity + 2 * tm, ROW_EXT), F32)),
        scratch_shapes=[pltpu.VMEM((tm, ROW_EXT), F32), pltpu.VMEM((tm, ROW_EXT), F32), pltpu.VMEM((8, tm), I32),
                        pltpu.SMEM((tm,), I32), pltpu.SMEM((tm,), I32), pltpu.VMEM((CLASS_ROWS, LANES), F32),
                        pltpu.SemaphoreType.DMA((2,)), pltpu.SemaphoreType.DMA],
        compiler_params=_params(("arbitrary",)),
        name="mix",
    )(hg_p, hg_s, pooled_p, pooled_s, ga, gb, x, *consts)


def _moe_kernel(blk_ref, elo_ref, ehi_ref, nvalid_ref, ntiles_ref,
                xs_ref, wg_lo, wu_lo, wd_lo, wg_hi, wu_hi, wd_hi, g2_ref, b2_ref, ys_ref):
    i = pl.program_id(0)

    @pl.when(i < ntiles_ref[0])
    def _():
        valid = lax.broadcasted_iota(I32, (MOE_TILE, 1), 0) < nvalid_ref[i]
        xe = xs_ref[...]
        x = jnp.where(valid, xe[:, 0:D_MODEL], 0.0)
        w_lo = jnp.where(valid, xe[:, D_MODEL:D_MODEL + 1], 0.0)
        w_hi = jnp.where(valid, xe[:, D_MODEL + 1:D_MODEL + 2], 0.0)
        xb = x.astype(BF16)

        def expert(wg, wu, wd, w):
            g = jnp.dot(xb, wg[0], preferred_element_type=F32)
            u = jnp.dot(xb, wu[0], preferred_element_type=F32)
            hid = (g * jax.nn.sigmoid(g)) * u * w
            return jnp.dot(hid.astype(BF16), wd[0], preferred_element_type=F32)

        y = expert(wg_lo, wu_lo, wd_lo, w_lo) + expert(wg_hi, wu_hi, wd_hi, w_hi)
        ys_ref[...] = _layer_norm(ALPHA * x + y, g2_ref[...], b2_ref[...])


def _moe(xs, tables, w, layer, max_tiles):
    blk, e_lo, e_hi, n_valid, n_tiles = tables
    up = lambda sel: pl.BlockSpec((None, 1, D_MODEL, D_EXPERT),
                                  lambda i, b, lo, hi, nv, nt: (layer, (lo, hi)[sel][i], 0, 0))
    down = lambda sel: pl.BlockSpec((None, 1, D_EXPERT, D_MODEL),
                                    lambda i, b, lo, hi, nv, nt: (layer, (lo, hi)[sel][i], 0, 0))
    grid_spec = pltpu.PrefetchScalarGridSpec(
        num_scalar_prefetch=5,
        grid=(max_tiles,),
        in_specs=[pl.BlockSpec((MOE_TILE, ROW_EXT), lambda i, b, *_: (b[i], 0)),
                  up(0), up(0), down(0), up(1), up(1), down(1),
                  _layer_spec(w["ln2_g"], layer), _layer_spec(w["ln2_b"], layer)],
        out_specs=pl.BlockSpec((MOE_TILE, D_MODEL), lambda i, b, *_: (b[i], 0)))
    return pl.pallas_call(
        _moe_kernel,
        grid_spec=grid_spec,
        out_shape=jax.ShapeDtypeStruct((xs.shape[0], D_MODEL), F32),
        compiler_params=_params(("arbitrary",)),
        name="moe",
    )(blk, e_lo, e_hi, n_valid, n_tiles, xs, w["w_e_gate"], w["w_e_up"], w["w_e_down"],
      w["w_e_gate"], w["w_e_up"], w["w_e_down"], w["ln2_g"], w["ln2_b"])


def _tile_tables(counts, capacity, max_tiles):
    cnt = counts[:N_CLASSES, 0].astype(I32)
    tiles = (cnt + MOE_TILE - 1) // MOE_TILE
    ends = jnp.cumsum(tiles)
    starts = ends - tiles
    n_tiles = ends[-1]
    t = jnp.minimum(jnp.arange(max_tiles, dtype=I32), jnp.maximum(n_tiles - 1, 0))
    cls = jnp.sum((ends[None, :] <= t[:, None]).astype(I32), axis=1)
    onehot = (jnp.arange(N_CLASSES, dtype=I32)[None, :] == cls[:, None]).astype(I32)
    within = t - jnp.sum(onehot * starts[None, :], axis=1)
    n_valid = jnp.clip(jnp.sum(onehot * cnt[None, :], axis=1) - within * MOE_TILE, 0, MOE_TILE)
    blk = cls * (capacity // MOE_TILE) + within
    grp = cls // len(PAIRS)
    pair = cls % len(PAIRS)
    lo = jnp.where(pair < 3, 0, jnp.where(pair < 5, 1, 2))
    hi = jnp.where(pair < 3, pair + 1, jnp.where(pair < 5, pair - 1, 3))
    return (blk, grp * EXPERTS_PER_GROUP + lo, grp * EXPERTS_PER_GROUP + hi, n_valid,
            n_tiles.reshape(1).astype(I32))


def _unpermute_kernel(dest_ref, ys_ref, prompt_ref, sample_ref, sem, *, tm, n_prompt_tiles):
    i = pl.program_id(0)

    def gather(out_ref):
        def fetch(r, carry):
            pltpu.make_async_copy(ys_ref.at[pl.ds(dest_ref[i * tm + r], 1), :], out_ref.at[pl.ds(r, 1), :],
                                  sem).start()
            return carry

        lax.fori_loop(0, tm, fetch, 0, unroll=8)
        pltpu.make_async_copy(ys_ref.at[pl.ds(0, tm), :], out_ref, sem).wait()

    pl.when(i < n_prompt_tiles)(functools.partial(gather, prompt_ref))
    pl.when(i >= n_prompt_tiles)(functools.partial(gather, sample_ref))


def _unpermute(ys, dest, n_prompt):
    n = dest.shape[0]
    tm = _row_tile(math.gcd(n_prompt, n - n_prompt))
    ntp = n_prompt // tm
    grid_spec = pltpu.PrefetchScalarGridSpec(
        num_scalar_prefetch=1,
        grid=(n // tm,),
        in_specs=[pl.BlockSpec(memory_space=pl.ANY)],
        out_specs=(pl.BlockSpec((tm, D_MODEL), lambda i, d: (jnp.minimum(i, ntp - 1), 0)),
                   pl.BlockSpec((tm, D_MODEL), lambda i, d: (jnp.maximum(i - ntp, 0), 0))),
        scratch_shapes=[pltpu.SemaphoreType.DMA])
    return pl.pallas_call(
        functools.partial(_unpermute_kernel, tm=tm, n_prompt_tiles=ntp),
        grid_spec=grid_spec,
        out_shape=(jax.ShapeDtypeStruct((n_prompt, D_MODEL), F32),
                   jax.ShapeDtypeStruct((n - n_prompt, D_MODEL), F32)),
        compiler_params=_params(("arbitrary",)),
        name="unpermute",
    )(dest, ys)


def _prepare_weights(w_in, b_gate, hn_gain, w_pool, pool_scale, w_proj_a, w_proj_b, w_out, ln1_g, ln1_b,
                     ln2_g, ln2_b, w_router, b_router, w_e_gate, w_e_up, w_e_down):
    w_main = jnp.concatenate([w_in[:, :, :_G0], w_in[:, :, _U0:]], axis=2).astype(BF16)
    w_gate = w_in[:, :, _G0:_U0]
    w_gate_p = jnp.pad(w_gate, ((0, 0), (0, 0), (0, LANES - 2 * N_HEADS))).astype(BF16)
    w_gate_t = jnp.pad(jnp.swapaxes(w_gate, 1, 2), ((0, 0), (0, GATE_ROWS - 2 * N_HEADS), (0, 0))).astype(BF16)
    w_key_t = jnp.swapaxes(w_in[:, :, D_MLSTM:2 * D_MLSTM], 1, 2).astype(BF16)
    b_row = jnp.pad(b_gate, ((0, 0), (0, LANES - 2 * N_HEADS)))[:, None, :]
    b_col = jnp.pad(b_gate, ((0, 0), (0, GATE_ROWS - 2 * N_HEADS)))[:, :, None]
    wr = w_router.T.reshape(N_EXPERT_GROUPS, EXPERTS_PER_GROUP, D_MODEL).swapaxes(0, 1)
    wr = jnp.pad(wr, ((0, 0), (0, 8 - N_EXPERT_GROUPS), (0, 0))).reshape(ROUTER_ROWS, D_MODEL)
    br = b_router.reshape(N_EXPERT_GROUPS, EXPERTS_PER_GROUP).T
    br = jnp.pad(br, ((0, 0), (0, 8 - N_EXPERT_GROUPS))).reshape(ROUTER_ROWS, 1)
    per_row = lambda a: a.reshape(DEPTH, 1, -1)
    return dict(
        w_main=w_main, w_gate=w_gate_p, w_gate_t=w_gate_t, w_key_t=w_key_t, b_row=b_row, b_col=b_col,
        gain=per_row(hn_gain), w_pool=w_pool.astype(BF16), pool_scale=per_row(pool_scale),
        w_proj_a=w_proj_a.astype(BF16), w_proj_b=w_proj_b.astype(BF16), w_out=w_out.astype(BF16),
        ln1_g=per_row(ln1_g), ln1_b=per_row(ln1_b), ln2_g=per_row(ln2_g), ln2_b=per_row(ln2_b),
        w_router_t=wr, b_router_col=br,
        w_e_gate=w_e_gate.astype(BF16), w_e_up=w_e_up.astype(BF16), w_e_down=w_e_down.astype(BF16))


def kernel(x_prompt, x_sample, state_C, state_n, state_m, state_pool, w_in, b_gate, hn_gain, w_pool, pool_scale,
           w_proj_a, w_proj_b, w_out, ln1_g, ln1_b, ln2_g, ln2_b, w_router, b_router, w_e_gate, w_e_up, w_e_down):
    w = _prepare_weights(w_in, b_gate, hn_gain, w_pool, pool_scale, w_proj_a, w_proj_b, w_out, ln1_g, ln1_b,
                         ln2_g, ln2_b, w_router, b_router, w_e_gate, w_e_up, w_e_down)
    n_pseq, p_len, _ = x_prompt.shape
    n_sseq, s_len, _ = x_sample.shape
    n_p = n_pseq * p_len
    n_s = n_sseq * s_len
    n = n_p + n_s
    capacity = -(-n // MOE_TILE) * MOE_TILE
    max_tiles = n // MOE_TILE + N_CLASSES
    x = jnp.concatenate([x_prompt.reshape(n_p, D_MODEL), x_sample.reshape(n_s, D_MODEL)], axis=0)
    np_, mp, bp, ns, ms, bs = [], [], [], [], [], []
    c_p = c_s = None
    ys = dest = None
    for l in range(DEPTH):
        if l == 0:
            q, k, v, o, u, ga, gb, gcol, grow, kt = _inproj(x, w, l, n_p)
        else:
            q, k, v, o, u, ga, gb, gcol, grow, kt, x = _inproj(ys, w, l, n_p, dest[0])
        hg_p, c_p, n1p, m1p = _mlstm_prompt(q, k, v, gcol, grow, o, w["gain"], c_p, l, n_pseq, p_len)
        mtok = jnp.pad(jnp.repeat(state_m[l], s_len, axis=0), ((0, 0), (0, LANES - N_HEADS)))
        hg_s, c_s, n1s, m1s = _mlstm_sample(q, k, kt, v, gcol, grow, mtok, o, w["gain"], state_C, state_n, c_s,
                                            l, n_p, n_sseq, s_len)
        pooled_p = _pool_prompt(u, n_pseq, p_len)
        pooled_s, nbuf = _pool_sample(u[n_p:].reshape(n_sseq, s_len, D_POOL), state_pool, l, PAST_LEN)
        dest, counts, xs = _mix(hg_p, hg_s, pooled_p, pooled_s.reshape(n_s, D_POOL), ga, gb, x, w, l, capacity)
        ys = _moe(xs, _tile_tables(counts, capacity, max_tiles), w, l, max_tiles)
        np_.append(n1p[:, :, :, 0])
        mp.append(m1p[:, :, 0])
        bp.append(u[:n_p].reshape(n_pseq, p_len, D_POOL)[:, p_len - POOL_BUF:])
        ns.append(n1s)
        ms.append(m1s[:, :, 0])
        bs.append(nbuf)
    y_p, y_s = _unpermute(ys, dest[0], n_p)
    st = jnp.stack
    return (y_p.reshape(n_pseq, p_len, D_MODEL), y_s.reshape(n_sseq, s_len, D_MODEL),
            c_p, st(np_), st(mp), st(bp), c_s, st(ns), st(ms), st(bs))
```

```python
import functools
import math

import jax
import jax.numpy as jnp
from jax import lax
from jax.experimental import pallas as pl
from jax.experimental.pallas import tpu as pltpu

F32 = jnp.float32
BF16 = jnp.bfloat16
I32 = jnp.int32

D_MODEL = 1024
N_HEADS = 4
HEAD_DIM = 256
D_MLSTM = N_HEADS * HEAD_DIM
POOL_WINDOWS = (2, 4, 8, 16)
POOL_GROUP_DIM = 128
D_POOL = len(POOL_WINDOWS) * POOL_GROUP_DIM
POOL_BUF = 15
N_EXPERTS = 16
N_EXPERT_GROUPS = 4
EXPERTS_PER_GROUP = 4
D_EXPERT = 512
DEPTH = 4
PAST_LEN = 16384
ALPHA = (2 * DEPTH) ** 0.25
LN_EPS = 1e-5
K_SCALE = HEAD_DIM ** -0.5

LANES = 128
GATE_ROWS = 16
ROUTER_ROWS = 32
VMEM_LIMIT = 52 * 1024 * 1024
MLSTM_PROMPT_CHUNK = 256
MLSTM_STRIP = 256
SAMPLE_BLOCK_ROWS = 128
SAMPLE_WINDOW = 16

PAIRS = ((0, 1), (0, 2), (0, 3), (1, 2), (1, 3), (2, 3))
N_CLASSES = N_EXPERT_GROUPS * len(PAIRS)
CLASS_ROWS = 32
MOE_TILE = 256
ROW_EXT = D_MODEL + LANES

_G0 = 4 * D_MLSTM
_U0 = _G0 + 2 * N_HEADS
_SEGS = ((0, 0, 1024), (0, 1024, 2048), (0, 2048, 3072), (0, 3072, 4096), (1, 0, 512), (1, 512, 1536), (1, 1536, 2560))


def _params(sem, **kw):
    return pltpu.CompilerParams(dimension_semantics=sem, vmem_limit_bytes=VMEM_LIMIT, **kw)


def _const_spec(shape):
    nd = len(shape)
    return pl.BlockSpec(shape, lambda *_: (0,) * nd, pipeline_mode=pl.Buffered(1))


def _layer_spec(stacked, layer):
    tail = stacked.shape[1:]
    return pl.BlockSpec((None,) + tail, lambda *_: (layer,) + (0,) * len(tail), pipeline_mode=pl.Buffered(1))


def _row_tile(n, cap=512):
    t = cap
    while n % t:
        t //= 2
    return t


def _inproj_kernel(xp_ref, xs_ref, wa_ref, wb_ref, wg_ref, wgt_ref, wkt_ref, brow_ref, bcol_ref,
                   q_ref, k_ref, v_ref, o_ref, u_ref, ga_ref, gb_ref, gcol_ref, grow_ref, kt_ref, x_ref,
                   *, n_prompt_tiles):
    x32 = jnp.where(pl.program_id(0) < n_prompt_tiles, xp_ref[...], xs_ref[...])
    x_ref[...] = x32
    _inproj_body(x32.astype(BF16), wa_ref, wb_ref, wg_ref, wgt_ref, wkt_ref, brow_ref, bcol_ref,
                 q_ref, k_ref, v_ref, o_ref, u_ref, ga_ref, gb_ref, gcol_ref, grow_ref, kt_ref, n_prompt_tiles)


def _inproj_gather_kernel(dest_ref, ys_ref, wa_ref, wb_ref, wg_ref, wgt_ref, wkt_ref, brow_ref, bcol_ref,
                          q_ref, k_ref, v_ref, o_ref, u_ref, ga_ref, gb_ref, gcol_ref, grow_ref, kt_ref, x_ref,
                          xbuf0_ref, xbuf1_ref, sem, *, n_prompt_tiles, n_tiles, tm):
    i = pl.program_id(0)
    bufs = (xbuf0_ref, xbuf1_ref)

    def request(tile, slot, lo, hi, unrolled):
        def one(r, carry=0):
            pltpu.make_async_copy(ys_ref.at[pl.ds(dest_ref[tile * tm + r], 1), :],
                                  bufs[slot].at[pl.ds(r, 1), :], sem.at[slot]).start()
            return carry
        if unrolled:
            for r in range(lo, hi):
                one(r)
        else:
            lax.fori_loop(lo, hi, one, 0, unroll=8)

    arrived = lambda slot: pltpu.make_async_copy(ys_ref.at[pl.ds(0, tm), :], bufs[slot], sem.at[slot])

    @pl.when(i == 0)
    def _():
        request(0, 0, 0, tm, False)

    def step(cur):
        arrived(cur).wait()
        request(jnp.minimum(i + 1, n_tiles - 1), 1 - cur, 0, tm, True)
        x32 = bufs[cur][...]
        x_ref[...] = x32
        _inproj_body(x32.astype(BF16), wa_ref, wb_ref, wg_ref, wgt_ref, wkt_ref, brow_ref, bcol_ref, q_ref, k_ref, v_ref,
                     o_ref, u_ref, ga_ref, gb_ref, gcol_ref, grow_ref, kt_ref, n_prompt_tiles)

        @pl.when(i == n_tiles - 1)
        def _():
            arrived(1 - cur).wait()

    for parity in (0, 1):
        pl.when(i % 2 == parity)(functools.partial(step, parity))


def _inproj_body(x, wa_ref, wb_ref, wg_ref, wgt_ref, wkt_ref, brow_ref, bcol_ref,
                 q_ref, k_ref, v_ref, o_ref, u_ref, ga_ref, gb_ref, gcol_ref, grow_ref, kt_ref, n_prompt_tiles):
    @pl.when(pl.program_id(0) >= n_prompt_tiles)
    def _():
        kt = lax.dot_general(wkt_ref[...], x, (((1,), (1,)), ((), ())), preferred_element_type=F32)
        kt_ref[...] = (kt * K_SCALE).astype(BF16)

    def seg(i):
        ref, lo, hi = _SEGS[i]
        return jnp.dot(x, (wa_ref, wb_ref)[ref][:, lo:hi], preferred_element_type=F32)

    q_ref[...] = seg(0).astype(BF16)
    k_ref[...] = (seg(1) * K_SCALE).astype(BF16)
    v_ref[...] = seg(2).astype(BF16)
    o_ref[...] = seg(3)
    u_ref[...] = seg(4)
    ga_ref[...] = seg(5)
    gb_ref[...] = seg(6)
    g = jnp.dot(x, wg_ref[...], preferred_element_type=F32) + brow_ref[...]
    lane = lax.broadcasted_iota(I32, g.shape, 1)
    gcol_ref[...] = jnp.where(lane < N_HEADS, g, jax.nn.log_sigmoid(g))
    gt = lax.dot_general(wgt_ref[...], x, (((1,), (1,)), ((), ())), preferred_element_type=F32) + bcol_ref[...]
    sub = lax.broadcasted_iota(I32, gt.shape, 0)
    grow_ref[...] = jnp.where(sub < N_HEADS, gt, jax.nn.log_sigmoid(gt))


def _inproj(x, w, layer, n_prompt, dest=None):
    n = sum(a.shape[0] for a in x) if dest is None else dest.shape[0]
    tm = _row_tile(math.gcd(n_prompt, n - n_prompt))
    ntp = n_prompt // tm
    row = lambda w: pl.BlockSpec((tm, w), lambda i, *_: (i, 0))
    consts = (w["w_qkvo"], w["w_rest"], w["w_gate"], w["w_gate_t"], w["w_key_t"], w["b_row"], w["b_col"])
    const_specs = [_layer_spec(c, layer) for c in consts]
    out_shape = [
        jax.ShapeDtypeStruct((n, D_MLSTM), BF16), jax.ShapeDtypeStruct((n, D_MLSTM), BF16),
        jax.ShapeDtypeStruct((n, D_MLSTM), BF16), jax.ShapeDtypeStruct((n, D_MLSTM), F32),
        jax.ShapeDtypeStruct((n, D_POOL), F32), jax.ShapeDtypeStruct((n, D_MODEL), F32),
        jax.ShapeDtypeStruct((n, D_MODEL), F32), jax.ShapeDtypeStruct((n, LANES), F32),
        jax.ShapeDtypeStruct((GATE_ROWS, n), F32), jax.ShapeDtypeStruct((D_MLSTM, n - n_prompt), BF16)]
    out_specs = [row(D_MLSTM), row(D_MLSTM), row(D_MLSTM), row(D_MLSTM), row(D_POOL), row(D_MODEL),
                 row(D_MODEL), row(LANES), pl.BlockSpec((GATE_ROWS, tm), lambda i, *_: (0, i)),
                 pl.BlockSpec((D_MLSTM, tm), lambda i, *_: (0, jnp.maximum(i - ntp, 0))), row(D_MODEL)]
    out_shape.append(jax.ShapeDtypeStruct((n, D_MODEL), F32))
    if dest is None:
        return pl.pallas_call(
            functools.partial(_inproj_kernel, n_prompt_tiles=ntp),
            grid=(n // tm,),
            in_specs=[pl.BlockSpec((tm, D_MODEL), lambda i: (jnp.minimum(i, ntp - 1), 0)),
                      pl.BlockSpec((tm, D_MODEL), lambda i: (jnp.maximum(i - ntp, 0), 0))] + const_specs,
            out_specs=out_specs,
            out_shape=out_shape,
            compiler_params=_params(("arbitrary",)),
            name="inproj",
        )(*x, *consts)
    grid_spec = pltpu.PrefetchScalarGridSpec(
        num_scalar_prefetch=1,
        grid=(n // tm,),
        in_specs=[pl.BlockSpec(memory_space=pl.ANY)] + const_specs,
        out_specs=out_specs,
        scratch_shapes=[pltpu.VMEM((tm, D_MODEL), F32), pltpu.VMEM((tm, D_MODEL), F32),
                        pltpu.SemaphoreType.DMA((2,))])
    return pl.pallas_call(
        functools.partial(_inproj_gather_kernel, n_prompt_tiles=ntp, n_tiles=n // tm, tm=tm),
        grid_spec=grid_spec,
        out_shape=out_shape,
        compiler_params=_params(("arbitrary",)),
        name="inproj_gather",
    )(dest, x, *consts)


def _head_out(hval, o, gain):
    mu = jnp.mean(hval, axis=1, keepdims=True)
    xc = hval - mu
    var = jnp.mean(xc * xc, axis=1, keepdims=True)
    return (jax.nn.sigmoid(o) * (xc * lax.rsqrt(var + LN_EPS) * gain)).astype(BF16)


def _split3(x):
    x1 = x.astype(BF16)
    r1 = x - x1.astype(F32)
    x2 = r1.astype(BF16)
    x3 = (r1 - x2.astype(F32)).astype(BF16)
    return x1, x2, x3


def _mlstm_prompt_kernel(*refs, rows, strip, chained):
    if chained:
        q_ref, k_ref, v_ref, gcol_ref, grow_ref, o_ref, gain_ref, _, hg_ref, c_ref, n_ref, m_ref = refs
    else:
        q_ref, k_ref, v_ref, gcol_ref, grow_ref, o_ref, gain_ref, hg_ref, c_ref, n_ref, m_ref = refs

    @pl.when(pl.program_id(1) == 0)
    def _():
        c_ref[...] = jnp.zeros_like(c_ref)
        n_ref[...] = jnp.zeros_like(n_ref)
        m_ref[...] = jnp.zeros_like(m_ref)

    t_idx = lax.broadcasted_iota(I32, (rows, 1), 0)
    s_idx = lax.broadcasted_iota(I32, (1, rows), 1)
    mask = s_idx <= t_idx
    lower = jnp.where(mask, 1.0, 0.0).astype(BF16)
    upper = jnp.where(t_idx <= s_idx, 1.0, 0.0).astype(BF16)
    gcol = gcol_ref[...]
    grow = grow_ref[...]
    bcol = sum(jnp.dot(lower, p, preferred_element_type=F32) for p in _split3(gcol))
    brow = sum(jnp.dot(p, upper, preferred_element_type=F32) for p in _split3(grow))
    ones = jnp.ones((rows, LANES), BF16)
    neg_inf = jnp.float32(-jnp.inf)
    heads = range(N_HEADS)
    sls = [slice(h * HEAD_DIM, (h + 1) * HEAD_DIM) for h in heads]
    qs = [q_ref[:, sl] for sl in sls]
    ks = [k_ref[:, sl] for sl in sls]
    vos = [jnp.concatenate([v_ref[:, sl], ones], axis=1) for sl in sls]
    b_cs = [bcol[:, N_HEADS + h:N_HEADS + h + 1] for h in heads]
    b_rs = [brow[N_HEADS + h:N_HEADS + h + 1, :] for h in heads]
    g_rs = [grow[h:h + 1, :] - b_rs[h] for h in heads]
    m_prevs = [m_ref[0, h:h + 1, 0:1] for h in heads]
    c_prevs = [c_ref[0, h] for h in heads]
    n_prevs = [n_ref[0, h] for h in heads]
    qks = [lax.dot_general(qs[h], ks[h], (((1,), (1,)), ((), ())), preferred_element_type=F32) for h in heads]
    inters = [jnp.dot(qs[h], jnp.concatenate([c_prevs[h].astype(BF16), n_prevs[h].astype(BF16)], axis=1),
                      preferred_element_type=F32) for h in heads]
    gms = [jnp.where(mask, g_rs[h], neg_inf) for h in heads]
    tops = [jnp.maximum(m_prevs[h], jnp.max(gms[h], axis=1, keepdims=True)) for h in heads]
    ss = [(qks[h] * jnp.exp(gms[h] - tops[h])).astype(BF16) for h in heads]
    intras = [jnp.dot(ss[h], vos[h], preferred_element_type=F32) for h in heads]
    b_lasts = [b_rs[h][:, rows - 1:rows] for h in heads]
    d_lasts = [b_lasts[h] - b_cs[h] + gcol[:, h:h + 1] for h in heads]
    m_news = [jnp.maximum(b_lasts[h] + m_prevs[h], jnp.max(d_lasts[h], axis=0, keepdims=True)) for h in heads]
    kws = [(ks[h].astype(F32) * jnp.exp(d_lasts[h] - m_news[h])).astype(BF16) for h in heads]
    upds = [lax.dot_general(kws[h], vos[h], (((0,), (0,)), ((), ())), preferred_element_type=F32) for h in heads]
    s_inters = [jnp.exp(m_prevs[h] - tops[h]) for h in heads]
    nums = [s_inters[h] * inters[h][:, 0:HEAD_DIM] + intras[h][:, 0:HEAD_DIM] for h in heads]
    dens = [s_inters[h] * inters[h][:, HEAD_DIM:HEAD_DIM + 1] + intras[h][:, HEAD_DIM:HEAD_DIM + 1] for h in heads]
    hvals = [nums[h] / jnp.maximum(jnp.abs(dens[h]), jnp.exp(-(b_cs[h] + tops[h]))) for h in heads]
    mus = [jnp.mean(hvals[h], axis=1, keepdims=True) for h in heads]
    xcs = [hvals[h] - mus[h] for h in heads]
    vars_ = [jnp.mean(xcs[h] * xcs[h], axis=1, keepdims=True) for h in heads]
    for h in heads:
        hn = xcs[h] * lax.rsqrt(vars_[h] + LN_EPS) * gain_ref[:, sls[h]]
        hg_ref[:, sls[h]] = (jax.nn.sigmoid(o_ref[:, sls[h]]) * hn).astype(BF16)
    for h in heads:
        s_last = jnp.exp(b_lasts[h] + m_prevs[h] - m_news[h])
        c_ref[0, h] = s_last * c_prevs[h] + upds[h][:, 0:HEAD_DIM]
        n_ref[0, h] = s_last * n_prevs[h] + upds[h][:, HEAD_DIM:]
        m_ref[0, h:h + 1, :] = jnp.broadcast_to(m_news[h], (1, LANES))


def _mlstm_sample_kernel(*refs, seq_len, window, block, chained):
    if chained:
        (q_ref, k_ref, kt_ref, v_ref, gcol_ref, grow_ref, mtok_ref, o_ref, gain_ref, c0_ref, n0_ref, _,
         hg_ref, c1_ref, n1_ref, m1_ref) = refs
    else:
        (q_ref, k_ref, kt_ref, v_ref, gcol_ref, grow_ref, mtok_ref, o_ref, gain_ref, c0_ref, n0_ref,
         hg_ref, c1_ref, n1_ref, m1_ref) = refs
    n_seq = window // seq_len
    shift = int(math.log2(seq_len))
    w0 = (pl.program_id(0) % (block // window)) * window
    r_idx = lax.broadcasted_iota(I32, (window, 1), 0)
    t_idx = w0 + r_idx
    s_idx = lax.broadcasted_iota(I32, (1, block), 1)
    same = jnp.right_shift(t_idx, shift) == jnp.right_shift(s_idx, shift)
    mask = same & (s_idx <= t_idx)
    mask_t = same & (t_idx <= s_idx)
    r_seq = jnp.right_shift(r_idx, shift)
    l_seq = jnp.right_shift(s_idx - w0, shift)
    gcol = gcol_ref[...]
    grow = grow_ref[...]
    mtok = mtok_ref[...]
    neg_inf = jnp.float32(-jnp.inf)
    for h in range(N_HEADS):
        sl = slice(h * HEAD_DIM, (h + 1) * HEAD_DIM)
        q = q_ref[:, sl]
        k = k_ref[:, sl]
        kt = kt_ref[sl, :]
        v = v_ref[:, sl]
        li_r = grow[h:h + 1, :]
        lf_r = grow[N_HEADS + h:N_HEADS + h + 1, :]
        li_c = gcol[:, h:h + 1]
        lf_c = gcol[:, N_HEADS + h:N_HEADS + h + 1]
        m_c = mtok[:, h:h + 1]
        b_c = jnp.sum(jnp.where(mask, lf_r, 0.0), axis=1, keepdims=True)
        b_r = jnp.sum(jnp.where(mask_t, lf_c, 0.0), axis=0, keepdims=True)
        dmat = jnp.where(mask, b_c - b_r + li_r, neg_inf)
        inter = b_c + m_c
        m_t = jnp.maximum(inter, jnp.max(dmat, axis=1, keepdims=True))
        s_inter = jnp.exp(inter - m_t)
        s = jnp.dot(q, kt, preferred_element_type=F32) * jnp.exp(dmat - m_t)
        intra = jnp.dot(s.astype(BF16), v, preferred_element_type=F32)
        qf = q.astype(F32)
        qc = qn = None
        for j in range(n_seq):
            qc_j = jnp.dot(q, c0_ref[j, h].astype(BF16), preferred_element_type=F32)
            qn_j = jnp.sum(qf * n0_ref[j, h:h + 1, :], axis=1, keepdims=True)
            qc = qc_j if j == 0 else jnp.where(r_seq == j, qc_j, qc)
            qn = qn_j if j == 0 else jnp.where(r_seq == j, qn_j, qn)
        num = s_inter * qc + intra
        den = s_inter * qn + jnp.sum(s, axis=1, keepdims=True)
        hval = num / jnp.maximum(jnp.abs(den), jnp.exp(-m_t))
        hg_ref[:, sl] = _head_out(hval, o_ref[:, sl], gain_ref[:, sl])
        kf = k.astype(F32)
        ktf = kt.astype(F32)
        for j in range(n_seq):
            lsel = l_seq == j
            m_j = mtok[j * seq_len:j * seq_len + 1, h:h + 1]
            b_last = jnp.sum(jnp.where(lsel, lf_r, 0.0), axis=1, keepdims=True)
            d_r = jnp.where(lsel, b_last - b_r + li_r, neg_inf)
            d_c = jnp.where(r_seq == j, b_last - b_c + li_c, neg_inf)
            m_new = jnp.maximum(b_last + m_j, jnp.max(d_r, axis=1, keepdims=True))
            s_last = jnp.exp(b_last + m_j - m_new)
            kwt = (ktf * jnp.exp(d_r - m_new)).astype(BF16)
            c1_ref[j, h] = s_last * c0_ref[j, h] + jnp.dot(kwt, v, preferred_element_type=F32)
            n1_ref[j, h:h + 1, :] = (s_last * n0_ref[j, h:h + 1, :]
                                     + jnp.sum(kf * jnp.exp(d_c - m_new), axis=0, keepdims=True))
            m1_ref[j, h:h + 1, :] = jnp.broadcast_to(m_new, (1, LANES))


def _chain(c_all):
    if c_all is None:
        return [], []
    return [c_all], [pl.BlockSpec(memory_space=pl.ANY)]


def _mlstm_prompt(q, k, v, gcol, grow, o, gain, c_all, layer, n_seq, seq_len):
    n = n_seq * seq_len
    chunk = math.gcd(seq_len, MLSTM_PROMPT_CHUNK)
    nc = seq_len // chunk
    row = lambda w: pl.BlockSpec((chunk, w), lambda b, c: (b * nc + c, 0))
    st = lambda *tail: pl.BlockSpec((1,) + tail, lambda b, c: (b,) + (0,) * len(tail))
    extra, extra_specs = _chain(c_all)
    n_in = 7
    return pl.pallas_call(
        functools.partial(_mlstm_prompt_kernel, rows=chunk, strip=min(chunk, MLSTM_STRIP), chained=bool(extra)),
        grid=(n_seq, nc),
        in_specs=[row(D_MLSTM), row(D_MLSTM), row(D_MLSTM), row(LANES),
                  pl.BlockSpec((GATE_ROWS, chunk), lambda b, c: (0, b * nc + c)),
                  row(D_MLSTM), _layer_spec(gain, layer)] + extra_specs,
        out_specs=(row(D_MLSTM),
                   pl.BlockSpec((None, 1, N_HEADS, HEAD_DIM, HEAD_DIM), lambda b, c: (layer, b, 0, 0, 0)),
                   st(N_HEADS, HEAD_DIM, LANES), st(N_HEADS, LANES)),
        out_shape=(jax.ShapeDtypeStruct((n, D_MLSTM), BF16),
                   jax.ShapeDtypeStruct((DEPTH, n_seq, N_HEADS, HEAD_DIM, HEAD_DIM), F32),
                   jax.ShapeDtypeStruct((n_seq, N_HEADS, HEAD_DIM, LANES), F32),
                   jax.ShapeDtypeStruct((n_seq, N_HEADS, LANES), F32)),
        input_output_aliases={n_in: 1} if extra else {},
        compiler_params=_params(("parallel", "arbitrary")),
        name="mlstm_prompt",
    )(q, k, v, gcol, grow, o, gain, *extra)


def _mlstm_sample(q, k, kt, v, gcol, grow, mtok, o, gain, c0, n0, c_all, layer, first_row, n_seq, seq_len):
    n = n_seq * seq_len
    window, block = SAMPLE_WINDOW, SAMPLE_BLOCK_ROWS
    assert n % block == 0 and first_row % block == 0 and window % seq_len == 0
    per_win = window // seq_len
    sub = block // window
    row = lambda w: pl.BlockSpec((window, w), lambda i: (first_row // window + i, 0))
    extra, extra_specs = _chain(c_all)
    n_in = 11
    return pl.pallas_call(
        functools.partial(_mlstm_sample_kernel, seq_len=seq_len, window=window, block=block, chained=bool(extra)),
        grid=(n // window,),
        in_specs=[row(D_MLSTM), row(D_MLSTM),
                  pl.BlockSpec((D_MLSTM, block), lambda i: (0, i // sub)),
                  pl.BlockSpec((block, D_MLSTM), lambda i: (first_row // block + i // sub, 0)),
                  row(LANES),
                  pl.BlockSpec((GATE_ROWS, block), lambda i: (0, first_row // block + i // sub)),
                  pl.BlockSpec((window, LANES), lambda i: (i, 0)),
                  row(D_MLSTM), _layer_spec(gain, layer),
                  pl.BlockSpec((None, per_win, N_HEADS, HEAD_DIM, HEAD_DIM), lambda i: (layer, i, 0, 0, 0)),
                  pl.BlockSpec((None, per_win, N_HEADS, HEAD_DIM), lambda i: (layer, i, 0, 0))] + extra_specs,
        out_specs=(pl.BlockSpec((window, D_MLSTM), lambda i: (i, 0)),
                   pl.BlockSpec((None, per_win, N_HEADS, HEAD_DIM, HEAD_DIM), lambda i: (layer, i, 0, 0, 0)),
                   pl.BlockSpec((per_win, N_HEADS, HEAD_DIM), lambda i: (i, 0, 0)),
                   pl.BlockSpec((per_win, N_HEADS, LANES), lambda i: (i, 0, 0))),
        out_shape=(jax.ShapeDtypeStruct((n, D_MLSTM), BF16),
                   jax.ShapeDtypeStruct((DEPTH, n_seq, N_HEADS, HEAD_DIM, HEAD_DIM), F32),
                   jax.ShapeDtypeStruct((n_seq, N_HEADS, HEAD_DIM), F32),
                   jax.ShapeDtypeStruct((n_seq, N_HEADS, LANES), F32)),
        input_output_aliases={n_in: 1} if extra else {},
        compiler_params=_params(("parallel",)),
        name="mlstm_sample",
    )(q, k, kt, v, gcol, grow, mtok, o, gain, c0, n0, *extra)


def _pool_prompt_kernel(u_ref, prev_ref, out_ref, tail_ref, ext_ref, *, tm, tiles_per_seq):
    tile = pl.program_id(0) % tiles_per_seq
    head = 16
    ext_ref[0:head, :] = jnp.where(tile == 0, 0.0, prev_ref[...])
    ext_ref[head:, :] = u_ref[...]
    tail_ref[...] = u_ref[tm - head:tm, :]
    pos = tile * tm + lax.broadcasted_iota(I32, (tm, 1), 0)
    for g, w in enumerate(POOL_WINDOWS):
        sl = slice(g * POOL_GROUP_DIM, (g + 1) * POOL_GROUP_DIM)
        acc = ext_ref[head:head + tm, sl]
        for d in range(1, w):
            acc = acc + ext_ref[head - d:head - d + tm, sl]
        cnt = jnp.minimum(pos + 1, w).astype(F32)
        out_ref[:, sl] = (acc / cnt - u_ref[:, sl]).astype(BF16)


def _pool_prompt(u, n_seq, seq_len):
    n = n_seq * seq_len
    tm = _row_tile(seq_len)
    head = 16
    return pl.pallas_call(
        functools.partial(_pool_prompt_kernel, tm=tm, tiles_per_seq=seq_len // tm),
        grid=(n // tm,),
        in_specs=[pl.BlockSpec((tm, D_POOL), lambda i: (i, 0)),
                  pl.BlockSpec((head, D_POOL), lambda i: (jnp.maximum(i * (tm // head) - 1, 0), 0))],
        out_specs=(pl.BlockSpec((tm, D_POOL), lambda i: (i, 0)),
                   pl.BlockSpec((head, D_POOL), lambda i: (i // (seq_len // tm), 0))),
        out_shape=(jax.ShapeDtypeStruct((n, D_POOL), BF16), jax.ShapeDtypeStruct((n_seq * head, D_POOL), F32)),
        scratch_shapes=[pltpu.VMEM((head + tm, D_POOL), F32)],
        compiler_params=_params(("arbitrary",)),
        name="pool_prompt",
    )(u, u)


def _pool_sample_kernel(u_ref, buf_ref, out_ref, nbuf_ref, ext_ref, *, seq_len, start):
    ext_ref[:, 0:POOL_BUF, :] = buf_ref[...]
    ext_ref[:, POOL_BUF:POOL_BUF + seq_len, :] = u_ref[...]
    pos = start + lax.broadcasted_iota(I32, (1, seq_len, 1), 1)
    for g, w in enumerate(POOL_WINDOWS):
        sl = slice(g * POOL_GROUP_DIM, (g + 1) * POOL_GROUP_DIM)
        acc = ext_ref[:, POOL_BUF:POOL_BUF + seq_len, sl]
        for d in range(1, w):
            acc = acc + ext_ref[:, POOL_BUF - d:POOL_BUF - d + seq_len, sl]
        cnt = jnp.minimum(pos + 1, w).astype(F32)
        out_ref[:, :, sl] = (acc / cnt - u_ref[:, :, sl]).astype(BF16)
    nbuf_ref[...] = ext_ref[:, seq_len:seq_len + POOL_BUF, :]


def _pool_sample(u3, buf, layer, start):
    n_seq, seq_len, _ = u3.shape
    bs = _row_tile(n_seq, 32)
    spec = lambda r: pl.BlockSpec((bs, r, D_POOL), lambda i: (i, 0, 0))
    return pl.pallas_call(
        functools.partial(_pool_sample_kernel, seq_len=seq_len, start=start),
        grid=(n_seq // bs,),
        in_specs=[spec(seq_len), pl.BlockSpec((None, bs, POOL_BUF, D_POOL), lambda i: (layer, i, 0, 0))],
        out_specs=(spec(seq_len), spec(POOL_BUF)),
        out_shape=(jax.ShapeDtypeStruct((n_seq, seq_len, D_POOL), BF16),
                   jax.ShapeDtypeStruct((n_seq, POOL_BUF, D_POOL), F32)),
        scratch_shapes=[pltpu.VMEM((bs, POOL_BUF + seq_len + 5, D_POOL), F32)],
        compiler_params=_params(("parallel",)),
        name="pool_sample",
    )(u3, buf)


def _layer_norm(y, g, b):
    mu = jnp.mean(y, axis=1, keepdims=True)
    yc = y - mu
    var = jnp.mean(yc * yc, axis=1, keepdims=True)
    return yc * lax.rsqrt(var + LN_EPS) * g + b


def _route(logits_t):
    tokens = logits_t.shape[1]
    grp = lax.broadcasted_iota(I32, (8, tokens), 0)
    live = grp < N_EXPERT_GROUPS
    neg_inf = jnp.float32(-jnp.inf)
    lm = [jnp.where(live, logits_t[8 * m:8 * m + 8, :], neg_inf) for m in range(EXPERTS_PER_GROUP)]
    mx = jnp.max(jnp.maximum(jnp.maximum(lm[0], lm[1]), jnp.maximum(lm[2], lm[3])), axis=0, keepdims=True)
    ex = [jnp.exp(l - mx) for l in lm]
    tot = jnp.sum(ex[0] + ex[1] + ex[2] + ex[3], axis=0, keepdims=True)
    p = [e / tot for e in ex]
    top1 = jnp.maximum(jnp.maximum(p[0], p[1]), jnp.maximum(p[2], p[3]))
    i1 = jnp.where(p[0] == top1, 0, jnp.where(p[1] == top1, 1, jnp.where(p[2] == top1, 2, 3)))
    r = [jnp.where(i1 == m, -1.0, p[m]) for m in range(EXPERTS_PER_GROUP)]
    top2 = jnp.maximum(jnp.maximum(r[0], r[1]), jnp.maximum(r[2], r[3]))
    i2 = jnp.where(r[0] == top2, 0, jnp.where(r[1] == top2, 1, jnp.where(r[2] == top2, 2, 3)))
    gscore = jnp.where(live, top1 + top2, neg_inf)
    gmax = jnp.max(gscore, axis=0, keepdims=True)
    gsel = jnp.min(jnp.where(gscore == gmax, grp, 8), axis=0, keepdims=True)
    chosen = grp == gsel
    tsum = top1 + top2
    w1 = top1 / tsum
    w2 = top2 / tsum
    first_is_lo = i1 < i2
    lo = jnp.minimum(i1, i2)
    hi = jnp.maximum(i1, i2)
    pair = jnp.where(lo == 0, hi - 1, jnp.where(lo == 1, hi + 1, 5))
    pick = lambda a: jnp.sum(jnp.where(chosen, a, jnp.zeros_like(a)), axis=0, keepdims=True)
    cls = pick(grp * len(PAIRS) + pair)
    w_lo = pick(jnp.where(first_is_lo, w1, w2))
    w_hi = pick(jnp.where(first_is_lo, w2, w1))
    return cls, w_lo, w_hi


def _mix_kernel(hgp_ref, hgs_ref, plp_ref, pls_ref, ga_ref, gb_ref, x_ref, wpool_ref, pscale_ref, wa_ref, wb_ref,
                wout_ref, g1_ref, b1_ref, wr_ref, br_ref,
                dest_ref, cnt_ref, xs_ref,
                rows0_ref, rows1_ref, dvm_ref, dsm0_ref, dsm1_ref, carry_ref, row_sem, idx_sem,
                *, tm, n_prompt_tiles, n_tiles, capacity):
    i = pl.program_id(0)
    rows = (rows0_ref, rows1_ref)
    dsm = (dsm0_ref, dsm1_ref)
    spare = N_CLASSES * capacity
    sent = lambda s: pltpu.make_async_copy(rows[s], xs_ref.at[pl.ds(0, tm), :], row_sem.at[s])

    def send(s, unrolled):
        def one(r, carry=0):
            pltpu.make_async_copy(rows[s].at[pl.ds(r, 1), :], xs_ref.at[pl.ds(dsm[s][r], 1), :],
                                  row_sem.at[s]).start()
            return carry
        if unrolled:
            for r in range(tm):
                one(r)
        else:
            lax.fori_loop(0, tm, one, 0, unroll=8)

    @pl.when(i == 0)
    def _():
        carry_ref[...] = jnp.zeros_like(carry_ref)
        rows0_ref[...] = jnp.zeros_like(rows0_ref)
        rows1_ref[...] = jnp.zeros_like(rows1_ref)

        def spare_rows(r, carry):
            dsm0_ref[r] = spare + r
            return carry

        lax.fori_loop(0, tm, spare_rows, 0)
        send(0, False)
        dvm_ref[...] = spare + tm + lax.broadcasted_iota(I32, (8, tm), 1)
        pltpu.make_async_copy(dvm_ref.at[0], dsm1_ref, idx_sem).start()

    for parity in (0, 1):
        pl.when(i % 2 == parity)(functools.partial(
            _mix_step, parity, i, hgp_ref, hgs_ref, plp_ref, pls_ref, ga_ref, gb_ref, x_ref, wpool_ref, pscale_ref,
            wa_ref, wb_ref, wout_ref, g1_ref, b1_ref, wr_ref, br_ref, dest_ref, cnt_ref, rows, dvm_ref, dsm,
            carry_ref, idx_sem, send, sent, tm, n_prompt_tiles, n_tiles, capacity))


def _mix_step(cur, i, hgp_ref, hgs_ref, plp_ref, pls_ref, ga_ref, gb_ref, x_ref, wpool_ref, pscale_ref, wa_ref,
              wb_ref, wout_ref, g1_ref, b1_ref, wr_ref, br_ref, dest_ref, cnt_ref, rows, dvm_ref, dsm, carry_ref,
              idx_sem, send, sent, tm, n_prompt_tiles, n_tiles, capacity):
    slots_arrived = lambda s: pltpu.make_async_copy(dvm_ref.at[0], dsm[s], idx_sem)
    slots_arrived(1 - cur).wait()
    send(1 - cur, True)
    is_prompt = i < n_prompt_tiles
    hg = jnp.where(is_prompt, hgp_ref[...], hgs_ref[...])
    pooled = jnp.where(is_prompt, plp_ref[...], pls_ref[...])
    ya = jnp.dot(hg, wa_ref[...], preferred_element_type=F32)
    parts = []
    for g in range(len(POOL_WINDOWS)):
        sl = slice(g * POOL_GROUP_DIM, (g + 1) * POOL_GROUP_DIM)
        parts.append(jnp.dot(pooled[:, sl], wpool_ref[g], preferred_element_type=F32))
    pl_lin = jnp.concatenate(parts, axis=1) * pscale_ref[...]
    yb = jnp.dot(pl_lin.astype(BF16), wb_ref[...], preferred_element_type=F32)
    mix = jax.nn.sigmoid(ga_ref[...]) * ya + jax.nn.sigmoid(gb_ref[...]) * yb
    res = jnp.dot(mix.astype(BF16), wout_ref[...], preferred_element_type=F32)
    x1 = _layer_norm(ALPHA * x_ref[...] + res, g1_ref[...], b1_ref[...])

    logits_t = lax.dot_general(wr_ref[...], x1, (((1,), (1,)), ((), ())), preferred_element_type=F32,
                               precision=lax.Precision.HIGHEST) + br_ref[...]
    cls, w_lo, w_hi = _route(logits_t)

    onehot = lax.broadcasted_iota(I32, (CLASS_ROWS, tm), 0) == cls
    earlier = lax.broadcasted_iota(I32, (tm, tm), 0) < lax.broadcasted_iota(I32, (tm, tm), 1)
    before = jnp.dot(jnp.where(onehot, 1.0, 0.0).astype(BF16), jnp.where(earlier, 1.0, 0.0).astype(BF16),
                     preferred_element_type=F32)
    seen = carry_ref[:, 0:1]
    rank = jnp.sum(jnp.where(onehot, before + seen, 0.0), axis=0, keepdims=True)
    carry_ref[...] = carry_ref[...] + jnp.sum(jnp.where(onehot, 1.0, 0.0), axis=1, keepdims=True)
    cnt_ref[...] = carry_ref[...]
    dest = cls * capacity + rank.astype(I32)
    dest_ref[...] = jnp.broadcast_to(dest, (8, tm))

    sent(cur).wait()
    wrows = jnp.concatenate([w_lo, w_hi, jnp.zeros((LANES - 2, tm), F32)], axis=0)
    rows[cur][:, 0:D_MODEL] = x1
    rows[cur][:, D_MODEL:ROW_EXT] = wrows.T
    dvm_ref[...] = jnp.broadcast_to(dest, (8, tm))
    slots_arrived(cur).start()

    @pl.when(i == n_tiles - 1)
    def _():
        slots_arrived(cur).wait()
        send(cur, False)
        sent(cur).wait()
        sent(1 - cur).wait()


def _mix(hg_p, hg_s, pooled_p, pooled_s, ga, gb, x, w, layer, capacity):
    n = x.shape[0]
    n_p = hg_p.shape[0]
    tm = _row_tile(math.gcd(n_p, n - n_p))
    ntp = n_p // tm
    row = lambda w: pl.BlockSpec((tm, w), lambda i: (i, 0))
    prow = lambda w: pl.BlockSpec((tm, w), lambda i: (jnp.minimum(i, ntp - 1), 0))
    srow = lambda w: pl.BlockSpec((tm, w), lambda i: (jnp.maximum(i - ntp, 0), 0))
    per_layer = (w["w_pool"], w["pool_scale"], w["w_proj_a"], w["w_proj_b"], w["w_out"], w["ln1_g"], w["ln1_b"])
    shared = (w["w_router_t"], w["b_router_col"])
    consts = per_layer + shared
    return pl.pallas_call(
        functools.partial(_mix_kernel, tm=tm, n_prompt_tiles=ntp, n_tiles=n // tm, capacity=capacity),
        grid=(n // tm,),
        in_specs=[prow(D_MLSTM), srow(D_MLSTM), prow(D_POOL), srow(D_POOL), row(D_MODEL), row(D_MODEL), row(D_MODEL)]
                 + [_layer_spec(c, layer) for c in per_layer] + [_const_spec(c.shape) for c in shared],
        out_specs=(pl.BlockSpec((8, tm), lambda i: (0, i)),
                   pl.BlockSpec((CLASS_ROWS, LANES), lambda i: (0, 0)),
                   pl.BlockSpec(memory_space=pl.ANY)),
        out_shape=(jax.ShapeDtypeStruct((8, n), I32),
                   jax.ShapeDtypeStruct((CLASS_ROWS, LANES), F32),
                   jax.ShapeDtypeStruct((N_CLASSES * capacity + 2 * tm, ROW_EXT), F32)),
        scratch_shapes=[pltpu.VMEM((tm, ROW_EXT), F32), pltpu.VMEM((tm, ROW_EXT), F32), pltpu.VMEM((8, tm), I32),
                        pltpu.SMEM((tm,), I32), pltpu.SMEM((tm,), I32), pltpu.VMEM((CLASS_ROWS, LANES), F32),
                        pltpu.SemaphoreType.DMA((2,)), pltpu.SemaphoreType.DMA],
        compiler_params=_params(("arbitrary",)),
        name="mix",
    )(hg_p, hg_s, pooled_p, pooled_s, ga, gb, x, *consts)


def _moe_kernel(blk_ref, elo_ref, ehi_ref, nvalid_ref, ntiles_ref,
                xs_ref, wg_lo, wu_lo, wd_lo, wg_hi, wu_hi, wd_hi, g2_ref, b2_ref, ys_ref):
    i = pl.program_id(0)

    @pl.when(i < ntiles_ref[0])
    def _():
        valid = lax.broadcasted_iota(I32, (MOE_TILE, 1), 0) < nvalid_ref[i]
        xe = xs_ref[...]
        x = jnp.where(valid, xe[:, 0:D_MODEL], 0.0)
        w_lo = jnp.where(valid, xe[:, D_MODEL:D_MODEL + 1], 0.0)
        w_hi = jnp.where(valid, xe[:, D_MODEL + 1:D_MODEL + 2], 0.0)
        xb = x.astype(BF16)

        def expert(wg, wu, wd, w):
            g = jnp.dot(xb, wg[0], preferred_element_type=F32)
            u = jnp.dot(xb, wu[0], preferred_element_type=F32)
            hid = (g * jax.nn.sigmoid(g)) * u * w
            return jnp.dot(hid.astype(BF16), wd[0], preferred_element_type=F32)

        y = expert(wg_lo, wu_lo, wd_lo, w_lo) + expert(wg_hi, wu_hi, wd_hi, w_hi)
        ys_ref[...] = _layer_norm(ALPHA * x + y, g2_ref[...], b2_ref[...])


def _moe(xs, tables, w, layer, max_tiles):
    blk, e_lo, e_hi, n_valid, n_tiles = tables
    up = lambda sel: pl.BlockSpec((None, 1, D_MODEL, D_EXPERT),
                                  lambda i, b, lo, hi, nv, nt: (layer, (lo, hi)[sel][i], 0, 0))
    down = lambda sel: pl.BlockSpec((None, 1, D_EXPERT, D_MODEL),
                                    lambda i, b, lo, hi, nv, nt: (layer, (lo, hi)[sel][i], 0, 0))
    grid_spec = pltpu.PrefetchScalarGridSpec(
        num_scalar_prefetch=5,
        grid=(max_tiles,),
        in_specs=[pl.BlockSpec((MOE_TILE, ROW_EXT), lambda i, b, *_: (b[i], 0)),
                  up(0), up(0), down(0), up(1), up(1), down(1),
                  _layer_spec(w["ln2_g"], layer), _layer_spec(w["ln2_b"], layer)],
        out_specs=pl.BlockSpec((MOE_TILE, D_MODEL), lambda i, b, *_: (b[i], 0)))
    return pl.pallas_call(
        _moe_kernel,
        grid_spec=grid_spec,
        out_shape=jax.ShapeDtypeStruct((xs.shape[0], D_MODEL), F32),
        compiler_params=_params(("arbitrary",)),
        name="moe",
    )(blk, e_lo, e_hi, n_valid, n_tiles, xs, w["w_e_gate"], w["w_e_up"], w["w_e_down"],
      w["w_e_gate"], w["w_e_up"], w["w_e_down"], w["ln2_g"], w["ln2_b"])


def _tile_tables(counts, capacity, max_tiles):
    cnt = counts[:N_CLASSES, 0].astype(I32)
    tiles = (cnt + MOE_TILE - 1) // MOE_TILE
    ends = jnp.cumsum(tiles)
    starts = ends - tiles
    n_tiles = ends[-1]
    t = jnp.minimum(jnp.arange(max_tiles, dtype=I32), jnp.maximum(n_tiles - 1, 0))
    cls = jnp.sum((ends[None, :] <= t[:, None]).astype(I32), axis=1)
    onehot = (jnp.arange(N_CLASSES, dtype=I32)[None, :] == cls[:, None]).astype(I32)
    within = t - jnp.sum(onehot * starts[None, :], axis=1)
    n_valid = jnp.clip(jnp.sum(onehot * cnt[None, :], axis=1) - within * MOE_TILE, 0, MOE_TILE)
    blk = cls * (capacity // MOE_TILE) + within
    grp = cls // len(PAIRS)
    pair = cls % len(PAIRS)
    lo = jnp.where(pair < 3, 0, jnp.where(pair < 5, 1, 2))
    hi = jnp.where(pair < 3, pair + 1, jnp.where(pair < 5, pair - 1, 3))
    return (blk, grp * EXPERTS_PER_GROUP + lo, grp * EXPERTS_PER_GROUP + hi, n_valid,
            n_tiles.reshape(1).astype(I32))


def _unpermute_kernel(dest_ref, ys_ref, prompt_ref, sample_ref, sem, *, tm, n_prompt_tiles):
    i = pl.program_id(0)

    def gather(out_ref):
        def fetch(r, carry):
            pltpu.make_async_copy(ys_ref.at[pl.ds(dest_ref[i * tm + r], 1), :], out_ref.at[pl.ds(r, 1), :],
                                  sem).start()
            return carry

        lax.fori_loop(0, tm, fetch, 0, unroll=8)
        pltpu.make_async_copy(ys_ref.at[pl.ds(0, tm), :], out_ref, sem).wait()

    pl.when(i < n_prompt_tiles)(functools.partial(gather, prompt_ref))
    pl.when(i >= n_prompt_tiles)(functools.partial(gather, sample_ref))


def _unpermute(ys, dest, n_prompt):
    n = dest.shape[0]
    tm = _row_tile(math.gcd(n_prompt, n - n_prompt))
    ntp = n_prompt // tm
    grid_spec = pltpu.PrefetchScalarGridSpec(
        num_scalar_prefetch=1,
        grid=(n // tm,),
        in_specs=[pl.BlockSpec(memory_space=pl.ANY)],
        out_specs=(pl.BlockSpec((tm, D_MODEL), lambda i, d: (jnp.minimum(i, ntp - 1), 0)),
                   pl.BlockSpec((tm, D_MODEL), lambda i, d: (jnp.maximum(i - ntp, 0), 0))),
        scratch_shapes=[pltpu.SemaphoreType.DMA])
    return pl.pallas_call(
        functools.partial(_unpermute_kernel, tm=tm, n_prompt_tiles=ntp),
        grid_spec=grid_spec,
        out_shape=(jax.ShapeDtypeStruct((n_prompt, D_MODEL), F32),
                   jax.ShapeDtypeStruct((n - n_prompt, D_MODEL), F32)),
        compiler_params=_params(("arbitrary",)),
        name="unpermute",
    )(dest, ys)


def _prepare_weights(w_in, b_gate, hn_gain, w_pool, pool_scale, w_proj_a, w_proj_b, w_out, ln1_g, ln1_b,
                     ln2_g, ln2_b, w_router, b_router, w_e_gate, w_e_up, w_e_down):
    w_qkvo = w_in[:, :, :_G0].astype(BF16)
    w_rest = w_in[:, :, _U0:].astype(BF16)
    w_gate = w_in[:, :, _G0:_U0]
    w_gate_p = jnp.pad(w_gate, ((0, 0), (0, 0), (0, LANES - 2 * N_HEADS))).astype(BF16)
    w_gate_t = jnp.pad(jnp.swapaxes(w_gate, 1, 2), ((0, 0), (0, GATE_ROWS - 2 * N_HEADS), (0, 0))).astype(BF16)
    w_key_t = jnp.swapaxes(w_in[:, :, D_MLSTM:2 * D_MLSTM], 1, 2).astype(BF16)
    b_row = jnp.pad(b_gate, ((0, 0), (0, LANES - 2 * N_HEADS)))[:, None, :]
    b_col = jnp.pad(b_gate, ((0, 0), (0, GATE_ROWS - 2 * N_HEADS)))[:, :, None]
    wr = w_router.T.reshape(N_EXPERT_GROUPS, EXPERTS_PER_GROUP, D_MODEL).swapaxes(0, 1)
    wr = jnp.pad(wr, ((0, 0), (0, 8 - N_EXPERT_GROUPS), (0, 0))).reshape(ROUTER_ROWS, D_MODEL)
    br = b_router.reshape(N_EXPERT_GROUPS, EXPERTS_PER_GROUP).T
    br = jnp.pad(br, ((0, 0), (0, 8 - N_EXPERT_GROUPS))).reshape(ROUTER_ROWS, 1)
    per_row = lambda a: a.reshape(DEPTH, 1, -1)
    return dict(
        w_qkvo=w_qkvo, w_rest=w_rest, w_gate=w_gate_p, w_gate_t=w_gate_t, w_key_t=w_key_t, b_row=b_row, b_col=b_col,
        gain=per_row(hn_gain), w_pool=w_pool.astype(BF16), pool_scale=per_row(pool_scale),
        w_proj_a=w_proj_a.astype(BF16), w_proj_b=w_proj_b.astype(BF16), w_out=w_out.astype(BF16),
        ln1_g=per_row(ln1_g), ln1_b=per_row(ln1_b), ln2_g=per_row(ln2_g), ln2_b=per_row(ln2_b),
        w_router_t=wr, b_router_col=br,
        w_e_gate=w_e_gate.astype(BF16), w_e_up=w_e_up.astype(BF16), w_e_down=w_e_down.astype(BF16))


def kernel(x_prompt, x_sample, state_C, state_n, state_m, state_pool, w_in, b_gate, hn_gain, w_pool, pool_scale,
           w_proj_a, w_proj_b, w_out, ln1_g, ln1_b, ln2_g, ln2_b, w_router, b_router, w_e_gate, w_e_up, w_e_down):
    w = _prepare_weights(w_in, b_gate, hn_gain, w_pool, pool_scale, w_proj_a, w_proj_b, w_out, ln1_g, ln1_b,
                         ln2_g, ln2_b, w_router, b_router, w_e_gate, w_e_up, w_e_down)
    n_pseq, p_len, _ = x_prompt.shape
    n_sseq, s_len, _ = x_sample.shape
    n_p = n_pseq * p_len
    n_s = n_sseq * s_len
    n = n_p + n_s
    capacity = -(-n // MOE_TILE) * MOE_TILE
    max_tiles = n // MOE_TILE + N_CLASSES
    np_, mp, bp, ns, ms, bs = [], [], [], [], [], []
    c_p = c_s = None
    ys = (x_prompt.reshape(n_p, D_MODEL), x_sample.reshape(n_s, D_MODEL))
    dest = None
    for l in range(DEPTH):
        q, k, v, o, u, ga, gb, gcol, grow, kt, x = _inproj(ys, w, l, n_p, None if l == 0 else dest[0])
        hg_p, c_p, n1p, m1p = _mlstm_prompt(q, k, v, gcol, grow, o, w["gain"], c_p, l, n_pseq, p_len)
        mtok = jnp.pad(jnp.repeat(state_m[l], s_len, axis=0), ((0, 0), (0, LANES - N_HEADS)))
        hg_s, c_s, n1s, m1s = _mlstm_sample(q, k, kt, v, gcol, grow, mtok, o, w["gain"], state_C, state_n, c_s,
                                            l, n_p, n_sseq, s_len)
        pooled_p, tail = _pool_prompt(u, n_pseq, p_len)
        pooled_s, nbuf = _pool_sample(u[n_p:].reshape(n_sseq, s_len, D_POOL), state_pool, l, PAST_LEN)
        dest, counts, xs = _mix(hg_p, hg_s, pooled_p, pooled_s.reshape(n_s, D_POOL), ga, gb, x, w, l, capacity)
        ys = _moe(xs, _tile_tables(counts, capacity, max_tiles), w, l, max_tiles)
        np_.append(n1p[:, :, :, 0])
        mp.append(m1p[:, :, 0])
        bp.append(tail.reshape(n_pseq, -1, D_POOL)[:, -POOL_BUF:])
        ns.append(n1s)
        ms.append(m1s[:, :, 0])
        bs.append(nbuf)
    y_p, y_s = _unpermute(ys, dest[0], n_p)
    st = jnp.stack
    return (y_p.reshape(n_pseq, p_len, D_MODEL), y_s.reshape(n_sseq, s_len, D_MODEL),
            c_p, st(np_), st(mp), st(bp), c_s, st(ns), st(ms), st(bs))
```

```python
import functools
import math

import jax
import jax.numpy as jnp
from jax import lax
from jax.experimental import pallas as pl
from jax.experimental.pallas import tpu as pltpu

F32 = jnp.float32
BF16 = jnp.bfloat16
I32 = jnp.int32

D_MODEL = 1024
N_HEADS = 4
HEAD_DIM = 256
D_MLSTM = N_HEADS * HEAD_DIM
POOL_WINDOWS = (2, 4, 8, 16)
POOL_GROUP_DIM = 128
D_POOL = len(POOL_WINDOWS) * POOL_GROUP_DIM
POOL_BUF = 15
N_EXPERTS = 16
N_EXPERT_GROUPS = 4
EXPERTS_PER_GROUP = 4
D_EXPERT = 512
DEPTH = 4
PAST_LEN = 16384
ALPHA = (2 * DEPTH) ** 0.25
LN_EPS = 1e-5
K_SCALE = HEAD_DIM ** -0.5

LANES = 128
GATE_ROWS = 16
ROUTER_ROWS = 32
VMEM_LIMIT = 52 * 1024 * 1024
MLSTM_PROMPT_CHUNK = 256
MLSTM_STRIP = 256
SAMPLE_BLOCK_ROWS = 128
SAMPLE_WINDOW = 16

PAIRS = ((0, 1), (0, 2), (0, 3), (1, 2), (1, 3), (2, 3))
N_CLASSES = N_EXPERT_GROUPS * len(PAIRS)
CLASS_ROWS = 32
MOE_TILE = 256
ROW_EXT = D_MODEL + LANES

_G0 = 4 * D_MLSTM
_U0 = _G0 + 2 * N_HEADS
_SEGS = ((0, 0, 1024), (0, 1024, 2048), (0, 2048, 3072), (0, 3072, 4096), (1, 0, 512), (1, 512, 1536), (1, 1536, 2560))


def _params(sem, **kw):
    return pltpu.CompilerParams(dimension_semantics=sem, vmem_limit_bytes=VMEM_LIMIT, **kw)


def _const_spec(shape):
    nd = len(shape)
    return pl.BlockSpec(shape, lambda *_: (0,) * nd, pipeline_mode=pl.Buffered(1))


def _layer_spec(stacked, layer):
    tail = stacked.shape[1:]
    return pl.BlockSpec((None,) + tail, lambda *_: (layer,) + (0,) * len(tail), pipeline_mode=pl.Buffered(1))


def _row_tile(n, cap=512):
    t = cap
    while n % t:
        t //= 2
    return t


def _inproj_kernel(xp_ref, xs_ref, wa_ref, wb_ref, wg_ref, wgt_ref, wkt_ref, brow_ref, bcol_ref,
                   q_ref, k_ref, v_ref, o_ref, u_ref, ga_ref, gb_ref, gcol_ref, grow_ref, kt_ref, x_ref,
                   *, n_prompt_tiles):
    x32 = jnp.where(pl.program_id(0) < n_prompt_tiles, xp_ref[...], xs_ref[...])
    x_ref[...] = x32
    _inproj_body(x32.astype(BF16), wa_ref, wb_ref, wg_ref, wgt_ref, wkt_ref, brow_ref, bcol_ref,
                 q_ref, k_ref, v_ref, o_ref, u_ref, ga_ref, gb_ref, gcol_ref, grow_ref, kt_ref, n_prompt_tiles)


def _inproj_gather_kernel(dest_ref, ys_ref, wa_ref, wb_ref, wg_ref, wgt_ref, wkt_ref, brow_ref, bcol_ref,
                          q_ref, k_ref, v_ref, o_ref, u_ref, ga_ref, gb_ref, gcol_ref, grow_ref, kt_ref, x_ref,
                          xbuf0_ref, xbuf1_ref, sem, *, n_prompt_tiles, n_tiles, tm):
    i = pl.program_id(0)
    bufs = (xbuf0_ref, xbuf1_ref)

    def request(tile, slot, lo, hi, unrolled):
        def one(r, carry=0):
            pltpu.make_async_copy(ys_ref.at[pl.ds(dest_ref[tile * tm + r], 1), :],
                                  bufs[slot].at[pl.ds(r, 1), :], sem.at[slot]).start()
            return carry
        if unrolled:
            for r in range(lo, hi):
                one(r)
        else:
            lax.fori_loop(lo, hi, one, 0, unroll=8)

    arrived = lambda slot: pltpu.make_async_copy(ys_ref.at[pl.ds(0, tm), :], bufs[slot], sem.at[slot])

    @pl.when(i == 0)
    def _():
        request(0, 0, 0, tm, False)

    def step(cur):
        arrived(cur).wait()
        request(jnp.minimum(i + 1, n_tiles - 1), 1 - cur, 0, tm, True)
        x32 = bufs[cur][...]
        x_ref[...] = x32
        _inproj_body(x32.astype(BF16), wa_ref, wb_ref, wg_ref, wgt_ref, wkt_ref, brow_ref, bcol_ref, q_ref, k_ref, v_ref,
                     o_ref, u_ref, ga_ref, gb_ref, gcol_ref, grow_ref, kt_ref, n_prompt_tiles)

        @pl.when(i == n_tiles - 1)
        def _():
            arrived(1 - cur).wait()

    for parity in (0, 1):
        pl.when(i % 2 == parity)(functools.partial(step, parity))


def _inproj_body(x, wa_ref, wb_ref, wg_ref, wgt_ref, wkt_ref, brow_ref, bcol_ref,
                 q_ref, k_ref, v_ref, o_ref, u_ref, ga_ref, gb_ref, gcol_ref, grow_ref, kt_ref, n_prompt_tiles):
    @pl.when(pl.program_id(0) >= n_prompt_tiles)
    def _():
        kt = lax.dot_general(wkt_ref[...], x, (((1,), (1,)), ((), ())), preferred_element_type=F32)
        kt_ref[...] = (kt * K_SCALE).astype(BF16)

    def seg(i):
        ref, lo, hi = _SEGS[i]
        return jnp.dot(x, (wa_ref, wb_ref)[ref][:, lo:hi], preferred_element_type=F32)

    q_ref[...] = seg(0).astype(BF16)
    k_ref[...] = (seg(1) * K_SCALE).astype(BF16)
    v_ref[...] = seg(2).astype(BF16)
    o_ref[...] = seg(3)
    u_ref[...] = seg(4)
    ga_ref[...] = seg(5)
    gb_ref[...] = seg(6)
    g = jnp.dot(x, wg_ref[...], preferred_element_type=F32) + brow_ref[...]
    lane = lax.broadcasted_iota(I32, g.shape, 1)
    gcol_ref[...] = jnp.where(lane < N_HEADS, g, jax.nn.log_sigmoid(g))
    gt = lax.dot_general(wgt_ref[...], x, (((1,), (1,)), ((), ())), preferred_element_type=F32) + bcol_ref[...]
    sub = lax.broadcasted_iota(I32, gt.shape, 0)
    grow_ref[...] = jnp.where(sub < N_HEADS, gt, jax.nn.log_sigmoid(gt))


def _inproj(x, w, layer, n_prompt, dest=None):
    n = sum(a.shape[0] for a in x) if dest is None else dest.shape[0]
    tm = _row_tile(math.gcd(n_prompt, n - n_prompt))
    ntp = n_prompt // tm
    row = lambda w: pl.BlockSpec((tm, w), lambda i, *_: (i, 0))
    consts = (w["w_qkvo"], w["w_rest"], w["w_gate"], w["w_gate_t"], w["w_key_t"], w["b_row"], w["b_col"])
    const_specs = [_layer_spec(c, layer) for c in consts]
    out_shape = [
        jax.ShapeDtypeStruct((n, D_MLSTM), BF16), jax.ShapeDtypeStruct((n, D_MLSTM), BF16),
        jax.ShapeDtypeStruct((n, D_MLSTM), BF16), jax.ShapeDtypeStruct((n, D_MLSTM), F32),
        jax.ShapeDtypeStruct((n, D_POOL), F32), jax.ShapeDtypeStruct((n, D_MODEL), F32),
        jax.ShapeDtypeStruct((n, D_MODEL), F32), jax.ShapeDtypeStruct((n, LANES), F32),
        jax.ShapeDtypeStruct((GATE_ROWS, n), F32), jax.ShapeDtypeStruct((D_MLSTM, n - n_prompt), BF16)]
    out_specs = [row(D_MLSTM), row(D_MLSTM), row(D_MLSTM), row(D_MLSTM), row(D_POOL), row(D_MODEL),
                 row(D_MODEL), row(LANES), pl.BlockSpec((GATE_ROWS, tm), lambda i, *_: (0, i)),
                 pl.BlockSpec((D_MLSTM, tm), lambda i, *_: (0, jnp.maximum(i - ntp, 0))), row(D_MODEL)]
    out_shape.append(jax.ShapeDtypeStruct((n, D_MODEL), F32))
    if dest is None:
        return pl.pallas_call(
            functools.partial(_inproj_kernel, n_prompt_tiles=ntp),
            grid=(n // tm,),
            in_specs=[pl.BlockSpec((tm, D_MODEL), lambda i: (jnp.minimum(i, ntp - 1), 0)),
                      pl.BlockSpec((tm, D_MODEL), lambda i: (jnp.maximum(i - ntp, 0), 0))] + const_specs,
            out_specs=out_specs,
            out_shape=out_shape,
            compiler_params=_params(("arbitrary",)),
            name="inproj",
        )(*x, *consts)
    grid_spec = pltpu.PrefetchScalarGridSpec(
        num_scalar_prefetch=1,
        grid=(n // tm,),
        in_specs=[pl.BlockSpec(memory_space=pl.ANY)] + const_specs,
        out_specs=out_specs,
        scratch_shapes=[pltpu.VMEM((tm, D_MODEL), F32), pltpu.VMEM((tm, D_MODEL), F32),
                        pltpu.SemaphoreType.DMA((2,))])
    return pl.pallas_call(
        functools.partial(_inproj_gather_kernel, n_prompt_tiles=ntp, n_tiles=n // tm, tm=tm),
        grid_spec=grid_spec,
        out_shape=out_shape,
        compiler_params=_params(("arbitrary",)),
        name="inproj_gather",
    )(dest, x, *consts)


def _head_out(hval, o, gain):
    mu = jnp.mean(hval, axis=1, keepdims=True)
    xc = hval - mu
    var = jnp.mean(xc * xc, axis=1, keepdims=True)
    return (jax.nn.sigmoid(o) * (xc * lax.rsqrt(var + LN_EPS) * gain)).astype(BF16)


def _split3(x):
    x1 = x.astype(BF16)
    r1 = x - x1.astype(F32)
    x2 = r1.astype(BF16)
    x3 = (r1 - x2.astype(F32)).astype(BF16)
    return x1, x2, x3


def _mlstm_prompt_kernel(*refs, rows, strip, chained):
    if chained:
        q_ref, k_ref, v_ref, gcol_ref, grow_ref, o_ref, gain_ref, _, hg_ref, c_ref, n_ref, m_ref = refs
    else:
        q_ref, k_ref, v_ref, gcol_ref, grow_ref, o_ref, gain_ref, hg_ref, c_ref, n_ref, m_ref = refs

    @pl.when(pl.program_id(1) == 0)
    def _():
        c_ref[...] = jnp.zeros_like(c_ref)
        n_ref[...] = jnp.zeros_like(n_ref)
        m_ref[...] = jnp.zeros_like(m_ref)

    t_idx = lax.broadcasted_iota(I32, (rows, 1), 0)
    s_idx = lax.broadcasted_iota(I32, (1, rows), 1)
    mask = s_idx <= t_idx
    lower = jnp.where(mask, 1.0, 0.0).astype(BF16)
    upper = jnp.where(t_idx <= s_idx, 1.0, 0.0).astype(BF16)
    gcol = gcol_ref[...]
    grow = grow_ref[...]
    bcol = sum(jnp.dot(lower, p, preferred_element_type=F32) for p in _split3(gcol))
    brow = sum(jnp.dot(p, upper, preferred_element_type=F32) for p in _split3(grow))
    ones = jnp.ones((rows, LANES), BF16)
    neg_inf = jnp.float32(-jnp.inf)
    heads = range(N_HEADS)
    sls = [slice(h * HEAD_DIM, (h + 1) * HEAD_DIM) for h in heads]
    qs = [q_ref[:, sl] for sl in sls]
    ks = [k_ref[:, sl] for sl in sls]
    vos = [jnp.concatenate([v_ref[:, sl], ones], axis=1) for sl in sls]
    b_cs = [bcol[:, N_HEADS + h:N_HEADS + h + 1] for h in heads]
    b_rs = [brow[N_HEADS + h:N_HEADS + h + 1, :] for h in heads]
    g_rs = [grow[h:h + 1, :] - b_rs[h] for h in heads]
    m_prevs = [m_ref[0, h:h + 1, 0:1] for h in heads]
    c_prevs = [c_ref[0, h] for h in heads]
    n_prevs = [n_ref[0, h] for h in heads]
    qks = [lax.dot_general(qs[h], ks[h], (((1,), (1,)), ((), ())), preferred_element_type=F32) for h in heads]
    inters = [jnp.dot(qs[h], jnp.concatenate([c_prevs[h].astype(BF16), n_prevs[h].astype(BF16)], axis=1),
                      preferred_element_type=F32) for h in heads]
    gms = [jnp.where(mask, g_rs[h], neg_inf) for h in heads]
    tops = [jnp.maximum(m_prevs[h], jnp.max(gms[h], axis=1, keepdims=True)) for h in heads]
    ss = [(qks[h] * jnp.exp(gms[h] - tops[h])).astype(BF16) for h in heads]
    intras = [jnp.dot(ss[h], vos[h], preferred_element_type=F32) for h in heads]
    b_lasts = [b_rs[h][:, rows - 1:rows] for h in heads]
    d_lasts = [b_lasts[h] - b_cs[h] + gcol[:, h:h + 1] for h in heads]
    m_news = [jnp.maximum(b_lasts[h] + m_prevs[h], jnp.max(d_lasts[h], axis=0, keepdims=True)) for h in heads]
    kws = [(ks[h].astype(F32) * jnp.exp(d_lasts[h] - m_news[h])).astype(BF16) for h in heads]
    upds = [lax.dot_general(kws[h], vos[h], (((0,), (0,)), ((), ())), preferred_element_type=F32) for h in heads]
    s_inters = [jnp.exp(m_prevs[h] - tops[h]) for h in heads]
    nums = [s_inters[h] * inters[h][:, 0:HEAD_DIM] + intras[h][:, 0:HEAD_DIM] for h in heads]
    dens = [s_inters[h] * inters[h][:, HEAD_DIM:HEAD_DIM + 1] + intras[h][:, HEAD_DIM:HEAD_DIM + 1] for h in heads]
    hvals = [nums[h] / jnp.maximum(jnp.abs(dens[h]), jnp.exp(-(b_cs[h] + tops[h]))) for h in heads]
    mus = [jnp.mean(hvals[h], axis=1, keepdims=True) for h in heads]
    xcs = [hvals[h] - mus[h] for h in heads]
    vars_ = [jnp.mean(xcs[h] * xcs[h], axis=1, keepdims=True) for h in heads]
    for h in heads:
        hn = xcs[h] * lax.rsqrt(vars_[h] + LN_EPS) * gain_ref[:, sls[h]]
        hg_ref[:, sls[h]] = (jax.nn.sigmoid(o_ref[:, sls[h]]) * hn).astype(BF16)
    for h in heads:
        s_last = jnp.exp(b_lasts[h] + m_prevs[h] - m_news[h])
        c_ref[0, h] = s_last * c_prevs[h] + upds[h][:, 0:HEAD_DIM]
        n_ref[0, h] = s_last * n_prevs[h] + upds[h][:, HEAD_DIM:]
        m_ref[0, h:h + 1, :] = jnp.broadcast_to(m_news[h], (1, LANES))


def _mlstm_sample_kernel(*refs, seq_len, window, block, chained):
    if chained:
        (q_ref, k_ref, kt_ref, v_ref, gcol_ref, grow_ref, mtok_ref, o_ref, gain_ref, c0_ref, n0_ref, _,
         hg_ref, c1_ref, n1_ref, m1_ref) = refs
    else:
        (q_ref, k_ref, kt_ref, v_ref, gcol_ref, grow_ref, mtok_ref, o_ref, gain_ref, c0_ref, n0_ref,
         hg_ref, c1_ref, n1_ref, m1_ref) = refs
    n_seq = window // seq_len
    shift = int(math.log2(seq_len))
    w0 = (pl.program_id(0) % (block // window)) * window
    r_idx = lax.broadcasted_iota(I32, (window, 1), 0)
    t_idx = w0 + r_idx
    s_idx = lax.broadcasted_iota(I32, (1, block), 1)
    same = jnp.right_shift(t_idx, shift) == jnp.right_shift(s_idx, shift)
    mask = same & (s_idx <= t_idx)
    mask_t = same & (t_idx <= s_idx)
    r_seq = jnp.right_shift(r_idx, shift)
    l_seq = jnp.right_shift(s_idx - w0, shift)
    gcol = gcol_ref[...]
    grow = grow_ref[...]
    mtok = mtok_ref[...]
    neg_inf = jnp.float32(-jnp.inf)
    for h in range(N_HEADS):
        sl = slice(h * HEAD_DIM, (h + 1) * HEAD_DIM)
        q = q_ref[:, sl]
        k = k_ref[:, sl]
        kt = kt_ref[sl, :]
        v = v_ref[:, sl]
        li_r = grow[h:h + 1, :]
        lf_r = grow[N_HEADS + h:N_HEADS + h + 1, :]
        li_c = gcol[:, h:h + 1]
        lf_c = gcol[:, N_HEADS + h:N_HEADS + h + 1]
        m_c = mtok[:, h:h + 1]
        b_c = jnp.sum(jnp.where(mask, lf_r, 0.0), axis=1, keepdims=True)
        b_r = jnp.sum(jnp.where(mask_t, lf_c, 0.0), axis=0, keepdims=True)
        dmat = jnp.where(mask, b_c - b_r + li_r, neg_inf)
        inter = b_c + m_c
        m_t = jnp.maximum(inter, jnp.max(dmat, axis=1, keepdims=True))
        s_inter = jnp.exp(inter - m_t)
        s = jnp.dot(q, kt, preferred_element_type=F32) * jnp.exp(dmat - m_t)
        intra = jnp.dot(s.astype(BF16), v, preferred_element_type=F32)
        qf = q.astype(F32)
        qc = qn = None
        for j in range(n_seq):
            qc_j = jnp.dot(q, c0_ref[j, h].astype(BF16), preferred_element_type=F32)
            qn_j = jnp.sum(qf * n0_ref[j, h:h + 1, :], axis=1, keepdims=True)
            qc = qc_j if j == 0 else jnp.where(r_seq == j, qc_j, qc)
            qn = qn_j if j == 0 else jnp.where(r_seq == j, qn_j, qn)
        num = s_inter * qc + intra
        den = s_inter * qn + jnp.sum(s, axis=1, keepdims=True)
        hval = num / jnp.maximum(jnp.abs(den), jnp.exp(-m_t))
        hg_ref[:, sl] = _head_out(hval, o_ref[:, sl], gain_ref[:, sl])
        kf = k.astype(F32)
        ktf = kt.astype(F32)
        for j in range(n_seq):
            lsel = l_seq == j
            m_j = mtok[j * seq_len:j * seq_len + 1, h:h + 1]
            b_last = jnp.sum(jnp.where(lsel, lf_r, 0.0), axis=1, keepdims=True)
            d_r = jnp.where(lsel, b_last - b_r + li_r, neg_inf)
            d_c = jnp.where(r_seq == j, b_last - b_c + li_c, neg_inf)
            m_new = jnp.maximum(b_last + m_j, jnp.max(d_r, axis=1, keepdims=True))
            s_last = jnp.exp(b_last + m_j - m_new)
            kwt = (ktf * jnp.exp(d_r - m_new)).astype(BF16)
            c1_ref[j, h] = s_last * c0_ref[j, h] + jnp.dot(kwt, v, preferred_element_type=F32)
            n1_ref[j, h:h + 1, :] = (s_last * n0_ref[j, h:h + 1, :]
                                     + jnp.sum(kf * jnp.exp(d_c - m_new), axis=0, keepdims=True))
            m1_ref[j, h:h + 1, :] = jnp.broadcast_to(m_new, (1, LANES))


def _chain(c_all):
    if c_all is None:
        return [], []
    return [c_all], [pl.BlockSpec(memory_space=pl.ANY)]


def _mlstm_prompt(q, k, v, gcol, grow, o, gain, c_all, layer, n_seq, seq_len):
    n = n_seq * seq_len
    chunk = math.gcd(seq_len, MLSTM_PROMPT_CHUNK)
    nc = seq_len // chunk
    row = lambda w: pl.BlockSpec((chunk, w), lambda b, c: (b * nc + c, 0))
    st = lambda *tail: pl.BlockSpec((1,) + tail, lambda b, c: (b,) + (0,) * len(tail))
    extra, extra_specs = _chain(c_all)
    n_in = 7
    return pl.pallas_call(
        functools.partial(_mlstm_prompt_kernel, rows=chunk, strip=min(chunk, MLSTM_STRIP), chained=bool(extra)),
        grid=(n_seq, nc),
        in_specs=[row(D_MLSTM), row(D_MLSTM), row(D_MLSTM), row(LANES),
                  pl.BlockSpec((GATE_ROWS, chunk), lambda b, c: (0, b * nc + c)),
                  row(D_MLSTM), _layer_spec(gain, layer)] + extra_specs,
        out_specs=(row(D_MLSTM),
                   pl.BlockSpec((None, 1, N_HEADS, HEAD_DIM, HEAD_DIM), lambda b, c: (layer, b, 0, 0, 0)),
                   st(N_HEADS, HEAD_DIM, LANES), st(N_HEADS, LANES)),
        out_shape=(jax.ShapeDtypeStruct((n, D_MLSTM), BF16),
                   jax.ShapeDtypeStruct((DEPTH, n_seq, N_HEADS, HEAD_DIM, HEAD_DIM), F32),
                   jax.ShapeDtypeStruct((n_seq, N_HEADS, HEAD_DIM, LANES), F32),
                   jax.ShapeDtypeStruct((n_seq, N_HEADS, LANES), F32)),
        input_output_aliases={n_in: 1} if extra else {},
        compiler_params=_params(("parallel", "arbitrary")),
        name="mlstm_prompt",
    )(q, k, v, gcol, grow, o, gain, *extra)


def _mlstm_sample(q, k, kt, v, gcol, grow, mtok, o, gain, c0, n0, c_all, layer, first_row, n_seq, seq_len):
    n = n_seq * seq_len
    window, block = SAMPLE_WINDOW, SAMPLE_BLOCK_ROWS
    assert n % block == 0 and first_row % block == 0 and window % seq_len == 0
    per_win = window // seq_len
    sub = block // window
    row = lambda w: pl.BlockSpec((window, w), lambda i: (first_row // window + i, 0))
    extra, extra_specs = _chain(c_all)
    n_in = 11
    return pl.pallas_call(
        functools.partial(_mlstm_sample_kernel, seq_len=seq_len, window=window, block=block, chained=bool(extra)),
        grid=(n // window,),
        in_specs=[row(D_MLSTM), row(D_MLSTM),
                  pl.BlockSpec((D_MLSTM, block), lambda i: (0, i // sub)),
                  pl.BlockSpec((block, D_MLSTM), lambda i: (first_row // block + i // sub, 0)),
                  row(LANES),
                  pl.BlockSpec((GATE_ROWS, block), lambda i: (0, first_row // block + i // sub)),
                  pl.BlockSpec((window, LANES), lambda i: (i, 0)),
                  row(D_MLSTM), _layer_spec(gain, layer),
                  pl.BlockSpec((None, per_win, N_HEADS, HEAD_DIM, HEAD_DIM), lambda i: (layer, i, 0, 0, 0)),
                  pl.BlockSpec((None, per_win, N_HEADS, HEAD_DIM), lambda i: (layer, i, 0, 0))] + extra_specs,
        out_specs=(pl.BlockSpec((window, D_MLSTM), lambda i: (i, 0)),
                   pl.BlockSpec((None, per_win, N_HEADS, HEAD_DIM, HEAD_DIM), lambda i: (layer, i, 0, 0, 0)),
                   pl.BlockSpec((per_win, N_HEADS, HEAD_DIM), lambda i: (i, 0, 0)),
                   pl.BlockSpec((per_win, N_HEADS, LANES), lambda i: (i, 0, 0))),
        out_shape=(jax.ShapeDtypeStruct((n, D_MLSTM), BF16),
                   jax.ShapeDtypeStruct((DEPTH, n_seq, N_HEADS, HEAD_DIM, HEAD_DIM), F32),
                   jax.ShapeDtypeStruct((n_seq, N_HEADS, HEAD_DIM), F32),
                   jax.ShapeDtypeStruct((n_seq, N_HEADS, LANES), F32)),
        input_output_aliases={n_in: 1} if extra else {},
        compiler_params=_params(("parallel",)),
        name="mlstm_sample",
    )(q, k, kt, v, gcol, grow, mtok, o, gain, c0, n0, *extra)


def _pool_prompt_kernel(u_ref, prev_ref, out_ref, tail_ref, *, tm, tiles_per_seq):
    tile = pl.program_id(0) % tiles_per_seq
    head = 16
    tail_ref[...] = u_ref[tm - head:tm, :]
    pos = tile * tm + lax.broadcasted_iota(I32, (tm, 1), 0)
    for g, w in enumerate(POOL_WINDOWS):
        sl = slice(g * POOL_GROUP_DIM, (g + 1) * POOL_GROUP_DIM)
        acc = jnp.concatenate([jnp.where(tile == 0, 0.0, prev_ref[:, sl]), u_ref[:, sl]], axis=0)
        span = 1
        while span < w:
            acc = acc + pltpu.roll(acc, span, 0)
            span *= 2
        cnt = jnp.minimum(pos + 1, w).astype(F32)
        out_ref[:, sl] = (acc[head:] / cnt - u_ref[:, sl]).astype(BF16)


def _pool_prompt(u, n_seq, seq_len):
    n = n_seq * seq_len
    tm = _row_tile(seq_len)
    head = 16
    return pl.pallas_call(
        functools.partial(_pool_prompt_kernel, tm=tm, tiles_per_seq=seq_len // tm),
        grid=(n // tm,),
        in_specs=[pl.BlockSpec((tm, D_POOL), lambda i: (i, 0)),
                  pl.BlockSpec((head, D_POOL), lambda i: (jnp.maximum(i * (tm // head) - 1, 0), 0))],
        out_specs=(pl.BlockSpec((tm, D_POOL), lambda i: (i, 0)),
                   pl.BlockSpec((head, D_POOL), lambda i: (i // (seq_len // tm), 0))),
        out_shape=(jax.ShapeDtypeStruct((n, D_POOL), BF16), jax.ShapeDtypeStruct((n_seq * head, D_POOL), F32)),
        compiler_params=_params(("arbitrary",)),
        name="pool_prompt",
    )(u, u)


def _pool_sample_kernel(u_ref, buf_ref, out_ref, nbuf_ref, ext_ref, *, seq_len, start):
    ext_ref[:, 0:POOL_BUF, :] = buf_ref[...]
    ext_ref[:, POOL_BUF:POOL_BUF + seq_len, :] = u_ref[...]
    pos = start + lax.broadcasted_iota(I32, (1, seq_len, 1), 1)
    for g, w in enumerate(POOL_WINDOWS):
        sl = slice(g * POOL_GROUP_DIM, (g + 1) * POOL_GROUP_DIM)
        acc = ext_ref[:, POOL_BUF:POOL_BUF + seq_len, sl]
        for d in range(1, w):
            acc = acc + ext_ref[:, POOL_BUF - d:POOL_BUF - d + seq_len, sl]
        cnt = jnp.minimum(pos + 1, w).astype(F32)
        out_ref[:, :, sl] = (acc / cnt - u_ref[:, :, sl]).astype(BF16)
    nbuf_ref[...] = ext_ref[:, seq_len:seq_len + POOL_BUF, :]


def _pool_sample(u3, buf, layer, start):
    n_seq, seq_len, _ = u3.shape
    bs = _row_tile(n_seq, 32)
    spec = lambda r: pl.BlockSpec((bs, r, D_POOL), lambda i: (i, 0, 0))
    return pl.pallas_call(
        functools.partial(_pool_sample_kernel, seq_len=seq_len, start=start),
        grid=(n_seq // bs,),
        in_specs=[spec(seq_len), pl.BlockSpec((None, bs, POOL_BUF, D_POOL), lambda i: (layer, i, 0, 0))],
        out_specs=(spec(seq_len), spec(POOL_BUF)),
        out_shape=(jax.ShapeDtypeStruct((n_seq, seq_len, D_POOL), BF16),
                   jax.ShapeDtypeStruct((n_seq, POOL_BUF, D_POOL), F32)),
        scratch_shapes=[pltpu.VMEM((bs, POOL_BUF + seq_len + 5, D_POOL), F32)],
        compiler_params=_params(("parallel",)),
        name="pool_sample",
    )(u3, buf)


def _layer_norm(y, g, b):
    mu = jnp.mean(y, axis=1, keepdims=True)
    yc = y - mu
    var = jnp.mean(yc * yc, axis=1, keepdims=True)
    return yc * lax.rsqrt(var + LN_EPS) * g + b


def _route(logits_t):
    tokens = logits_t.shape[1]
    grp = lax.broadcasted_iota(I32, (8, tokens), 0)
    live = grp < N_EXPERT_GROUPS
    neg_inf = jnp.float32(-jnp.inf)
    lm = [jnp.where(live, logits_t[8 * m:8 * m + 8, :], neg_inf) for m in range(EXPERTS_PER_GROUP)]
    mx = jnp.max(jnp.maximum(jnp.maximum(lm[0], lm[1]), jnp.maximum(lm[2], lm[3])), axis=0, keepdims=True)
    ex = [jnp.exp(l - mx) for l in lm]
    tot = jnp.sum(ex[0] + ex[1] + ex[2] + ex[3], axis=0, keepdims=True)
    p = [e / tot for e in ex]
    top1 = jnp.maximum(jnp.maximum(p[0], p[1]), jnp.maximum(p[2], p[3]))
    i1 = jnp.where(p[0] == top1, 0, jnp.where(p[1] == top1, 1, jnp.where(p[2] == top1, 2, 3)))
    r = [jnp.where(i1 == m, -1.0, p[m]) for m in range(EXPERTS_PER_GROUP)]
    top2 = jnp.maximum(jnp.maximum(r[0], r[1]), jnp.maximum(r[2], r[3]))
    i2 = jnp.where(r[0] == top2, 0, jnp.where(r[1] == top2, 1, jnp.where(r[2] == top2, 2, 3)))
    gscore = jnp.where(live, top1 + top2, neg_inf)
    gmax = jnp.max(gscore, axis=0, keepdims=True)
    gsel = jnp.min(jnp.where(gscore == gmax, grp, 8), axis=0, keepdims=True)
    chosen = grp == gsel
    tsum = top1 + top2
    w1 = top1 / tsum
    w2 = top2 / tsum
    first_is_lo = i1 < i2
    lo = jnp.minimum(i1, i2)
    hi = jnp.maximum(i1, i2)
    pair = jnp.where(lo == 0, hi - 1, jnp.where(lo == 1, hi + 1, 5))
    pick = lambda a: jnp.sum(jnp.where(chosen, a, jnp.zeros_like(a)), axis=0, keepdims=True)
    cls = pick(grp * len(PAIRS) + pair)
    w_lo = pick(jnp.where(first_is_lo, w1, w2))
    w_hi = pick(jnp.where(first_is_lo, w2, w1))
    return cls, w_lo, w_hi


def _mix_kernel(hgp_ref, hgs_ref, plp_ref, pls_ref, ga_ref, gb_ref, x_ref, wpool_ref, pscale_ref, wa_ref, wb_ref,
                wout_ref, g1_ref, b1_ref, wr_ref, br_ref,
                dest_ref, cnt_ref, xs_ref,
                rows0_ref, rows1_ref, dvm_ref, dsm0_ref, dsm1_ref, carry_ref, row_sem, idx_sem,
                *, tm, n_prompt_tiles, n_tiles, capacity):
    i = pl.program_id(0)
    rows = (rows0_ref, rows1_ref)
    dsm = (dsm0_ref, dsm1_ref)
    spare = N_CLASSES * capacity
    sent = lambda s: pltpu.make_async_copy(rows[s], xs_ref.at[pl.ds(0, tm), :], row_sem.at[s])

    def send(s, unrolled):
        def one(r, carry=0):
            pltpu.make_async_copy(rows[s].at[pl.ds(r, 1), :], xs_ref.at[pl.ds(dsm[s][r], 1), :],
                                  row_sem.at[s]).start()
            return carry
        if unrolled:
            for r in range(tm):
                one(r)
        else:
            lax.fori_loop(0, tm, one, 0, unroll=8)

    @pl.when(i == 0)
    def _():
        carry_ref[...] = jnp.zeros_like(carry_ref)
        rows0_ref[...] = jnp.zeros_like(rows0_ref)
        rows1_ref[...] = jnp.zeros_like(rows1_ref)

        def spare_rows(r, carry):
            dsm0_ref[r] = spare + r
            return carry

        lax.fori_loop(0, tm, spare_rows, 0)
        send(0, False)
        dvm_ref[...] = spare + tm + lax.broadcasted_iota(I32, (8, tm), 1)
        pltpu.make_async_copy(dvm_ref.at[0], dsm1_ref, idx_sem).start()

    for parity in (0, 1):
        pl.when(i % 2 == parity)(functools.partial(
            _mix_step, parity, i, hgp_ref, hgs_ref, plp_ref, pls_ref, ga_ref, gb_ref, x_ref, wpool_ref, pscale_ref,
            wa_ref, wb_ref, wout_ref, g1_ref, b1_ref, wr_ref, br_ref, dest_ref, cnt_ref, rows, dvm_ref, dsm,
            carry_ref, idx_sem, send, sent, tm, n_prompt_tiles, n_tiles, capacity))


def _mix_step(cur, i, hgp_ref, hgs_ref, plp_ref, pls_ref, ga_ref, gb_ref, x_ref, wpool_ref, pscale_ref, wa_ref,
              wb_ref, wout_ref, g1_ref, b1_ref, wr_ref, br_ref, dest_ref, cnt_ref, rows, dvm_ref, dsm, carry_ref,
              idx_sem, send, sent, tm, n_prompt_tiles, n_tiles, capacity):
    slots_arrived = lambda s: pltpu.make_async_copy(dvm_ref.at[0], dsm[s], idx_sem)
    slots_arrived(1 - cur).wait()
    send(1 - cur, True)
    is_prompt = i < n_prompt_tiles
    hg = jnp.where(is_prompt, hgp_ref[...], hgs_ref[...])
    pooled = jnp.where(is_prompt, plp_ref[...], pls_ref[...])
    ya = jnp.dot(hg, wa_ref[...], preferred_element_type=F32)
    parts = []
    for g in range(len(POOL_WINDOWS)):
        sl = slice(g * POOL_GROUP_DIM, (g + 1) * POOL_GROUP_DIM)
        parts.append(jnp.dot(pooled[:, sl], wpool_ref[g], preferred_element_type=F32))
    pl_lin = jnp.concatenate(parts, axis=1) * pscale_ref[...]
    yb = jnp.dot(pl_lin.astype(BF16), wb_ref[...], preferred_element_type=F32)
    mix = jax.nn.sigmoid(ga_ref[...]) * ya + jax.nn.sigmoid(gb_ref[...]) * yb
    res = jnp.dot(mix.astype(BF16), wout_ref[...], preferred_element_type=F32)
    x1 = _layer_norm(ALPHA * x_ref[...] + res, g1_ref[...], b1_ref[...])

    nt = lambda a, b: lax.dot_general(a, b, (((1,), (1,)), ((), ())), preferred_element_type=F32)
    wr = wr_ref[...]
    wr_hi = wr.astype(BF16)
    wr_lo = (wr - wr_hi.astype(F32)).astype(BF16)
    x1_hi = x1.astype(BF16)
    x1_lo = (x1 - x1_hi.astype(F32)).astype(BF16)
    logits_t = nt(wr_hi, x1_hi) + (nt(wr_hi, x1_lo) + nt(wr_lo, x1_hi)) + br_ref[...]
    cls, w_lo, w_hi = _route(logits_t)

    onehot = lax.broadcasted_iota(I32, (CLASS_ROWS, tm), 0) == cls
    earlier = lax.broadcasted_iota(I32, (tm, tm), 0) < lax.broadcasted_iota(I32, (tm, tm), 1)
    before = jnp.dot(jnp.where(onehot, 1.0, 0.0).astype(BF16), jnp.where(earlier, 1.0, 0.0).astype(BF16),
                     preferred_element_type=F32)
    seen = carry_ref[:, 0:1]
    rank = jnp.sum(jnp.where(onehot, before + seen, 0.0), axis=0, keepdims=True)
    carry_ref[...] = carry_ref[...] + jnp.sum(jnp.where(onehot, 1.0, 0.0), axis=1, keepdims=True)
    cnt_ref[...] = carry_ref[...]
    dest = cls * capacity + rank.astype(I32)
    dest_ref[...] = jnp.broadcast_to(dest, (8, tm))

    sent(cur).wait()
    wrows = jnp.concatenate([w_lo, w_hi, jnp.zeros((LANES - 2, tm), F32)], axis=0)
    rows[cur][:, 0:D_MODEL] = x1
    rows[cur][:, D_MODEL:ROW_EXT] = wrows.T
    dvm_ref[...] = jnp.broadcast_to(dest, (8, tm))
    slots_arrived(cur).start()

    @pl.when(i == n_tiles - 1)
    def _():
        slots_arrived(cur).wait()
        send(cur, False)
        sent(cur).wait()
        sent(1 - cur).wait()


def _mix(hg_p, hg_s, pooled_p, pooled_s, ga, gb, x, w, layer, capacity):
    n = x.shape[0]
    n_p = hg_p.shape[0]
    tm = _row_tile(math.gcd(n_p, n - n_p))
    ntp = n_p // tm
    row = lambda w: pl.BlockSpec((tm, w), lambda i: (i, 0))
    prow = lambda w: pl.BlockSpec((tm, w), lambda i: (jnp.minimum(i, ntp - 1), 0))
    srow = lambda w: pl.BlockSpec((tm, w), lambda i: (jnp.maximum(i - ntp, 0), 0))
    per_layer = (w["w_pool"], w["pool_scale"], w["w_proj_a"], w["w_proj_b"], w["w_out"], w["ln1_g"], w["ln1_b"])
    shared = (w["w_router_t"], w["b_router_col"])
    consts = per_layer + shared
    return pl.pallas_call(
        functools.partial(_mix_kernel, tm=tm, n_prompt_tiles=ntp, n_tiles=n // tm, capacity=capacity),
        grid=(n // tm,),
        in_specs=[prow(D_MLSTM), srow(D_MLSTM), prow(D_POOL), srow(D_POOL), row(D_MODEL), row(D_MODEL), row(D_MODEL)]
                 + [_layer_spec(c, layer) for c in per_layer] + [_const_spec(c.shape) for c in shared],
        out_specs=(pl.BlockSpec((8, tm), lambda i: (0, i)),
                   pl.BlockSpec((CLASS_ROWS, LANES), lambda i: (0, 0)),
                   pl.BlockSpec(memory_space=pl.ANY)),
        out_shape=(jax.ShapeDtypeStruct((8, n), I32),
                   jax.ShapeDtypeStruct((CLASS_ROWS, LANES), F32),
                   jax.ShapeDtypeStruct((N_CLASSES * capacity + 2 * tm, ROW_EXT), F32)),
        scratch_shapes=[pltpu.VMEM((tm, ROW_EXT), F32), pltpu.VMEM((tm, ROW_EXT), F32), pltpu.VMEM((8, tm), I32),
                        pltpu.SMEM((tm,), I32), pltpu.SMEM((tm,), I32), pltpu.VMEM((CLASS_ROWS, LANES), F32),
                        pltpu.SemaphoreType.DMA((2,)), pltpu.SemaphoreType.DMA],
        compiler_params=_params(("arbitrary",)),
        name="mix",
    )(hg_p, hg_s, pooled_p, pooled_s, ga, gb, x, *consts)


def _moe_kernel(blk_ref, elo_ref, ehi_ref, nvalid_ref, ntiles_ref,
                xs_ref, wg_lo32, wu_lo32, wd_lo32, wg_hi32, wu_hi32, wd_hi32, g2_ref, b2_ref, ys_ref,
                wg_lo, wu_lo, wd_lo, wg_hi, wu_hi, wd_hi):
    i = pl.program_id(0)
    prev = jnp.maximum(i - 1, 0)

    @pl.when((i == 0) | (elo_ref[i] != elo_ref[prev]))
    def _():
        wg_lo[...] = wg_lo32[...].astype(BF16)
        wu_lo[...] = wu_lo32[...].astype(BF16)
        wd_lo[...] = wd_lo32[...].astype(BF16)

    @pl.when((i == 0) | (ehi_ref[i] != ehi_ref[prev]))
    def _():
        wg_hi[...] = wg_hi32[...].astype(BF16)
        wu_hi[...] = wu_hi32[...].astype(BF16)
        wd_hi[...] = wd_hi32[...].astype(BF16)

    @pl.when(i < ntiles_ref[0])
    def _():
        valid = lax.broadcasted_iota(I32, (MOE_TILE, 1), 0) < nvalid_ref[i]
        xe = xs_ref[...]
        x = jnp.where(valid, xe[:, 0:D_MODEL], 0.0)
        w_lo = jnp.where(valid, xe[:, D_MODEL:D_MODEL + 1], 0.0)
        w_hi = jnp.where(valid, xe[:, D_MODEL + 1:D_MODEL + 2], 0.0)
        xb = x.astype(BF16)

        def expert(wg, wu, wd, w):
            g = jnp.dot(xb, wg[0], preferred_element_type=F32)
            u = jnp.dot(xb, wu[0], preferred_element_type=F32)
            hid = (g * jax.nn.sigmoid(g)) * u * w
            return jnp.dot(hid.astype(BF16), wd[0], preferred_element_type=F32)

        y = expert(wg_lo, wu_lo, wd_lo, w_lo) + expert(wg_hi, wu_hi, wd_hi, w_hi)
        ys_ref[...] = _layer_norm(ALPHA * x + y, g2_ref[...], b2_ref[...])


def _moe(xs, tables, w, layer, max_tiles):
    blk, e_lo, e_hi, n_valid, n_tiles = tables
    up = lambda sel: pl.BlockSpec((None, 1, D_MODEL, D_EXPERT),
                                  lambda i, b, lo, hi, nv, nt: (layer, (lo, hi)[sel][i], 0, 0))
    down = lambda sel: pl.BlockSpec((None, 1, D_EXPERT, D_MODEL),
                                    lambda i, b, lo, hi, nv, nt: (layer, (lo, hi)[sel][i], 0, 0))
    grid_spec = pltpu.PrefetchScalarGridSpec(
        num_scalar_prefetch=5,
        grid=(max_tiles,),
        in_specs=[pl.BlockSpec((MOE_TILE, ROW_EXT), lambda i, b, *_: (b[i], 0)),
                  up(0), up(0), down(0), up(1), up(1), down(1),
                  _layer_spec(w["ln2_g"], layer), _layer_spec(w["ln2_b"], layer)],
        out_specs=pl.BlockSpec((MOE_TILE, D_MODEL), lambda i, b, *_: (b[i], 0)),
        scratch_shapes=[pltpu.VMEM(s, BF16) for s in 2 * [(1, D_MODEL, D_EXPERT), (1, D_MODEL, D_EXPERT),
                                                         (1, D_EXPERT, D_MODEL)]])
    return pl.pallas_call(
        _moe_kernel,
        grid_spec=grid_spec,
        out_shape=jax.ShapeDtypeStruct((xs.shape[0], D_MODEL), F32),
        compiler_params=_params(("arbitrary",)),
        name="moe",
    )(blk, e_lo, e_hi, n_valid, n_tiles, xs, w["w_e_gate"], w["w_e_up"], w["w_e_down"],
      w["w_e_gate"], w["w_e_up"], w["w_e_down"], w["ln2_g"], w["ln2_b"])


def _tile_tables(counts, capacity, max_tiles):
    cnt = counts[:N_CLASSES, 0].astype(I32)
    tiles = (cnt + MOE_TILE - 1) // MOE_TILE
    ends = jnp.cumsum(tiles)
    starts = ends - tiles
    n_tiles = ends[-1]
    t = jnp.minimum(jnp.arange(max_tiles, dtype=I32), jnp.maximum(n_tiles - 1, 0))
    cls = jnp.sum((ends[None, :] <= t[:, None]).astype(I32), axis=1)
    onehot = (jnp.arange(N_CLASSES, dtype=I32)[None, :] == cls[:, None]).astype(I32)
    within = t - jnp.sum(onehot * starts[None, :], axis=1)
    n_valid = jnp.clip(jnp.sum(onehot * cnt[None, :], axis=1) - within * MOE_TILE, 0, MOE_TILE)
    blk = cls * (capacity // MOE_TILE) + within
    grp = cls // len(PAIRS)
    pair = cls % len(PAIRS)
    lo = jnp.where(pair < 3, 0, jnp.where(pair < 5, 1, 2))
    hi = jnp.where(pair < 3, pair + 1, jnp.where(pair < 5, pair - 1, 3))
    return (blk, grp * EXPERTS_PER_GROUP + lo, grp * EXPERTS_PER_GROUP + hi, n_valid,
            n_tiles.reshape(1).astype(I32))


def _unpermute_kernel(dest_ref, ys_ref, prompt_ref, sample_ref, sem, *, tm, n_prompt_tiles):
    i = pl.program_id(0)

    def gather(out_ref):
        def fetch(r, carry):
            pltpu.make_async_copy(ys_ref.at[pl.ds(dest_ref[i * tm + r], 1), :], out_ref.at[pl.ds(r, 1), :],
                                  sem).start()
            return carry

        lax.fori_loop(0, tm, fetch, 0, unroll=8)
        pltpu.make_async_copy(ys_ref.at[pl.ds(0, tm), :], out_ref, sem).wait()

    pl.when(i < n_prompt_tiles)(functools.partial(gather, prompt_ref))
    pl.when(i >= n_prompt_tiles)(functools.partial(gather, sample_ref))


def _unpermute(ys, dest, n_prompt):
    n = dest.shape[0]
    tm = _row_tile(math.gcd(n_prompt, n - n_prompt))
    ntp = n_prompt // tm
    grid_spec = pltpu.PrefetchScalarGridSpec(
        num_scalar_prefetch=1,
        grid=(n // tm,),
        in_specs=[pl.BlockSpec(memory_space=pl.ANY)],
        out_specs=(pl.BlockSpec((tm, D_MODEL), lambda i, d: (jnp.minimum(i, ntp - 1), 0)),
                   pl.BlockSpec((tm, D_MODEL), lambda i, d: (jnp.maximum(i - ntp, 0), 0))),
        scratch_shapes=[pltpu.SemaphoreType.DMA])
    return pl.pallas_call(
        functools.partial(_unpermute_kernel, tm=tm, n_prompt_tiles=ntp),
        grid_spec=grid_spec,
        out_shape=(jax.ShapeDtypeStruct((n_prompt, D_MODEL), F32),
                   jax.ShapeDtypeStruct((n - n_prompt, D_MODEL), F32)),
        compiler_params=_params(("arbitrary",)),
        name="unpermute",
    )(dest, ys)


def _prepare_weights(w_in, b_gate, hn_gain, w_pool, pool_scale, w_proj_a, w_proj_b, w_out, ln1_g, ln1_b,
                     ln2_g, ln2_b, w_router, b_router, w_e_gate, w_e_up, w_e_down):
    w_qkvo = w_in[:, :, :_G0].astype(BF16)
    w_rest = w_in[:, :, _U0:].astype(BF16)
    w_gate = w_in[:, :, _G0:_U0]
    w_gate_p = jnp.pad(w_gate, ((0, 0), (0, 0), (0, LANES - 2 * N_HEADS))).astype(BF16)
    w_gate_t = jnp.pad(jnp.swapaxes(w_gate, 1, 2), ((0, 0), (0, GATE_ROWS - 2 * N_HEADS), (0, 0))).astype(BF16)
    w_key_t = jnp.swapaxes(w_in[:, :, D_MLSTM:2 * D_MLSTM], 1, 2).astype(BF16)
    b_row = jnp.pad(b_gate, ((0, 0), (0, LANES - 2 * N_HEADS)))[:, None, :]
    b_col = jnp.pad(b_gate, ((0, 0), (0, GATE_ROWS - 2 * N_HEADS)))[:, :, None]
    wr = w_router.T.reshape(N_EXPERT_GROUPS, EXPERTS_PER_GROUP, D_MODEL).swapaxes(0, 1)
    wr = jnp.pad(wr, ((0, 0), (0, 8 - N_EXPERT_GROUPS), (0, 0))).reshape(ROUTER_ROWS, D_MODEL)
    br = b_router.reshape(N_EXPERT_GROUPS, EXPERTS_PER_GROUP).T
    br = jnp.pad(br, ((0, 0), (0, 8 - N_EXPERT_GROUPS))).reshape(ROUTER_ROWS, 1)
    per_row = lambda a: a.reshape(DEPTH, 1, -1)
    return dict(
        w_qkvo=w_qkvo, w_rest=w_rest, w_gate=w_gate_p, w_gate_t=w_gate_t, w_key_t=w_key_t, b_row=b_row, b_col=b_col,
        gain=per_row(hn_gain), w_pool=w_pool.astype(BF16), pool_scale=per_row(pool_scale),
        w_proj_a=w_proj_a.astype(BF16), w_proj_b=w_proj_b.astype(BF16), w_out=w_out.astype(BF16),
        ln1_g=per_row(ln1_g), ln1_b=per_row(ln1_b), ln2_g=per_row(ln2_g), ln2_b=per_row(ln2_b),
        w_router_t=wr, b_router_col=br,
        w_e_gate=w_e_gate, w_e_up=w_e_up, w_e_down=w_e_down)


def kernel(x_prompt, x_sample, state_C, state_n, state_m, state_pool, w_in, b_gate, hn_gain, w_pool, pool_scale,
           w_proj_a, w_proj_b, w_out, ln1_g, ln1_b, ln2_g, ln2_b, w_router, b_router, w_e_gate, w_e_up, w_e_down):
    w = _prepare_weights(w_in, b_gate, hn_gain, w_pool, pool_scale, w_proj_a, w_proj_b, w_out, ln1_g, ln1_b,
                         ln2_g, ln2_b, w_router, b_router, w_e_gate, w_e_up, w_e_down)
    n_pseq, p_len, _ = x_prompt.shape
    n_sseq, s_len, _ = x_sample.shape
    n_p = n_pseq * p_len
    n_s = n_sseq * s_len
    n = n_p + n_s
    capacity = -(-n // MOE_TILE) * MOE_TILE
    max_tiles = n // MOE_TILE + N_CLASSES
    np_, mp, bp, ns, ms, bs = [], [], [], [], [], []
    c_p = c_s = None
    ys = (x_prompt.reshape(n_p, D_MODEL), x_sample.reshape(n_s, D_MODEL))
    dest = None
    for l in range(DEPTH):
        q, k, v, o, u, ga, gb, gcol, grow, kt, x = _inproj(ys, w, l, n_p, None if l == 0 else dest[0])
        hg_p, c_p, n1p, m1p = _mlstm_prompt(q, k, v, gcol, grow, o, w["gain"], c_p, l, n_pseq, p_len)
        mtok = jnp.pad(jnp.repeat(state_m[l], s_len, axis=0), ((0, 0), (0, LANES - N_HEADS)))
        hg_s, c_s, n1s, m1s = _mlstm_sample(q, k, kt, v, gcol, grow, mtok, o, w["gain"], state_C, state_n, c_s,
                                            l, n_p, n_sseq, s_len)
        pooled_p, tail = _pool_prompt(u, n_pseq, p_len)
        pooled_s, nbuf = _pool_sample(u[n_p:].reshape(n_sseq, s_len, D_POOL), state_pool, l, PAST_LEN)
        dest, counts, xs = _mix(hg_p, hg_s, pooled_p, pooled_s.reshape(n_s, D_POOL), ga, gb, x, w, l, capacity)
        ys = _moe(xs, _tile_tables(counts, capacity, max_tiles), w, l, max_tiles)
        np_.append(n1p[:, :, :, 0])
        mp.append(m1p[:, :, 0])
        bp.append(tail.reshape(n_pseq, -1, D_POOL)[:, -POOL_BUF:])
        ns.append(n1s)
        ms.append(m1s[:, :, 0])
        bs.append(nbuf)
    y_p, y_s = _unpermute(ys, dest[0], n_p)
    st = jnp.stack
    return (y_p.reshape(n_pseq, p_len, D_MODEL), y_s.reshape(n_sseq, s_len, D_MODEL),
            c_p, st(np_), st(mp), st(bp), c_s, st(ns), st(ms), st(bs))
```

```python
import functools
import math

import jax
import jax.numpy as jnp
from jax import lax
from jax.experimental import pallas as pl
from jax.experimental.pallas import tpu as pltpu

F32 = jnp.float32
BF16 = jnp.bfloat16
I32 = jnp.int32

D_MODEL = 1024
N_HEADS = 4
HEAD_DIM = 256
D_MLSTM = N_HEADS * HEAD_DIM
POOL_WINDOWS = (2, 4, 8, 16)
POOL_GROUP_DIM = 128
D_POOL = len(POOL_WINDOWS) * POOL_GROUP_DIM
POOL_BUF = 15
N_EXPERTS = 16
N_EXPERT_GROUPS = 4
EXPERTS_PER_GROUP = 4
D_EXPERT = 512
DEPTH = 4
PAST_LEN = 16384
ALPHA = (2 * DEPTH) ** 0.25
LN_EPS = 1e-5
K_SCALE = HEAD_DIM ** -0.5

LANES = 128
GATE_ROWS = 16
ROUTER_ROWS = 32
VMEM_LIMIT = 52 * 1024 * 1024
MLSTM_PROMPT_CHUNK = 256
MLSTM_STRIP = 256
SAMPLE_BLOCK_ROWS = 128
SAMPLE_WINDOW = 16

PAIRS = ((0, 1), (0, 2), (0, 3), (1, 2), (1, 3), (2, 3))
N_CLASSES = N_EXPERT_GROUPS * len(PAIRS)
CLASS_ROWS = 32
MOE_TILE = 256
ROW_EXT = D_MODEL + LANES

_G0 = 4 * D_MLSTM
_U0 = _G0 + 2 * N_HEADS
_SEGS = ((0, 0, 1024), (0, 1024, 2048), (0, 2048, 3072), (0, 3072, 4096), (1, 0, 512), (1, 512, 1536), (1, 1536, 2560))


def _params(sem, **kw):
    return pltpu.CompilerParams(dimension_semantics=sem, vmem_limit_bytes=VMEM_LIMIT, **kw)


def _const_spec(shape):
    nd = len(shape)
    return pl.BlockSpec(shape, lambda *_: (0,) * nd, pipeline_mode=pl.Buffered(1))


def _layer_spec(stacked, layer):
    tail = stacked.shape[1:]
    return pl.BlockSpec((None,) + tail, lambda *_: (layer,) + (0,) * len(tail), pipeline_mode=pl.Buffered(1))


def _row_tile(n, cap=512):
    t = cap
    while n % t:
        t //= 2
    return t


def _inproj_kernel(xp_ref, xs_ref, wa_ref, wb_ref, wg_ref, wgt_ref, wkt_ref, brow_ref, bcol_ref,
                   q_ref, k_ref, v_ref, o_ref, u_ref, ga_ref, gb_ref, gcol_ref, grow_ref, kt_ref, x_ref,
                   *, n_prompt_tiles):
    x32 = jnp.where(pl.program_id(0) < n_prompt_tiles, xp_ref[...], xs_ref[...])
    x_ref[...] = x32
    _inproj_body(x32.astype(BF16), wa_ref, wb_ref, wg_ref, wgt_ref, wkt_ref, brow_ref, bcol_ref,
                 q_ref, k_ref, v_ref, o_ref, u_ref, ga_ref, gb_ref, gcol_ref, grow_ref, kt_ref, n_prompt_tiles)


def _inproj_gather_kernel(dest_ref, ys_ref, wa_ref, wb_ref, wg_ref, wgt_ref, wkt_ref, brow_ref, bcol_ref,
                          q_ref, k_ref, v_ref, o_ref, u_ref, ga_ref, gb_ref, gcol_ref, grow_ref, kt_ref, x_ref,
                          xbuf0_ref, xbuf1_ref, sem, *, n_prompt_tiles, n_tiles, tm):
    i = pl.program_id(0)
    bufs = (xbuf0_ref, xbuf1_ref)

    def request(tile, slot, lo, hi, unrolled):
        def one(r, carry=0):
            pltpu.make_async_copy(ys_ref.at[pl.ds(dest_ref[tile * tm + r], 1), :],
                                  bufs[slot].at[pl.ds(r, 1), :], sem.at[slot]).start()
            return carry
        if unrolled:
            for r in range(lo, hi):
                one(r)
        else:
            lax.fori_loop(lo, hi, one, 0, unroll=8)

    arrived = lambda slot: pltpu.make_async_copy(ys_ref.at[pl.ds(0, tm), :], bufs[slot], sem.at[slot])

    @pl.when(i == 0)
    def _():
        request(0, 0, 0, tm, False)

    def step(cur):
        arrived(cur).wait()
        request(jnp.minimum(i + 1, n_tiles - 1), 1 - cur, 0, tm, True)
        x32 = bufs[cur][...]
        x_ref[...] = x32
        _inproj_body(x32.astype(BF16), wa_ref, wb_ref, wg_ref, wgt_ref, wkt_ref, brow_ref, bcol_ref, q_ref, k_ref, v_ref,
                     o_ref, u_ref, ga_ref, gb_ref, gcol_ref, grow_ref, kt_ref, n_prompt_tiles)

        @pl.when(i == n_tiles - 1)
        def _():
            arrived(1 - cur).wait()

    for parity in (0, 1):
        pl.when(i % 2 == parity)(functools.partial(step, parity))


def _inproj_body(x, wa_ref, wb_ref, wg_ref, wgt_ref, wkt_ref, brow_ref, bcol_ref,
                 q_ref, k_ref, v_ref, o_ref, u_ref, ga_ref, gb_ref, gcol_ref, grow_ref, kt_ref, n_prompt_tiles):
    @pl.when(pl.program_id(0) >= n_prompt_tiles)
    def _():
        kt = lax.dot_general(wkt_ref[...], x, (((1,), (1,)), ((), ())), preferred_element_type=F32)
        kt_ref[...] = (kt * K_SCALE).astype(BF16)

    def seg(i):
        ref, lo, hi = _SEGS[i]
        return jnp.dot(x, (wa_ref, wb_ref)[ref][:, lo:hi], preferred_element_type=F32)

    q_ref[...] = seg(0).astype(BF16)
    k_ref[...] = (seg(1) * K_SCALE).astype(BF16)
    v_ref[...] = seg(2).astype(BF16)
    o_ref[...] = seg(3)
    u_ref[...] = seg(4)
    ga_ref[...] = seg(5)
    gb_ref[...] = seg(6)
    g = jnp.dot(x, wg_ref[...], preferred_element_type=F32) + brow_ref[...]
    lane = lax.broadcasted_iota(I32, g.shape, 1)
    gcol_ref[...] = jnp.where(lane < N_HEADS, g, jax.nn.log_sigmoid(g))
    gt = lax.dot_general(wgt_ref[...], x, (((1,), (1,)), ((), ())), preferred_element_type=F32) + bcol_ref[...]
    sub = lax.broadcasted_iota(I32, gt.shape, 0)
    grow_ref[...] = jnp.where(sub < N_HEADS, gt, jax.nn.log_sigmoid(gt))


def _inproj(x, w, layer, n_prompt, dest=None):
    n = sum(a.shape[0] for a in x) if dest is None else dest.shape[0]
    tm = _row_tile(math.gcd(n_prompt, n - n_prompt))
    ntp = n_prompt // tm
    row = lambda w: pl.BlockSpec((tm, w), lambda i, *_: (i, 0))
    consts = (w["w_qkvo"], w["w_rest"], w["w_gate"], w["w_gate_t"], w["w_key_t"], w["b_row"], w["b_col"])
    const_specs = [_layer_spec(c, layer) for c in consts]
    out_shape = [
        jax.ShapeDtypeStruct((n, D_MLSTM), BF16), jax.ShapeDtypeStruct((n, D_MLSTM), BF16),
        jax.ShapeDtypeStruct((n, D_MLSTM), BF16), jax.ShapeDtypeStruct((n, D_MLSTM), F32),
        jax.ShapeDtypeStruct((n, D_POOL), F32), jax.ShapeDtypeStruct((n, D_MODEL), F32),
        jax.ShapeDtypeStruct((n, D_MODEL), F32), jax.ShapeDtypeStruct((n, LANES), F32),
        jax.ShapeDtypeStruct((GATE_ROWS, n), F32), jax.ShapeDtypeStruct((D_MLSTM, n - n_prompt), BF16)]
    out_specs = [row(D_MLSTM), row(D_MLSTM), row(D_MLSTM), row(D_MLSTM), row(D_POOL), row(D_MODEL),
                 row(D_MODEL), row(LANES), pl.BlockSpec((GATE_ROWS, tm), lambda i, *_: (0, i)),
                 pl.BlockSpec((D_MLSTM, tm), lambda i, *_: (0, jnp.maximum(i - ntp, 0))), row(D_MODEL)]
    out_shape.append(jax.ShapeDtypeStruct((n, D_MODEL), F32))
    if dest is None:
        return pl.pallas_call(
            functools.partial(_inproj_kernel, n_prompt_tiles=ntp),
            grid=(n // tm,),
            in_specs=[pl.BlockSpec((tm, D_MODEL), lambda i: (jnp.minimum(i, ntp - 1), 0)),
                      pl.BlockSpec((tm, D_MODEL), lambda i: (jnp.maximum(i - ntp, 0), 0))] + const_specs,
            out_specs=out_specs,
            out_shape=out_shape,
            compiler_params=_params(("arbitrary",)),
            name="inproj",
        )(*x, *consts)
    grid_spec = pltpu.PrefetchScalarGridSpec(
        num_scalar_prefetch=1,
        grid=(n // tm,),
        in_specs=[pl.BlockSpec(memory_space=pl.ANY)] + const_specs,
        out_specs=out_specs,
        scratch_shapes=[pltpu.VMEM((tm, D_MODEL), F32), pltpu.VMEM((tm, D_MODEL), F32),
                        pltpu.SemaphoreType.DMA((2,))])
    return pl.pallas_call(
        functools.partial(_inproj_gather_kernel, n_prompt_tiles=ntp, n_tiles=n // tm, tm=tm),
        grid_spec=grid_spec,
        out_shape=out_shape,
        compiler_params=_params(("arbitrary",)),
        name="inproj_gather",
    )(dest, x, *consts)


def _head_out(hval, o, gain):
    mu = jnp.mean(hval, axis=1, keepdims=True)
    xc = hval - mu
    var = jnp.mean(xc * xc, axis=1, keepdims=True)
    return (jax.nn.sigmoid(o) * (xc * lax.rsqrt(var + LN_EPS) * gain)).astype(BF16)


def _split3(x):
    x1 = x.astype(BF16)
    r1 = x - x1.astype(F32)
    x2 = r1.astype(BF16)
    x3 = (r1 - x2.astype(F32)).astype(BF16)
    return x1, x2, x3


def _mlstm_prompt_kernel(*refs, rows, strip, chained):
    if chained:
        q_ref, k_ref, v_ref, gcol_ref, grow_ref, o_ref, gain_ref, _, hg_ref, c_ref, n_ref, m_ref = refs
    else:
        q_ref, k_ref, v_ref, gcol_ref, grow_ref, o_ref, gain_ref, hg_ref, c_ref, n_ref, m_ref = refs

    @pl.when(pl.program_id(1) == 0)
    def _():
        c_ref[...] = jnp.zeros_like(c_ref)
        n_ref[...] = jnp.zeros_like(n_ref)
        m_ref[...] = jnp.zeros_like(m_ref)

    t_idx = lax.broadcasted_iota(I32, (rows, 1), 0)
    s_idx = lax.broadcasted_iota(I32, (1, rows), 1)
    mask = s_idx <= t_idx
    lower = jnp.where(mask, 1.0, 0.0).astype(BF16)
    upper = jnp.where(t_idx <= s_idx, 1.0, 0.0).astype(BF16)
    gcol = gcol_ref[...]
    grow = grow_ref[...]
    bcol = sum(jnp.dot(lower, p, preferred_element_type=F32) for p in _split3(gcol))
    brow = sum(jnp.dot(p, upper, preferred_element_type=F32) for p in _split3(grow))
    ones = jnp.ones((rows, LANES), BF16)
    neg_inf = jnp.float32(-jnp.inf)
    heads = range(N_HEADS)
    sls = [slice(h * HEAD_DIM, (h + 1) * HEAD_DIM) for h in heads]
    qs = [q_ref[:, sl] for sl in sls]
    ks = [k_ref[:, sl] for sl in sls]
    vos = [jnp.concatenate([v_ref[:, sl], ones], axis=1) for sl in sls]
    b_cs = [bcol[:, N_HEADS + h:N_HEADS + h + 1] for h in heads]
    b_rs = [brow[N_HEADS + h:N_HEADS + h + 1, :] for h in heads]
    g_rs = [grow[h:h + 1, :] - b_rs[h] for h in heads]
    m_prevs = [m_ref[0, h:h + 1, 0:1] for h in heads]
    c_prevs = [c_ref[0, h] for h in heads]
    n_prevs = [n_ref[0, h] for h in heads]
    qks = [lax.dot_general(qs[h], ks[h], (((1,), (1,)), ((), ())), preferred_element_type=F32) for h in heads]
    inters = [jnp.dot(qs[h], jnp.concatenate([c_prevs[h].astype(BF16), n_prevs[h].astype(BF16)], axis=1),
                      preferred_element_type=F32) for h in heads]
    gms = [jnp.where(mask, g_rs[h], neg_inf) for h in heads]
    tops = [jnp.maximum(m_prevs[h], jnp.max(gms[h], axis=1, keepdims=True)) for h in heads]
    ss = [(qks[h] * jnp.exp(gms[h] - tops[h])).astype(BF16) for h in heads]
    intras = [jnp.dot(ss[h], vos[h], preferred_element_type=F32) for h in heads]
    b_lasts = [b_rs[h][:, rows - 1:rows] for h in heads]
    d_lasts = [b_lasts[h] - b_cs[h] + gcol[:, h:h + 1] for h in heads]
    m_news = [jnp.maximum(b_lasts[h] + m_prevs[h], jnp.max(d_lasts[h], axis=0, keepdims=True)) for h in heads]
    kws = [(ks[h].astype(F32) * jnp.exp(d_lasts[h] - m_news[h])).astype(BF16) for h in heads]
    upds = [lax.dot_general(kws[h], vos[h], (((0,), (0,)), ((), ())), preferred_element_type=F32) for h in heads]
    s_inters = [jnp.exp(m_prevs[h] - tops[h]) for h in heads]
    nums = [s_inters[h] * inters[h][:, 0:HEAD_DIM] + intras[h][:, 0:HEAD_DIM] for h in heads]
    dens = [s_inters[h] * inters[h][:, HEAD_DIM:HEAD_DIM + 1] + intras[h][:, HEAD_DIM:HEAD_DIM + 1] for h in heads]
    hvals = [nums[h] / jnp.maximum(jnp.abs(dens[h]), jnp.exp(-(b_cs[h] + tops[h]))) for h in heads]
    mus = [jnp.mean(hvals[h], axis=1, keepdims=True) for h in heads]
    xcs = [hvals[h] - mus[h] for h in heads]
    vars_ = [jnp.mean(xcs[h] * xcs[h], axis=1, keepdims=True) for h in heads]
    for h in heads:
        hn = xcs[h] * lax.rsqrt(vars_[h] + LN_EPS) * gain_ref[:, sls[h]]
        hg_ref[:, sls[h]] = (jax.nn.sigmoid(o_ref[:, sls[h]]) * hn).astype(BF16)
    for h in heads:
        s_last = jnp.exp(b_lasts[h] + m_prevs[h] - m_news[h])
        c_ref[0, h] = s_last * c_prevs[h] + upds[h][:, 0:HEAD_DIM]
        n_ref[0, h] = s_last * n_prevs[h] + upds[h][:, HEAD_DIM:]
        m_ref[0, h:h + 1, :] = jnp.broadcast_to(m_news[h], (1, LANES))


def _mlstm_sample_kernel(*refs, seq_len, window, block, chained):
    if chained:
        (q_ref, k_ref, kt_ref, v_ref, gcol_ref, grow_ref, mtok_ref, o_ref, gain_ref, c0_ref, n0_ref, _,
         hg_ref, c1_ref, n1_ref, m1_ref) = refs
    else:
        (q_ref, k_ref, kt_ref, v_ref, gcol_ref, grow_ref, mtok_ref, o_ref, gain_ref, c0_ref, n0_ref,
         hg_ref, c1_ref, n1_ref, m1_ref) = refs
    n_seq = window // seq_len
    shift = int(math.log2(seq_len))
    w0 = (pl.program_id(0) % (block // window)) * window
    r_idx = lax.broadcasted_iota(I32, (window, 1), 0)
    t_idx = w0 + r_idx
    s_idx = lax.broadcasted_iota(I32, (1, block), 1)
    same = jnp.right_shift(t_idx, shift) == jnp.right_shift(s_idx, shift)
    mask = same & (s_idx <= t_idx)
    mask_t = same & (t_idx <= s_idx)
    r_seq = jnp.right_shift(r_idx, shift)
    l_seq = jnp.right_shift(s_idx - w0, shift)
    gcol = gcol_ref[...]
    grow = grow_ref[...]
    mtok = mtok_ref[...]
    neg_inf = jnp.float32(-jnp.inf)
    for h in range(N_HEADS):
        sl = slice(h * HEAD_DIM, (h + 1) * HEAD_DIM)
        q = q_ref[:, sl]
        k = k_ref[:, sl]
        kt = kt_ref[sl, :]
        v = v_ref[:, sl]
        li_r = grow[h:h + 1, :]
        lf_r = grow[N_HEADS + h:N_HEADS + h + 1, :]
        li_c = gcol[:, h:h + 1]
        lf_c = gcol[:, N_HEADS + h:N_HEADS + h + 1]
        m_c = mtok[:, h:h + 1]
        b_c = jnp.sum(jnp.where(mask, lf_r, 0.0), axis=1, keepdims=True)
        b_r = jnp.sum(jnp.where(mask_t, lf_c, 0.0), axis=0, keepdims=True)
        dmat = jnp.where(mask, b_c - b_r + li_r, neg_inf)
        inter = b_c + m_c
        m_t = jnp.maximum(inter, jnp.max(dmat, axis=1, keepdims=True))
        s_inter = jnp.exp(inter - m_t)
        s = jnp.dot(q, kt, preferred_element_type=F32) * jnp.exp(dmat - m_t)
        intra = jnp.dot(s.astype(BF16), v, preferred_element_type=F32)
        qf = q.astype(F32)
        qc = qn = None
        for j in range(n_seq):
            qc_j = jnp.dot(q, c0_ref[j, h].astype(BF16), preferred_element_type=F32)
            qn_j = jnp.sum(qf * n0_ref[j, h:h + 1, :], axis=1, keepdims=True)
            qc = qc_j if j == 0 else jnp.where(r_seq == j, qc_j, qc)
            qn = qn_j if j == 0 else jnp.where(r_seq == j, qn_j, qn)
        num = s_inter * qc + intra
        den = s_inter * qn + jnp.sum(s, axis=1, keepdims=True)
        hval = num / jnp.maximum(jnp.abs(den), jnp.exp(-m_t))
        hg_ref[:, sl] = _head_out(hval, o_ref[:, sl], gain_ref[:, sl])
        kf = k.astype(F32)
        ktf = kt.astype(F32)
        for j in range(n_seq):
            lsel = l_seq == j
            m_j = mtok[j * seq_len:j * seq_len + 1, h:h + 1]
            b_last = jnp.sum(jnp.where(lsel, lf_r, 0.0), axis=1, keepdims=True)
            d_r = jnp.where(lsel, b_last - b_r + li_r, neg_inf)
            d_c = jnp.where(r_seq == j, b_last - b_c + li_c, neg_inf)
            m_new = jnp.maximum(b_last + m_j, jnp.max(d_r, axis=1, keepdims=True))
            s_last = jnp.exp(b_last + m_j - m_new)
            kwt = (ktf * jnp.exp(d_r - m_new)).astype(BF16)
            c1_ref[j, h] = s_last * c0_ref[j, h] + jnp.dot(kwt, v, preferred_element_type=F32)
            n1_ref[j, h:h + 1, :] = (s_last * n0_ref[j, h:h + 1, :]
                                     + jnp.sum(kf * jnp.exp(d_c - m_new), axis=0, keepdims=True))
            m1_ref[j, h:h + 1, :] = jnp.broadcast_to(m_new, (1, LANES))


def _chain(c_all):
    if c_all is None:
        return [], []
    return [c_all], [pl.BlockSpec(memory_space=pl.ANY)]


def _mlstm_prompt(q, k, v, gcol, grow, o, gain, c_all, layer, n_seq, seq_len):
    n = n_seq * seq_len
    chunk = math.gcd(seq_len, MLSTM_PROMPT_CHUNK)
    nc = seq_len // chunk
    row = lambda w: pl.BlockSpec((chunk, w), lambda b, c: (b * nc + c, 0))
    st = lambda *tail: pl.BlockSpec((1,) + tail, lambda b, c: (b,) + (0,) * len(tail))
    extra, extra_specs = _chain(c_all)
    n_in = 7
    return pl.pallas_call(
        functools.partial(_mlstm_prompt_kernel, rows=chunk, strip=min(chunk, MLSTM_STRIP), chained=bool(extra)),
        grid=(n_seq, nc),
        in_specs=[row(D_MLSTM), row(D_MLSTM), row(D_MLSTM), row(LANES),
                  pl.BlockSpec((GATE_ROWS, chunk), lambda b, c: (0, b * nc + c)),
                  row(D_MLSTM), _layer_spec(gain, layer)] + extra_specs,
        out_specs=(row(D_MLSTM),
                   pl.BlockSpec((None, 1, N_HEADS, HEAD_DIM, HEAD_DIM), lambda b, c: (layer, b, 0, 0, 0)),
                   st(N_HEADS, HEAD_DIM, LANES), st(N_HEADS, LANES)),
        out_shape=(jax.ShapeDtypeStruct((n, D_MLSTM), BF16),
                   jax.ShapeDtypeStruct((DEPTH, n_seq, N_HEADS, HEAD_DIM, HEAD_DIM), F32),
                   jax.ShapeDtypeStruct((n_seq, N_HEADS, HEAD_DIM, LANES), F32),
                   jax.ShapeDtypeStruct((n_seq, N_HEADS, LANES), F32)),
        input_output_aliases={n_in: 1} if extra else {},
        compiler_params=_params(("parallel", "arbitrary")),
        name="mlstm_prompt",
    )(q, k, v, gcol, grow, o, gain, *extra)


def _mlstm_sample(q, k, kt, v, gcol, grow, mtok, o, gain, c0, n0, c_all, layer, first_row, n_seq, seq_len):
    n = n_seq * seq_len
    window, block = SAMPLE_WINDOW, SAMPLE_BLOCK_ROWS
    assert n % block == 0 and first_row % block == 0 and window % seq_len == 0
    per_win = window // seq_len
    sub = block // window
    row = lambda w: pl.BlockSpec((window, w), lambda i: (first_row // window + i, 0))
    extra, extra_specs = _chain(c_all)
    n_in = 11
    return pl.pallas_call(
        functools.partial(_mlstm_sample_kernel, seq_len=seq_len, window=window, block=block, chained=bool(extra)),
        grid=(n // window,),
        in_specs=[row(D_MLSTM), row(D_MLSTM),
                  pl.BlockSpec((D_MLSTM, block), lambda i: (0, i // sub)),
                  pl.BlockSpec((block, D_MLSTM), lambda i: (first_row // block + i // sub, 0)),
                  row(LANES),
                  pl.BlockSpec((GATE_ROWS, block), lambda i: (0, first_row // block + i // sub)),
                  pl.BlockSpec((window, LANES), lambda i: (i, 0)),
                  row(D_MLSTM), _layer_spec(gain, layer),
                  pl.BlockSpec((None, per_win, N_HEADS, HEAD_DIM, HEAD_DIM), lambda i: (layer, i, 0, 0, 0)),
                  pl.BlockSpec((None, per_win, N_HEADS, HEAD_DIM), lambda i: (layer, i, 0, 0))] + extra_specs,
        out_specs=(pl.BlockSpec((window, D_MLSTM), lambda i: (i, 0)),
                   pl.BlockSpec((None, per_win, N_HEADS, HEAD_DIM, HEAD_DIM), lambda i: (layer, i, 0, 0, 0)),
                   pl.BlockSpec((per_win, N_HEADS, HEAD_DIM), lambda i: (i, 0, 0)),
                   pl.BlockSpec((per_win, N_HEADS, LANES), lambda i: (i, 0, 0))),
        out_shape=(jax.ShapeDtypeStruct((n, D_MLSTM), BF16),
                   jax.ShapeDtypeStruct((DEPTH, n_seq, N_HEADS, HEAD_DIM, HEAD_DIM), F32),
                   jax.ShapeDtypeStruct((n_seq, N_HEADS, HEAD_DIM), F32),
                   jax.ShapeDtypeStruct((n_seq, N_HEADS, LANES), F32)),
        input_output_aliases={n_in: 1} if extra else {},
        compiler_params=_params(("parallel",)),
        name="mlstm_sample",
    )(q, k, kt, v, gcol, grow, mtok, o, gain, c0, n0, *extra)


def _pool_prompt_kernel(u_ref, prev_ref, out_ref, tail_ref, *, tm, tiles_per_seq):
    tile = pl.program_id(0) % tiles_per_seq
    head = 16
    tail_ref[...] = u_ref[tm - head:tm, :]
    pos = tile * tm + lax.broadcasted_iota(I32, (tm, 1), 0)
    for g, w in enumerate(POOL_WINDOWS):
        sl = slice(g * POOL_GROUP_DIM, (g + 1) * POOL_GROUP_DIM)
        acc = jnp.concatenate([jnp.where(tile == 0, 0.0, prev_ref[:, sl]), u_ref[:, sl]], axis=0)
        span = 1
        while span < w:
            acc = acc + pltpu.roll(acc, span, 0)
            span *= 2
        cnt = jnp.minimum(pos + 1, w).astype(F32)
        out_ref[:, sl] = (acc[head:] / cnt - u_ref[:, sl]).astype(BF16)


def _pool_prompt(u, n_seq, seq_len):
    n = n_seq * seq_len
    tm = _row_tile(seq_len)
    head = 16
    return pl.pallas_call(
        functools.partial(_pool_prompt_kernel, tm=tm, tiles_per_seq=seq_len // tm),
        grid=(n // tm,),
        in_specs=[pl.BlockSpec((tm, D_POOL), lambda i: (i, 0)),
                  pl.BlockSpec((head, D_POOL), lambda i: (jnp.maximum(i * (tm // head) - 1, 0), 0))],
        out_specs=(pl.BlockSpec((tm, D_POOL), lambda i: (i, 0)),
                   pl.BlockSpec((head, D_POOL), lambda i: (i // (seq_len // tm), 0))),
        out_shape=(jax.ShapeDtypeStruct((n, D_POOL), BF16), jax.ShapeDtypeStruct((n_seq * head, D_POOL), F32)),
        compiler_params=_params(("arbitrary",)),
        name="pool_prompt",
    )(u, u)


def _pool_sample_kernel(u_ref, buf_ref, out_ref, nbuf_ref, ext_ref, *, seq_len, start):
    ext_ref[:, 0:POOL_BUF, :] = buf_ref[...]
    ext_ref[:, POOL_BUF:POOL_BUF + seq_len, :] = u_ref[...]
    pos = start + lax.broadcasted_iota(I32, (1, seq_len, 1), 1)
    for g, w in enumerate(POOL_WINDOWS):
        sl = slice(g * POOL_GROUP_DIM, (g + 1) * POOL_GROUP_DIM)
        acc = ext_ref[:, POOL_BUF:POOL_BUF + seq_len, sl]
        for d in range(1, w):
            acc = acc + ext_ref[:, POOL_BUF - d:POOL_BUF - d + seq_len, sl]
        cnt = jnp.minimum(pos + 1, w).astype(F32)
        out_ref[:, :, sl] = (acc / cnt - u_ref[:, :, sl]).astype(BF16)
    nbuf_ref[...] = ext_ref[:, seq_len:seq_len + POOL_BUF, :]


def _pool_sample(u3, buf, layer, start):
    n_seq, seq_len, _ = u3.shape
    bs = _row_tile(n_seq, 32)
    spec = lambda r: pl.BlockSpec((bs, r, D_POOL), lambda i: (i, 0, 0))
    return pl.pallas_call(
        functools.partial(_pool_sample_kernel, seq_len=seq_len, start=start),
        grid=(n_seq // bs,),
        in_specs=[spec(seq_len), pl.BlockSpec((None, bs, POOL_BUF, D_POOL), lambda i: (layer, i, 0, 0))],
        out_specs=(spec(seq_len), spec(POOL_BUF)),
        out_shape=(jax.ShapeDtypeStruct((n_seq, seq_len, D_POOL), BF16),
                   jax.ShapeDtypeStruct((n_seq, POOL_BUF, D_POOL), F32)),
        scratch_shapes=[pltpu.VMEM((bs, POOL_BUF + seq_len + 5, D_POOL), F32)],
        compiler_params=_params(("parallel",)),
        name="pool_sample",
    )(u3, buf)


def _layer_norm(y, g, b):
    mu = jnp.mean(y, axis=1, keepdims=True)
    yc = y - mu
    var = jnp.mean(yc * yc, axis=1, keepdims=True)
    return yc * lax.rsqrt(var + LN_EPS) * g + b


def _route(logits_t):
    tokens = logits_t.shape[1]
    grp = lax.broadcasted_iota(I32, (8, tokens), 0)
    live = grp < N_EXPERT_GROUPS
    neg_inf = jnp.float32(-jnp.inf)
    lm = [jnp.where(live, logits_t[8 * m:8 * m + 8, :], neg_inf) for m in range(EXPERTS_PER_GROUP)]
    mx = jnp.max(jnp.maximum(jnp.maximum(lm[0], lm[1]), jnp.maximum(lm[2], lm[3])), axis=0, keepdims=True)
    ex = [jnp.exp(l - mx) for l in lm]
    tot = jnp.sum(ex[0] + ex[1] + ex[2] + ex[3], axis=0, keepdims=True)
    p = [e / tot for e in ex]
    top1 = jnp.maximum(jnp.maximum(p[0], p[1]), jnp.maximum(p[2], p[3]))
    i1 = jnp.where(p[0] == top1, 0, jnp.where(p[1] == top1, 1, jnp.where(p[2] == top1, 2, 3)))
    r = [jnp.where(i1 == m, -1.0, p[m]) for m in range(EXPERTS_PER_GROUP)]
    top2 = jnp.maximum(jnp.maximum(r[0], r[1]), jnp.maximum(r[2], r[3]))
    i2 = jnp.where(r[0] == top2, 0, jnp.where(r[1] == top2, 1, jnp.where(r[2] == top2, 2, 3)))
    gscore = jnp.where(live, top1 + top2, neg_inf)
    gmax = jnp.max(gscore, axis=0, keepdims=True)
    gsel = jnp.min(jnp.where(gscore == gmax, grp, 8), axis=0, keepdims=True)
    chosen = grp == gsel
    tsum = top1 + top2
    w1 = top1 / tsum
    w2 = top2 / tsum
    first_is_lo = i1 < i2
    lo = jnp.minimum(i1, i2)
    hi = jnp.maximum(i1, i2)
    pair = jnp.where(lo == 0, hi - 1, jnp.where(lo == 1, hi + 1, 5))
    pick = lambda a: jnp.sum(jnp.where(chosen, a, jnp.zeros_like(a)), axis=0, keepdims=True)
    cls = pick(grp * len(PAIRS) + pair)
    w_lo = pick(jnp.where(first_is_lo, w1, w2))
    w_hi = pick(jnp.where(first_is_lo, w2, w1))
    return cls, w_lo, w_hi


def _mix_kernel(hgp_ref, hgs_ref, plp_ref, pls_ref, ga_ref, gb_ref, x_ref, wpool_ref, pscale_ref, wa_ref, wb_ref,
                wout_ref, g1_ref, b1_ref, wr_ref, br_ref,
                dest_ref, cnt_ref, xs_ref,
                rows0_ref, rows1_ref, dvm_ref, dsm0_ref, dsm1_ref, carry_ref, row_sem, idx_sem,
                *, tm, n_prompt_tiles, n_tiles, capacity):
    i = pl.program_id(0)
    rows = (rows0_ref, rows1_ref)
    dsm = (dsm0_ref, dsm1_ref)
    spare = N_CLASSES * capacity
    sent = lambda s: pltpu.make_async_copy(rows[s], xs_ref.at[pl.ds(0, tm), :], row_sem.at[s])

    def send(s, unrolled):
        def one(r, carry=0):
            pltpu.make_async_copy(rows[s].at[pl.ds(r, 1), :], xs_ref.at[pl.ds(dsm[s][r], 1), :],
                                  row_sem.at[s]).start()
            return carry
        if unrolled:
            for r in range(tm):
                one(r)
        else:
            lax.fori_loop(0, tm, one, 0, unroll=8)

    @pl.when(i == 0)
    def _():
        carry_ref[...] = jnp.zeros_like(carry_ref)
        rows0_ref[...] = jnp.zeros_like(rows0_ref)
        rows1_ref[...] = jnp.zeros_like(rows1_ref)

        def spare_rows(r, carry):
            dsm0_ref[r] = spare + r
            return carry

        lax.fori_loop(0, tm, spare_rows, 0)
        send(0, False)
        dvm_ref[...] = spare + tm + lax.broadcasted_iota(I32, (8, tm), 1)
        pltpu.make_async_copy(dvm_ref.at[0], dsm1_ref, idx_sem).start()

    for parity in (0, 1):
        pl.when(i % 2 == parity)(functools.partial(
            _mix_step, parity, i, hgp_ref, hgs_ref, plp_ref, pls_ref, ga_ref, gb_ref, x_ref, wpool_ref, pscale_ref,
            wa_ref, wb_ref, wout_ref, g1_ref, b1_ref, wr_ref, br_ref, dest_ref, cnt_ref, rows, dvm_ref, dsm,
            carry_ref, idx_sem, send, sent, tm, n_prompt_tiles, n_tiles, capacity))


def _mix_step(cur, i, hgp_ref, hgs_ref, plp_ref, pls_ref, ga_ref, gb_ref, x_ref, wpool_ref, pscale_ref, wa_ref,
              wb_ref, wout_ref, g1_ref, b1_ref, wr_ref, br_ref, dest_ref, cnt_ref, rows, dvm_ref, dsm, carry_ref,
              idx_sem, send, sent, tm, n_prompt_tiles, n_tiles, capacity):
    slots_arrived = lambda s: pltpu.make_async_copy(dvm_ref.at[0], dsm[s], idx_sem)
    slots_arrived(1 - cur).wait()
    send(1 - cur, True)
    is_prompt = i < n_prompt_tiles
    hg = jnp.where(is_prompt, hgp_ref[...], hgs_ref[...])
    pooled = jnp.where(is_prompt, plp_ref[...], pls_ref[...])
    ya = jnp.dot(hg, wa_ref[...], preferred_element_type=F32)
    parts = []
    for g in range(len(POOL_WINDOWS)):
        sl = slice(g * POOL_GROUP_DIM, (g + 1) * POOL_GROUP_DIM)
        parts.append(jnp.dot(pooled[:, sl], wpool_ref[g], preferred_element_type=F32))
    pl_lin = jnp.concatenate(parts, axis=1) * pscale_ref[...]
    yb = jnp.dot(pl_lin.astype(BF16), wb_ref[...], preferred_element_type=F32)
    mix = jax.nn.sigmoid(ga_ref[...]) * ya + jax.nn.sigmoid(gb_ref[...]) * yb
    res = jnp.dot(mix.astype(BF16), wout_ref[...], preferred_element_type=F32)
    x1 = _layer_norm(ALPHA * x_ref[...] + res, g1_ref[...], b1_ref[...])

    nt = lambda a, b: lax.dot_general(a, b, (((1,), (1,)), ((), ())), preferred_element_type=F32)
    wr = wr_ref[...]
    wr_hi = wr.astype(BF16)
    wr_lo = (wr - wr_hi.astype(F32)).astype(BF16)
    x1_hi = x1.astype(BF16)
    x1_lo = (x1 - x1_hi.astype(F32)).astype(BF16)
    logits_t = nt(wr_hi, x1_hi) + (nt(wr_hi, x1_lo) + nt(wr_lo, x1_hi)) + br_ref[...]
    cls, w_lo, w_hi = _route(logits_t)

    onehot = lax.broadcasted_iota(I32, (CLASS_ROWS, tm), 0) == cls
    earlier = lax.broadcasted_iota(I32, (tm, tm), 0) < lax.broadcasted_iota(I32, (tm, tm), 1)
    before = jnp.dot(jnp.where(onehot, 1.0, 0.0).astype(BF16), jnp.where(earlier, 1.0, 0.0).astype(BF16),
                     preferred_element_type=F32)
    seen = carry_ref[:, 0:1]
    rank = jnp.sum(jnp.where(onehot, before + seen, 0.0), axis=0, keepdims=True)
    carry_ref[...] = carry_ref[...] + jnp.sum(jnp.where(onehot, 1.0, 0.0), axis=1, keepdims=True)
    cnt_ref[...] = carry_ref[...]
    dest = cls * capacity + rank.astype(I32)
    dest_ref[...] = jnp.broadcast_to(dest, (8, tm))

    sent(cur).wait()
    wrows = jnp.concatenate([w_lo, w_hi, jnp.zeros((LANES - 2, tm), F32)], axis=0)
    rows[cur][:, 0:D_MODEL] = x1
    rows[cur][:, D_MODEL:ROW_EXT] = wrows.T
    dvm_ref[...] = jnp.broadcast_to(dest, (8, tm))
    slots_arrived(cur).start()

    @pl.when(i == n_tiles - 1)
    def _():
        slots_arrived(cur).wait()
        send(cur, False)
        sent(cur).wait()
        sent(1 - cur).wait()


def _mix(hg_p, hg_s, pooled_p, pooled_s, ga, gb, x, w, layer, capacity):
    n = x.shape[0]
    n_p = hg_p.shape[0]
    tm = _row_tile(math.gcd(n_p, n - n_p))
    ntp = n_p // tm
    row = lambda w: pl.BlockSpec((tm, w), lambda i: (i, 0))
    prow = lambda w: pl.BlockSpec((tm, w), lambda i: (jnp.minimum(i, ntp - 1), 0))
    srow = lambda w: pl.BlockSpec((tm, w), lambda i: (jnp.maximum(i - ntp, 0), 0))
    per_layer = (w["w_pool"], w["pool_scale"], w["w_proj_a"], w["w_proj_b"], w["w_out"], w["ln1_g"], w["ln1_b"])
    shared = (w["w_router_t"], w["b_router_col"])
    consts = per_layer + shared
    return pl.pallas_call(
        functools.partial(_mix_kernel, tm=tm, n_prompt_tiles=ntp, n_tiles=n // tm, capacity=capacity),
        grid=(n // tm,),
        in_specs=[prow(D_MLSTM), srow(D_MLSTM), prow(D_POOL), srow(D_POOL), row(D_MODEL), row(D_MODEL), row(D_MODEL)]
                 + [_layer_spec(c, layer) for c in per_layer] + [_const_spec(c.shape) for c in shared],
        out_specs=(pl.BlockSpec((8, tm), lambda i: (0, i)),
                   pl.BlockSpec((CLASS_ROWS, LANES), lambda i: (0, 0)),
                   pl.BlockSpec(memory_space=pl.ANY)),
        out_shape=(jax.ShapeDtypeStruct((8, n), I32),
                   jax.ShapeDtypeStruct((CLASS_ROWS, LANES), F32),
                   jax.ShapeDtypeStruct((N_CLASSES * capacity + 2 * tm, ROW_EXT), F32)),
        scratch_shapes=[pltpu.VMEM((tm, ROW_EXT), F32), pltpu.VMEM((tm, ROW_EXT), F32), pltpu.VMEM((8, tm), I32),
                        pltpu.SMEM((tm,), I32), pltpu.SMEM((tm,), I32), pltpu.VMEM((CLASS_ROWS, LANES), F32),
                        pltpu.SemaphoreType.DMA((2,)), pltpu.SemaphoreType.DMA],
        compiler_params=_params(("arbitrary",)),
        name="mix",
    )(hg_p, hg_s, pooled_p, pooled_s, ga, gb, x, *consts)


def _moe_kernel(blk_ref, elo_ref, ehi_ref, nvalid_ref, ntiles_ref,
                xs_ref, wg_lo, wu_lo, wd_lo, wg_hi, wu_hi, wd_hi, g2_ref, b2_ref, ys_ref):
    i = pl.program_id(0)

    @pl.when(i < ntiles_ref[0])
    def _():
        valid = lax.broadcasted_iota(I32, (MOE_TILE, 1), 0) < nvalid_ref[i]
        xe = xs_ref[...]
        x = jnp.where(valid, xe[:, 0:D_MODEL], 0.0)
        w_lo = jnp.where(valid, xe[:, D_MODEL:D_MODEL + 1], 0.0)
        w_hi = jnp.where(valid, xe[:, D_MODEL + 1:D_MODEL + 2], 0.0)
        xb = x.astype(BF16)

        def expert(wg, wu, wd, w):
            g = jnp.dot(xb, wg[0], preferred_element_type=F32)
            u = jnp.dot(xb, wu[0], preferred_element_type=F32)
            hid = (g * jax.nn.sigmoid(g)) * u * w
            return jnp.dot(hid.astype(BF16), wd[0], preferred_element_type=F32)

        y = expert(wg_lo, wu_lo, wd_lo, w_lo) + expert(wg_hi, wu_hi, wd_hi, w_hi)
        ys_ref[...] = _layer_norm(ALPHA * x + y, g2_ref[...], b2_ref[...])


def _moe(xs, tables, w, layer, max_tiles):
    blk, e_lo, e_hi, n_valid, n_tiles = tables
    up = lambda sel: pl.BlockSpec((None, 1, D_MODEL, D_EXPERT),
                                  lambda i, b, lo, hi, nv, nt: (layer, (lo, hi)[sel][i], 0, 0))
    down = lambda sel: pl.BlockSpec((None, 1, D_EXPERT, D_MODEL),
                                    lambda i, b, lo, hi, nv, nt: (layer, (lo, hi)[sel][i], 0, 0))
    grid_spec = pltpu.PrefetchScalarGridSpec(
        num_scalar_prefetch=5,
        grid=(max_tiles,),
        in_specs=[pl.BlockSpec((MOE_TILE, ROW_EXT), lambda i, b, *_: (b[i], 0)),
                  up(0), up(0), down(0), up(1), up(1), down(1),
                  _layer_spec(w["ln2_g"], layer), _layer_spec(w["ln2_b"], layer)],
        out_specs=pl.BlockSpec((MOE_TILE, D_MODEL), lambda i, b, *_: (b[i], 0)))
    return pl.pallas_call(
        _moe_kernel,
        grid_spec=grid_spec,
        out_shape=jax.ShapeDtypeStruct((xs.shape[0], D_MODEL), F32),
        compiler_params=_params(("arbitrary",)),
        name="moe",
    )(blk, e_lo, e_hi, n_valid, n_tiles, xs, w["w_e_gate"], w["w_e_up"], w["w_e_down"],
      w["w_e_gate"], w["w_e_up"], w["w_e_down"], w["ln2_g"], w["ln2_b"])


def _tile_tables(counts, capacity, max_tiles):
    cnt = counts[:N_CLASSES, 0].astype(I32)
    tiles = (cnt + MOE_TILE - 1) // MOE_TILE
    ends = jnp.cumsum(tiles)
    starts = ends - tiles
    n_tiles = ends[-1]
    t = jnp.minimum(jnp.arange(max_tiles, dtype=I32), jnp.maximum(n_tiles - 1, 0))
    cls = jnp.sum((ends[None, :] <= t[:, None]).astype(I32), axis=1)
    onehot = (jnp.arange(N_CLASSES, dtype=I32)[None, :] == cls[:, None]).astype(I32)
    within = t - jnp.sum(onehot * starts[None, :], axis=1)
    n_valid = jnp.clip(jnp.sum(onehot * cnt[None, :], axis=1) - within * MOE_TILE, 0, MOE_TILE)
    blk = cls * (capacity // MOE_TILE) + within
    grp = cls // len(PAIRS)
    pair = cls % len(PAIRS)
    lo = jnp.where(pair < 3, 0, jnp.where(pair < 5, 1, 2))
    hi = jnp.where(pair < 3, pair + 1, jnp.where(pair < 5, pair - 1, 3))
    return (blk, grp * EXPERTS_PER_GROUP + lo, grp * EXPERTS_PER_GROUP + hi, n_valid,
            n_tiles.reshape(1).astype(I32))


def _unpermute_kernel(dest_ref, ys_ref, prompt_ref, sample_ref, buf_ref, sem, *, tm, n_prompt_tiles, n_tiles):
    i = pl.program_id(0)
    cur = i % 2

    def request(tile, slot):
        def fetch(r, carry):
            pltpu.make_async_copy(ys_ref.at[pl.ds(dest_ref[tile * tm + r], 1), :],
                                  buf_ref.at[slot, pl.ds(r, 1), :], sem.at[slot]).start()
            return carry

        lax.fori_loop(0, tm, fetch, 0, unroll=8)

    arrived = lambda slot: pltpu.make_async_copy(ys_ref.at[pl.ds(0, tm), :], buf_ref.at[slot], sem.at[slot])

    @pl.when(i == 0)
    def _():
        request(0, 0)

    @pl.when(i + 1 < n_tiles)
    def _():
        request(i + 1, 1 - cur)

    arrived(cur).wait()

    @pl.when(i < n_prompt_tiles)
    def _():
        prompt_ref[...] = buf_ref[cur]

    @pl.when(i >= n_prompt_tiles)
    def _():
        sample_ref[...] = buf_ref[cur]


def _unpermute(ys, dest, n_prompt):
    n = dest.shape[0]
    tm = _row_tile(math.gcd(n_prompt, n - n_prompt))
    ntp = n_prompt // tm
    grid_spec = pltpu.PrefetchScalarGridSpec(
        num_scalar_prefetch=1,
        grid=(n // tm,),
        in_specs=[pl.BlockSpec(memory_space=pl.ANY)],
        out_specs=(pl.BlockSpec((tm, D_MODEL), lambda i, d: (jnp.minimum(i, ntp - 1), 0)),
                   pl.BlockSpec((tm, D_MODEL), lambda i, d: (jnp.maximum(i - ntp, 0), 0))),
        scratch_shapes=[pltpu.VMEM((2, tm, D_MODEL), F32), pltpu.SemaphoreType.DMA((2,))])
    return pl.pallas_call(
        functools.partial(_unpermute_kernel, tm=tm, n_prompt_tiles=ntp, n_tiles=n // tm),
        grid_spec=grid_spec,
        out_shape=(jax.ShapeDtypeStruct((n_prompt, D_MODEL), F32),
                   jax.ShapeDtypeStruct((n - n_prompt, D_MODEL), F32)),
        compiler_params=_params(("arbitrary",)),
        name="unpermute",
    )(dest, ys)


def _prepare_weights(w_in, b_gate, hn_gain, w_pool, pool_scale, w_proj_a, w_proj_b, w_out, ln1_g, ln1_b,
                     ln2_g, ln2_b, w_router, b_router, w_e_gate, w_e_up, w_e_down):
    w_qkvo = w_in[:, :, :_G0].astype(BF16)
    w_rest = w_in[:, :, _U0:].astype(BF16)
    w_gate = w_in[:, :, _G0:_U0]
    w_gate_p = jnp.pad(w_gate, ((0, 0), (0, 0), (0, LANES - 2 * N_HEADS))).astype(BF16)
    w_gate_t = jnp.pad(jnp.swapaxes(w_gate, 1, 2), ((0, 0), (0, GATE_ROWS - 2 * N_HEADS), (0, 0))).astype(BF16)
    w_key_t = jnp.swapaxes(w_in[:, :, D_MLSTM:2 * D_MLSTM], 1, 2).astype(BF16)
    b_row = jnp.pad(b_gate, ((0, 0), (0, LANES - 2 * N_HEADS)))[:, None, :]
    b_col = jnp.pad(b_gate, ((0, 0), (0, GATE_ROWS - 2 * N_HEADS)))[:, :, None]
    wr = w_router.T.reshape(N_EXPERT_GROUPS, EXPERTS_PER_GROUP, D_MODEL).swapaxes(0, 1)
    wr = jnp.pad(wr, ((0, 0), (0, 8 - N_EXPERT_GROUPS), (0, 0))).reshape(ROUTER_ROWS, D_MODEL)
    br = b_router.reshape(N_EXPERT_GROUPS, EXPERTS_PER_GROUP).T
    br = jnp.pad(br, ((0, 0), (0, 8 - N_EXPERT_GROUPS))).reshape(ROUTER_ROWS, 1)
    per_row = lambda a: a.reshape(DEPTH, 1, -1)
    return dict(
        w_qkvo=w_qkvo, w_rest=w_rest, w_gate=w_gate_p, w_gate_t=w_gate_t, w_key_t=w_key_t, b_row=b_row, b_col=b_col,
        gain=per_row(hn_gain), w_pool=w_pool.astype(BF16), pool_scale=per_row(pool_scale),
        w_proj_a=w_proj_a.astype(BF16), w_proj_b=w_proj_b.astype(BF16), w_out=w_out.astype(BF16),
        ln1_g=per_row(ln1_g), ln1_b=per_row(ln1_b), ln2_g=per_row(ln2_g), ln2_b=per_row(ln2_b),
        w_router_t=wr, b_router_col=br,
        w_e_gate=w_e_gate.astype(BF16), w_e_up=w_e_up.astype(BF16), w_e_down=w_e_down.astype(BF16))


def kernel(x_prompt, x_sample, state_C, state_n, state_m, state_pool, w_in, b_gate, hn_gain, w_pool, pool_scale,
           w_proj_a, w_proj_b, w_out, ln1_g, ln1_b, ln2_g, ln2_b, w_router, b_router, w_e_gate, w_e_up, w_e_down):
    w = _prepare_weights(w_in, b_gate, hn_gain, w_pool, pool_scale, w_proj_a, w_proj_b, w_out, ln1_g, ln1_b,
                         ln2_g, ln2_b, w_router, b_router, w_e_gate, w_e_up, w_e_down)
    n_pseq, p_len, _ = x_prompt.shape
    n_sseq, s_len, _ = x_sample.shape
    n_p = n_pseq * p_len
    n_s = n_sseq * s_len
    n = n_p + n_s
    capacity = -(-n // MOE_TILE) * MOE_TILE
    max_tiles = n // MOE_TILE + N_CLASSES
    np_, mp, bp, ns, ms, bs = [], [], [], [], [], []
    c_p = c_s = None
    ys = (x_prompt.reshape(n_p, D_MODEL), x_sample.reshape(n_s, D_MODEL))
    dest = None
    for l in range(DEPTH):
        q, k, v, o, u, ga, gb, gcol, grow, kt, x = _inproj(ys, w, l, n_p, None if l == 0 else dest[0])
        hg_p, c_p, n1p, m1p = _mlstm_prompt(q, k, v, gcol, grow, o, w["gain"], c_p, l, n_pseq, p_len)
        mtok = jnp.pad(jnp.repeat(state_m[l], s_len, axis=0), ((0, 0), (0, LANES - N_HEADS)))
        hg_s, c_s, n1s, m1s = _mlstm_sample(q, k, kt, v, gcol, grow, mtok, o, w["gain"], state_C, state_n, c_s,
                                            l, n_p, n_sseq, s_len)
        pooled_p, tail = _pool_prompt(u, n_pseq, p_len)
        pooled_s, nbuf = _pool_sample(u[n_p:].reshape(n_sseq, s_len, D_POOL), state_pool, l, PAST_LEN)
        dest, counts, xs = _mix(hg_p, hg_s, pooled_p, pooled_s.reshape(n_s, D_POOL), ga, gb, x, w, l, capacity)
        ys = _moe(xs, _tile_tables(counts, capacity, max_tiles), w, l, max_tiles)
        np_.append(n1p[:, :, :, 0])
        mp.append(m1p[:, :, 0])
        bp.append(tail.reshape(n_pseq, -1, D_POOL)[:, -POOL_BUF:])
        ns.append(n1s)
        ms.append(m1s[:, :, 0])
        bs.append(nbuf)
    y_p, y_s = _unpermute(ys, dest[0], n_p)
    st = jnp.stack
    return (y_p.reshape(n_pseq, p_len, D_MODEL), y_s.reshape(n_sseq, s_len, D_MODEL),
            c_p, st(np_), st(mp), st(bp), c_s, st(ns), st(ms), st(bs))
```

```python
import functools
import math

import jax
import jax.numpy as jnp
from jax import lax
from jax.experimental import pallas as pl
from jax.experimental.pallas import tpu as pltpu

F32 = jnp.float32
BF16 = jnp.bfloat16
I32 = jnp.int32

D_MODEL = 1024
N_HEADS = 4
HEAD_DIM = 256
D_MLSTM = N_HEADS * HEAD_DIM
POOL_WINDOWS = (2, 4, 8, 16)
POOL_GROUP_DIM = 128
D_POOL = len(POOL_WINDOWS) * POOL_GROUP_DIM
POOL_BUF = 15
N_EXPERTS = 16
N_EXPERT_GROUPS = 4
EXPERTS_PER_GROUP = 4
D_EXPERT = 512
DEPTH = 4
PAST_LEN = 16384
ALPHA = (2 * DEPTH) ** 0.25
LN_EPS = 1e-5
K_SCALE = HEAD_DIM ** -0.5

LANES = 128
GATE_ROWS = 16
ROUTER_ROWS = 32
VMEM_LIMIT = 52 * 1024 * 1024
MLSTM_PROMPT_CHUNK = 512
MLSTM_STRIP = 256
SAMPLE_BLOCK_ROWS = 128
SAMPLE_WINDOW = 16

PAIRS = ((0, 1), (0, 2), (0, 3), (1, 2), (1, 3), (2, 3))
N_CLASSES = N_EXPERT_GROUPS * len(PAIRS)
CLASS_ROWS = 32
MOE_TILE = 256
ROW_EXT = D_MODEL + LANES

_G0 = 4 * D_MLSTM
_U0 = _G0 + 2 * N_HEADS
_SEGS = ((0, 0, 1024), (0, 1024, 2048), (0, 2048, 3072), (0, 3072, 4096), (1, 0, 512), (1, 512, 1536), (1, 1536, 2560))


def _params(sem, **kw):
    return pltpu.CompilerParams(dimension_semantics=sem, vmem_limit_bytes=VMEM_LIMIT, **kw)


def _const_spec(shape):
    nd = len(shape)
    return pl.BlockSpec(shape, lambda *_: (0,) * nd, pipeline_mode=pl.Buffered(1))


def _layer_spec(stacked, layer):
    tail = stacked.shape[1:]
    return pl.BlockSpec((None,) + tail, lambda *_: (layer,) + (0,) * len(tail), pipeline_mode=pl.Buffered(1))


def _row_tile(n, cap=512):
    t = cap
    while n % t:
        t //= 2
    return t


def _inproj_kernel(xp_ref, xs_ref, wa_ref, wb_ref, wg_ref, wgt_ref, wkt_ref, brow_ref, bcol_ref,
                   q_ref, k_ref, v_ref, o_ref, u_ref, ga_ref, gb_ref, gcol_ref, grow_ref, kt_ref, x_ref,
                   *, n_prompt_tiles):
    x32 = jnp.where(pl.program_id(0) < n_prompt_tiles, xp_ref[...], xs_ref[...])
    x_ref[...] = x32
    _inproj_body(x32.astype(BF16), wa_ref, wb_ref, wg_ref, wgt_ref, wkt_ref, brow_ref, bcol_ref,
                 q_ref, k_ref, v_ref, o_ref, u_ref, ga_ref, gb_ref, gcol_ref, grow_ref, kt_ref, n_prompt_tiles)


def _inproj_gather_kernel(dest_ref, ys_ref, wa_ref, wb_ref, wg_ref, wgt_ref, wkt_ref, brow_ref, bcol_ref,
                          q_ref, k_ref, v_ref, o_ref, u_ref, ga_ref, gb_ref, gcol_ref, grow_ref, kt_ref, x_ref,
                          xbuf0_ref, xbuf1_ref, sem, *, n_prompt_tiles, n_tiles, tm):
    i = pl.program_id(0)
    bufs = (xbuf0_ref, xbuf1_ref)

    def request(tile, slot, lo, hi, unrolled):
        def one(r, carry=0):
            pltpu.make_async_copy(ys_ref.at[pl.ds(dest_ref[tile * tm + r], 1), :],
                                  bufs[slot].at[pl.ds(r, 1), :], sem.at[slot]).start()
            return carry
        if unrolled:
            for r in range(lo, hi):
                one(r)
        else:
            lax.fori_loop(lo, hi, one, 0, unroll=8)

    arrived = lambda slot: pltpu.make_async_copy(ys_ref.at[pl.ds(0, tm), :], bufs[slot], sem.at[slot])

    @pl.when(i == 0)
    def _():
        request(0, 0, 0, tm, False)

    def step(cur):
        arrived(cur).wait()
        request(jnp.minimum(i + 1, n_tiles - 1), 1 - cur, 0, tm, True)
        x32 = bufs[cur][...]
        x_ref[...] = x32
        _inproj_body(x32.astype(BF16), wa_ref, wb_ref, wg_ref, wgt_ref, wkt_ref, brow_ref, bcol_ref, q_ref, k_ref, v_ref,
                     o_ref, u_ref, ga_ref, gb_ref, gcol_ref, grow_ref, kt_ref, n_prompt_tiles)

        @pl.when(i == n_tiles - 1)
        def _():
            arrived(1 - cur).wait()

    for parity in (0, 1):
        pl.when(i % 2 == parity)(functools.partial(step, parity))


def _inproj_body(x, wa_ref, wb_ref, wg_ref, wgt_ref, wkt_ref, brow_ref, bcol_ref,
                 q_ref, k_ref, v_ref, o_ref, u_ref, ga_ref, gb_ref, gcol_ref, grow_ref, kt_ref, n_prompt_tiles):
    @pl.when(pl.program_id(0) >= n_prompt_tiles)
    def _():
        kt = lax.dot_general(wkt_ref[...], x, (((1,), (1,)), ((), ())), preferred_element_type=F32)
        kt_ref[...] = (kt * K_SCALE).astype(BF16)

    def seg(i):
        ref, lo, hi = _SEGS[i]
        return jnp.dot(x, (wa_ref, wb_ref)[ref][:, lo:hi], preferred_element_type=F32)

    q_ref[...] = seg(0).astype(BF16)
    k_ref[...] = (seg(1) * K_SCALE).astype(BF16)
    v_ref[...] = seg(2).astype(BF16)
    o_ref[...] = seg(3)
    u_ref[...] = seg(4)
    ga_ref[...] = seg(5)
    gb_ref[...] = seg(6)
    g = jnp.dot(x, wg_ref[...], preferred_element_type=F32) + brow_ref[...]
    lane = lax.broadcasted_iota(I32, g.shape, 1)
    gcol_ref[...] = jnp.where(lane < N_HEADS, g, jax.nn.log_sigmoid(g))
    gt = lax.dot_general(wgt_ref[...], x, (((1,), (1,)), ((), ())), preferred_element_type=F32) + bcol_ref[...]
    sub = lax.broadcasted_iota(I32, gt.shape, 0)
    grow_ref[...] = jnp.where(sub < N_HEADS, gt, jax.nn.log_sigmoid(gt))


def _inproj(x, w, layer, n_prompt, dest=None):
    n = sum(a.shape[0] for a in x) if dest is None else dest.shape[0]
    tm = _row_tile(math.gcd(n_prompt, n - n_prompt))
    ntp = n_prompt // tm
    row = lambda w: pl.BlockSpec((tm, w), lambda i, *_: (i, 0))
    consts = (w["w_qkvo"], w["w_rest"], w["w_gate"], w["w_gate_t"], w["w_key_t"], w["b_row"], w["b_col"])
    const_specs = [_layer_spec(c, layer) for c in consts]
    out_shape = [
        jax.ShapeDtypeStruct((n, D_MLSTM), BF16), jax.ShapeDtypeStruct((n, D_MLSTM), BF16),
        jax.ShapeDtypeStruct((n, D_MLSTM), BF16), jax.ShapeDtypeStruct((n, D_MLSTM), F32),
        jax.ShapeDtypeStruct((n, D_POOL), F32), jax.ShapeDtypeStruct((n, D_MODEL), F32),
        jax.ShapeDtypeStruct((n, D_MODEL), F32), jax.ShapeDtypeStruct((n, LANES), F32),
        jax.ShapeDtypeStruct((GATE_ROWS, n), F32), jax.ShapeDtypeStruct((D_MLSTM, n - n_prompt), BF16)]
    out_specs = [row(D_MLSTM), row(D_MLSTM), row(D_MLSTM), row(D_MLSTM), row(D_POOL), row(D_MODEL),
                 row(D_MODEL), row(LANES), pl.BlockSpec((GATE_ROWS, tm), lambda i, *_: (0, i)),
                 pl.BlockSpec((D_MLSTM, tm), lambda i, *_: (0, jnp.maximum(i - ntp, 0))), row(D_MODEL)]
    out_shape.append(jax.ShapeDtypeStruct((n, D_MODEL), F32))
    if dest is None:
        return pl.pallas_call(
            functools.partial(_inproj_kernel, n_prompt_tiles=ntp),
            grid=(n // tm,),
            in_specs=[pl.BlockSpec((tm, D_MODEL), lambda i: (jnp.minimum(i, ntp - 1), 0)),
                      pl.BlockSpec((tm, D_MODEL), lambda i: (jnp.maximum(i - ntp, 0), 0))] + const_specs,
            out_specs=out_specs,
            out_shape=out_shape,
            compiler_params=_params(("arbitrary",)),
            name="inproj",
        )(*x, *consts)
    grid_spec = pltpu.PrefetchScalarGridSpec(
        num_scalar_prefetch=1,
        grid=(n // tm,),
        in_specs=[pl.BlockSpec(memory_space=pl.ANY)] + const_specs,
        out_specs=out_specs,
        scratch_shapes=[pltpu.VMEM((tm, D_MODEL), F32), pltpu.VMEM((tm, D_MODEL), F32),
                        pltpu.SemaphoreType.DMA((2,))])
    return pl.pallas_call(
        functools.partial(_inproj_gather_kernel, n_prompt_tiles=ntp, n_tiles=n // tm, tm=tm),
        grid_spec=grid_spec,
        out_shape=out_shape,
        compiler_params=_params(("arbitrary",)),
        name="inproj_gather",
    )(dest, x, *consts)


def _head_out(hval, o, gain):
    mu = jnp.mean(hval, axis=1, keepdims=True)
    xc = hval - mu
    var = jnp.mean(xc * xc, axis=1, keepdims=True)
    return (jax.nn.sigmoid(o) * (xc * lax.rsqrt(var + LN_EPS) * gain)).astype(BF16)


def _split3(x):
    x1 = x.astype(BF16)
    r1 = x - x1.astype(F32)
    x2 = r1.astype(BF16)
    x3 = (r1 - x2.astype(F32)).astype(BF16)
    return x1, x2, x3


def _mlstm_prompt_kernel(*refs, rows, strip, chained):
    if chained:
        q_ref, k_ref, v_ref, gcol_ref, grow_ref, o_ref, gain_ref, _, hg_ref, c_ref, n_ref, m_ref = refs
    else:
        q_ref, k_ref, v_ref, gcol_ref, grow_ref, o_ref, gain_ref, hg_ref, c_ref, n_ref, m_ref = refs

    @pl.when(pl.program_id(1) == 0)
    def _():
        c_ref[...] = jnp.zeros_like(c_ref)
        n_ref[...] = jnp.zeros_like(n_ref)
        m_ref[...] = jnp.zeros_like(m_ref)

    t_idx = lax.broadcasted_iota(I32, (rows, 1), 0)
    s_idx = lax.broadcasted_iota(I32, (1, rows), 1)
    mask = s_idx <= t_idx
    lower = jnp.where(mask, 1.0, 0.0).astype(BF16)
    upper = jnp.where(t_idx <= s_idx, 1.0, 0.0).astype(BF16)
    gcol = gcol_ref[...]
    grow = grow_ref[...]
    bcol = sum(jnp.dot(lower, p, preferred_element_type=F32) for p in _split3(gcol))
    brow = sum(jnp.dot(p, upper, preferred_element_type=F32) for p in _split3(grow))
    ones = jnp.ones((rows, LANES), BF16)
    neg_inf = jnp.float32(-jnp.inf)
    heads = range(N_HEADS)
    sls = [slice(h * HEAD_DIM, (h + 1) * HEAD_DIM) for h in heads]
    qs = [q_ref[:, sl] for sl in sls]
    ks = [k_ref[:, sl] for sl in sls]
    vos = [jnp.concatenate([v_ref[:, sl], ones], axis=1) for sl in sls]
    b_cs = [bcol[:, N_HEADS + h:N_HEADS + h + 1] for h in heads]
    b_rs = [brow[N_HEADS + h:N_HEADS + h + 1, :] for h in heads]
    g_rs = [grow[h:h + 1, :] - b_rs[h] for h in heads]
    m_prevs = [m_ref[0, h:h + 1, 0:1] for h in heads]
    c_prevs = [c_ref[0, h] for h in heads]
    n_prevs = [n_ref[0, h] for h in heads]
    qks = [lax.dot_general(qs[h], ks[h], (((1,), (1,)), ((), ())), preferred_element_type=F32) for h in heads]
    inters = [jnp.dot(qs[h], jnp.concatenate([c_prevs[h].astype(BF16), n_prevs[h].astype(BF16)], axis=1),
                      preferred_element_type=F32) for h in heads]
    gms = [jnp.where(mask, g_rs[h], neg_inf) for h in heads]
    tops = [jnp.maximum(m_prevs[h], jnp.max(gms[h], axis=1, keepdims=True)) for h in heads]
    ss = [(qks[h] * jnp.exp(gms[h] - tops[h])).astype(BF16) for h in heads]
    intras = [jnp.dot(ss[h], vos[h], preferred_element_type=F32) for h in heads]
    b_lasts = [b_rs[h][:, rows - 1:rows] for h in heads]
    d_lasts = [b_lasts[h] - b_cs[h] + gcol[:, h:h + 1] for h in heads]
    m_news = [jnp.maximum(b_lasts[h] + m_prevs[h], jnp.max(d_lasts[h], axis=0, keepdims=True)) for h in heads]
    kws = [(ks[h].astype(F32) * jnp.exp(d_lasts[h] - m_news[h])).astype(BF16) for h in heads]
    upds = [lax.dot_general(kws[h], vos[h], (((0,), (0,)), ((), ())), preferred_element_type=F32) for h in heads]
    s_inters = [jnp.exp(m_prevs[h] - tops[h]) for h in heads]
    nums = [s_inters[h] * inters[h][:, 0:HEAD_DIM] + intras[h][:, 0:HEAD_DIM] for h in heads]
    dens = [s_inters[h] * inters[h][:, HEAD_DIM:HEAD_DIM + 1] + intras[h][:, HEAD_DIM:HEAD_DIM + 1] for h in heads]
    hvals = [nums[h] / jnp.maximum(jnp.abs(dens[h]), jnp.exp(-(b_cs[h] + tops[h]))) for h in heads]
    mus = [jnp.mean(hvals[h], axis=1, keepdims=True) for h in heads]
    xcs = [hvals[h] - mus[h] for h in heads]
    vars_ = [jnp.mean(xcs[h] * xcs[h], axis=1, keepdims=True) for h in heads]
    for h in heads:
        hn = xcs[h] * lax.rsqrt(vars_[h] + LN_EPS) * gain_ref[:, sls[h]]
        hg_ref[:, sls[h]] = (jax.nn.sigmoid(o_ref[:, sls[h]]) * hn).astype(BF16)
    for h in heads:
        s_last = jnp.exp(b_lasts[h] + m_prevs[h] - m_news[h])
        c_ref[0, h] = s_last * c_prevs[h] + upds[h][:, 0:HEAD_DIM]
        n_ref[0, h] = s_last * n_prevs[h] + upds[h][:, HEAD_DIM:]
        m_ref[0, h:h + 1, :] = jnp.broadcast_to(m_news[h], (1, LANES))


def _mlstm_sample_kernel(*refs, seq_len, window, block, chained):
    if chained:
        (q_ref, k_ref, kt_ref, v_ref, gcol_ref, grow_ref, mtok_ref, o_ref, gain_ref, c0_ref, n0_ref, _,
         hg_ref, c1_ref, n1_ref, m1_ref) = refs
    else:
        (q_ref, k_ref, kt_ref, v_ref, gcol_ref, grow_ref, mtok_ref, o_ref, gain_ref, c0_ref, n0_ref,
         hg_ref, c1_ref, n1_ref, m1_ref) = refs
    n_seq = window // seq_len
    shift = int(math.log2(seq_len))
    w0 = (pl.program_id(0) % (block // window)) * window
    r_idx = lax.broadcasted_iota(I32, (window, 1), 0)
    t_idx = w0 + r_idx
    s_idx = lax.broadcasted_iota(I32, (1, block), 1)
    same = jnp.right_shift(t_idx, shift) == jnp.right_shift(s_idx, shift)
    mask = same & (s_idx <= t_idx)
    mask_t = same & (t_idx <= s_idx)
    r_seq = jnp.right_shift(r_idx, shift)
    l_seq = jnp.right_shift(s_idx - w0, shift)
    gcol = gcol_ref[...]
    grow = grow_ref[...]
    mtok = mtok_ref[...]
    neg_inf = jnp.float32(-jnp.inf)
    for h in range(N_HEADS):
        sl = slice(h * HEAD_DIM, (h + 1) * HEAD_DIM)
        q = q_ref[:, sl]
        k = k_ref[:, sl]
        kt = kt_ref[sl, :]
        v = v_ref[:, sl]
        li_r = grow[h:h + 1, :]
        lf_r = grow[N_HEADS + h:N_HEADS + h + 1, :]
        li_c = gcol[:, h:h + 1]
        lf_c = gcol[:, N_HEADS + h:N_HEADS + h + 1]
        m_c = mtok[:, h:h + 1]
        b_c = jnp.sum(jnp.where(mask, lf_r, 0.0), axis=1, keepdims=True)
        b_r = jnp.sum(jnp.where(mask_t, lf_c, 0.0), axis=0, keepdims=True)
        dmat = jnp.where(mask, b_c - b_r + li_r, neg_inf)
        inter = b_c + m_c
        m_t = jnp.maximum(inter, jnp.max(dmat, axis=1, keepdims=True))
        s_inter = jnp.exp(inter - m_t)
        s = jnp.dot(q, kt, preferred_element_type=F32) * jnp.exp(dmat - m_t)
        intra = jnp.dot(s.astype(BF16), v, preferred_element_type=F32)
        qf = q.astype(F32)
        qc = qn = None
        for j in range(n_seq):
            qc_j = jnp.dot(q, c0_ref[j, h].astype(BF16), preferred_element_type=F32)
            qn_j = jnp.sum(qf * n0_ref[j, h:h + 1, :], axis=1, keepdims=True)
            qc = qc_j if j == 0 else jnp.where(r_seq == j, qc_j, qc)
            qn = qn_j if j == 0 else jnp.where(r_seq == j, qn_j, qn)
        num = s_inter * qc + intra
        den = s_inter * qn + jnp.sum(s, axis=1, keepdims=True)
        hval = num / jnp.maximum(jnp.abs(den), jnp.exp(-m_t))
        hg_ref[:, sl] = _head_out(hval, o_ref[:, sl], gain_ref[:, sl])
        kf = k.astype(F32)
        ktf = kt.astype(F32)
        for j in range(n_seq):
            lsel = l_seq == j
            m_j = mtok[j * seq_len:j * seq_len + 1, h:h + 1]
            b_last = jnp.sum(jnp.where(lsel, lf_r, 0.0), axis=1, keepdims=True)
            d_r = jnp.where(lsel, b_last - b_r + li_r, neg_inf)
            d_c = jnp.where(r_seq == j, b_last - b_c + li_c, neg_inf)
            m_new = jnp.maximum(b_last + m_j, jnp.max(d_r, axis=1, keepdims=True))
            s_last = jnp.exp(b_last + m_j - m_new)
            kwt = (ktf * jnp.exp(d_r - m_new)).astype(BF16)
            c1_ref[j, h] = s_last * c0_ref[j, h] + jnp.dot(kwt, v, preferred_element_type=F32)
            n1_ref[j, h:h + 1, :] = (s_last * n0_ref[j, h:h + 1, :]
                                     + jnp.sum(kf * jnp.exp(d_c - m_new), axis=0, keepdims=True))
            m1_ref[j, h:h + 1, :] = jnp.broadcast_to(m_new, (1, LANES))


def _chain(c_all):
    if c_all is None:
        return [], []
    return [c_all], [pl.BlockSpec(memory_space=pl.ANY)]


def _mlstm_prompt(q, k, v, gcol, grow, o, gain, c_all, layer, n_seq, seq_len):
    n = n_seq * seq_len
    chunk = math.gcd(seq_len, MLSTM_PROMPT_CHUNK)
    nc = seq_len // chunk
    row = lambda w: pl.BlockSpec((chunk, w), lambda b, c: (b * nc + c, 0))
    st = lambda *tail: pl.BlockSpec((1,) + tail, lambda b, c: (b,) + (0,) * len(tail))
    extra, extra_specs = _chain(c_all)
    n_in = 7
    return pl.pallas_call(
        functools.partial(_mlstm_prompt_kernel, rows=chunk, strip=min(chunk, MLSTM_STRIP), chained=bool(extra)),
        grid=(n_seq, nc),
        in_specs=[row(D_MLSTM), row(D_MLSTM), row(D_MLSTM), row(LANES),
                  pl.BlockSpec((GATE_ROWS, chunk), lambda b, c: (0, b * nc + c)),
                  row(D_MLSTM), _layer_spec(gain, layer)] + extra_specs,
        out_specs=(row(D_MLSTM),
                   pl.BlockSpec((None, 1, N_HEADS, HEAD_DIM, HEAD_DIM), lambda b, c: (layer, b, 0, 0, 0)),
                   st(N_HEADS, HEAD_DIM, LANES), st(N_HEADS, LANES)),
        out_shape=(jax.ShapeDtypeStruct((n, D_MLSTM), BF16),
                   jax.ShapeDtypeStruct((DEPTH, n_seq, N_HEADS, HEAD_DIM, HEAD_DIM), F32),
                   jax.ShapeDtypeStruct((n_seq, N_HEADS, HEAD_DIM, LANES), F32),
                   jax.ShapeDtypeStruct((n_seq, N_HEADS, LANES), F32)),
        input_output_aliases={n_in: 1} if extra else {},
        compiler_params=_params(("parallel", "arbitrary")),
        name="mlstm_prompt",
    )(q, k, v, gcol, grow, o, gain, *extra)


def _mlstm_sample(q, k, kt, v, gcol, grow, mtok, o, gain, c0, n0, c_all, layer, first_row, n_seq, seq_len):
    n = n_seq * seq_len
    window, block = SAMPLE_WINDOW, SAMPLE_BLOCK_ROWS
    assert n % block == 0 and first_row % block == 0 and window % seq_len == 0
    per_win = window // seq_len
    sub = block // window
    row = lambda w: pl.BlockSpec((window, w), lambda i: (first_row // window + i, 0))
    extra, extra_specs = _chain(c_all)
    n_in = 11
    return pl.pallas_call(
        functools.partial(_mlstm_sample_kernel, seq_len=seq_len, window=window, block=block, chained=bool(extra)),
        grid=(n // window,),
        in_specs=[row(D_MLSTM), row(D_MLSTM),
                  pl.BlockSpec((D_MLSTM, block), lambda i: (0, i // sub)),
                  pl.BlockSpec((block, D_MLSTM), lambda i: (first_row // block + i // sub, 0)),
                  row(LANES),
                  pl.BlockSpec((GATE_ROWS, block), lambda i: (0, first_row // block + i // sub)),
                  pl.BlockSpec((window, LANES), lambda i: (i, 0)),
                  row(D_MLSTM), _layer_spec(gain, layer),
                  pl.BlockSpec((None, per_win, N_HEADS, HEAD_DIM, HEAD_DIM), lambda i: (layer, i, 0, 0, 0)),
                  pl.BlockSpec((None, per_win, N_HEADS, HEAD_DIM), lambda i: (layer, i, 0, 0))] + extra_specs,
        out_specs=(pl.BlockSpec((window, D_MLSTM), lambda i: (i, 0)),
                   pl.BlockSpec((None, per_win, N_HEADS, HEAD_DIM, HEAD_DIM), lambda i: (layer, i, 0, 0, 0)),
                   pl.BlockSpec((per_win, N_HEADS, HEAD_DIM), lambda i: (i, 0, 0)),
                   pl.BlockSpec((per_win, N_HEADS, LANES), lambda i: (i, 0, 0))),
        out_shape=(jax.ShapeDtypeStruct((n, D_MLSTM), BF16),
                   jax.ShapeDtypeStruct((DEPTH, n_seq, N_HEADS, HEAD_DIM, HEAD_DIM), F32),
                   jax.ShapeDtypeStruct((n_seq, N_HEADS, HEAD_DIM), F32),
                   jax.ShapeDtypeStruct((n_seq, N_HEADS, LANES), F32)),
        input_output_aliases={n_in: 1} if extra else {},
        compiler_params=_params(("parallel",)),
        name="mlstm_sample",
    )(q, k, kt, v, gcol, grow, mtok, o, gain, c0, n0, *extra)


def _pool_prompt_kernel(u_ref, prev_ref, out_ref, tail_ref, *, tm, tiles_per_seq):
    tile = pl.program_id(0) % tiles_per_seq
    head = 16
    tail_ref[...] = u_ref[tm - head:tm, :]
    pos = tile * tm + lax.broadcasted_iota(I32, (tm, 1), 0)
    for g, w in enumerate(POOL_WINDOWS):
        sl = slice(g * POOL_GROUP_DIM, (g + 1) * POOL_GROUP_DIM)
        acc = jnp.concatenate([jnp.where(tile == 0, 0.0, prev_ref[:, sl]), u_ref[:, sl]], axis=0)
        span = 1
        while span < w:
            acc = acc + pltpu.roll(acc, span, 0)
            span *= 2
        cnt = jnp.minimum(pos + 1, w).astype(F32)
        out_ref[:, sl] = (acc[head:] / cnt - u_ref[:, sl]).astype(BF16)


def _pool_prompt(u, n_seq, seq_len):
    n = n_seq * seq_len
    tm = _row_tile(seq_len)
    head = 16
    return pl.pallas_call(
        functools.partial(_pool_prompt_kernel, tm=tm, tiles_per_seq=seq_len // tm),
        grid=(n // tm,),
        in_specs=[pl.BlockSpec((tm, D_POOL), lambda i: (i, 0)),
                  pl.BlockSpec((head, D_POOL), lambda i: (jnp.maximum(i * (tm // head) - 1, 0), 0))],
        out_specs=(pl.BlockSpec((tm, D_POOL), lambda i: (i, 0)),
                   pl.BlockSpec((head, D_POOL), lambda i: (i // (seq_len // tm), 0))),
        out_shape=(jax.ShapeDtypeStruct((n, D_POOL), BF16), jax.ShapeDtypeStruct((n_seq * head, D_POOL), F32)),
        compiler_params=_params(("arbitrary",)),
        name="pool_prompt",
    )(u, u)


def _pool_sample_kernel(u_ref, buf_ref, out_ref, nbuf_ref, ext_ref, *, seq_len, start):
    ext_ref[:, 0:POOL_BUF, :] = buf_ref[...]
    ext_ref[:, POOL_BUF:POOL_BUF + seq_len, :] = u_ref[...]
    pos = start + lax.broadcasted_iota(I32, (1, seq_len, 1), 1)
    for g, w in enumerate(POOL_WINDOWS):
        sl = slice(g * POOL_GROUP_DIM, (g + 1) * POOL_GROUP_DIM)
        acc = ext_ref[:, POOL_BUF:POOL_BUF + seq_len, sl]
        for d in range(1, w):
            acc = acc + ext_ref[:, POOL_BUF - d:POOL_BUF - d + seq_len, sl]
        cnt = jnp.minimum(pos + 1, w).astype(F32)
        out_ref[:, :, sl] = (acc / cnt - u_ref[:, :, sl]).astype(BF16)
    nbuf_ref[...] = ext_ref[:, seq_len:seq_len + POOL_BUF, :]


def _pool_sample(u3, buf, layer, start):
    n_seq, seq_len, _ = u3.shape
    bs = _row_tile(n_seq, 32)
    spec = lambda r: pl.BlockSpec((bs, r, D_POOL), lambda i: (i, 0, 0))
    return pl.pallas_call(
        functools.partial(_pool_sample_kernel, seq_len=seq_len, start=start),
        grid=(n_seq // bs,),
        in_specs=[spec(seq_len), pl.BlockSpec((None, bs, POOL_BUF, D_POOL), lambda i: (layer, i, 0, 0))],
        out_specs=(spec(seq_len), spec(POOL_BUF)),
        out_shape=(jax.ShapeDtypeStruct((n_seq, seq_len, D_POOL), BF16),
                   jax.ShapeDtypeStruct((n_seq, POOL_BUF, D_POOL), F32)),
        scratch_shapes=[pltpu.VMEM((bs, POOL_BUF + seq_len + 5, D_POOL), F32)],
        compiler_params=_params(("parallel",)),
        name="pool_sample",
    )(u3, buf)


def _layer_norm(y, g, b):
    mu = jnp.mean(y, axis=1, keepdims=True)
    yc = y - mu
    var = jnp.mean(yc * yc, axis=1, keepdims=True)
    return yc * lax.rsqrt(var + LN_EPS) * g + b


def _route(logits_t):
    tokens = logits_t.shape[1]
    grp = lax.broadcasted_iota(I32, (8, tokens), 0)
    live = grp < N_EXPERT_GROUPS
    neg_inf = jnp.float32(-jnp.inf)
    lm = [jnp.where(live, logits_t[8 * m:8 * m + 8, :], neg_inf) for m in range(EXPERTS_PER_GROUP)]
    mx = jnp.max(jnp.maximum(jnp.maximum(lm[0], lm[1]), jnp.maximum(lm[2], lm[3])), axis=0, keepdims=True)
    ex = [jnp.exp(l - mx) for l in lm]
    tot = jnp.sum(ex[0] + ex[1] + ex[2] + ex[3], axis=0, keepdims=True)
    p = [e / tot for e in ex]
    top1 = jnp.maximum(jnp.maximum(p[0], p[1]), jnp.maximum(p[2], p[3]))
    i1 = jnp.where(p[0] == top1, 0, jnp.where(p[1] == top1, 1, jnp.where(p[2] == top1, 2, 3)))
    r = [jnp.where(i1 == m, -1.0, p[m]) for m in range(EXPERTS_PER_GROUP)]
    top2 = jnp.maximum(jnp.maximum(r[0], r[1]), jnp.maximum(r[2], r[3]))
    i2 = jnp.where(r[0] == top2, 0, jnp.where(r[1] == top2, 1, jnp.where(r[2] == top2, 2, 3)))
    gscore = jnp.where(live, top1 + top2, neg_inf)
    gmax = jnp.max(gscore, axis=0, keepdims=True)
    gsel = jnp.min(jnp.where(gscore == gmax, grp, 8), axis=0, keepdims=True)
    chosen = grp == gsel
    tsum = top1 + top2
    w1 = top1 / tsum
    w2 = top2 / tsum
    first_is_lo = i1 < i2
    lo = jnp.minimum(i1, i2)
    hi = jnp.maximum(i1, i2)
    pair = jnp.where(lo == 0, hi - 1, jnp.where(lo == 1, hi + 1, 5))
    pick = lambda a: jnp.sum(jnp.where(chosen, a, jnp.zeros_like(a)), axis=0, keepdims=True)
    cls = pick(grp * len(PAIRS) + pair)
    w_lo = pick(jnp.where(first_is_lo, w1, w2))
    w_hi = pick(jnp.where(first_is_lo, w2, w1))
    return cls, w_lo, w_hi


def _mix_kernel(hgp_ref, hgs_ref, plp_ref, pls_ref, ga_ref, gb_ref, x_ref, wpool_ref, pscale_ref, wa_ref, wb_ref,
                wout_ref, g1_ref, b1_ref, wr_ref, br_ref,
                dest_ref, cnt_ref, xs_ref,
                rows0_ref, rows1_ref, dvm_ref, dsm0_ref, dsm1_ref, carry_ref, row_sem, idx_sem,
                *, tm, n_prompt_tiles, n_tiles, capacity):
    i = pl.program_id(0)
    rows = (rows0_ref, rows1_ref)
    dsm = (dsm0_ref, dsm1_ref)
    spare = N_CLASSES * capacity
    sent = lambda s: pltpu.make_async_copy(rows[s], xs_ref.at[pl.ds(0, tm), :], row_sem.at[s])

    def send(s, unrolled):
        def one(r, carry=0):
            pltpu.make_async_copy(rows[s].at[pl.ds(r, 1), :], xs_ref.at[pl.ds(dsm[s][r], 1), :],
                                  row_sem.at[s]).start()
            return carry
        if unrolled:
            for r in range(tm):
                one(r)
        else:
            lax.fori_loop(0, tm, one, 0, unroll=8)

    @pl.when(i == 0)
    def _():
        carry_ref[...] = jnp.zeros_like(carry_ref)
        rows0_ref[...] = jnp.zeros_like(rows0_ref)
        rows1_ref[...] = jnp.zeros_like(rows1_ref)

        def spare_rows(r, carry):
            dsm0_ref[r] = spare + r
            return carry

        lax.fori_loop(0, tm, spare_rows, 0)
        send(0, False)
        dvm_ref[...] = spare + tm + lax.broadcasted_iota(I32, (8, tm), 1)
        pltpu.make_async_copy(dvm_ref.at[0], dsm1_ref, idx_sem).start()

    for parity in (0, 1):
        pl.when(i % 2 == parity)(functools.partial(
            _mix_step, parity, i, hgp_ref, hgs_ref, plp_ref, pls_ref, ga_ref, gb_ref, x_ref, wpool_ref, pscale_ref,
            wa_ref, wb_ref, wout_ref, g1_ref, b1_ref, wr_ref, br_ref, dest_ref, cnt_ref, rows, dvm_ref, dsm,
            carry_ref, idx_sem, send, sent, tm, n_prompt_tiles, n_tiles, capacity))


def _mix_step(cur, i, hgp_ref, hgs_ref, plp_ref, pls_ref, ga_ref, gb_ref, x_ref, wpool_ref, pscale_ref, wa_ref,
              wb_ref, wout_ref, g1_ref, b1_ref, wr_ref, br_ref, dest_ref, cnt_ref, rows, dvm_ref, dsm, carry_ref,
              idx_sem, send, sent, tm, n_prompt_tiles, n_tiles, capacity):
    slots_arrived = lambda s: pltpu.make_async_copy(dvm_ref.at[0], dsm[s], idx_sem)
    slots_arrived(1 - cur).wait()
    send(1 - cur, True)
    is_prompt = i < n_prompt_tiles
    hg = jnp.where(is_prompt, hgp_ref[...], hgs_ref[...])
    pooled = jnp.where(is_prompt, plp_ref[...], pls_ref[...])
    ya = jnp.dot(hg, wa_ref[...], preferred_element_type=F32)
    parts = []
    for g in range(len(POOL_WINDOWS)):
        sl = slice(g * POOL_GROUP_DIM, (g + 1) * POOL_GROUP_DIM)
        parts.append(jnp.dot(pooled[:, sl], wpool_ref[g], preferred_element_type=F32))
    pl_lin = jnp.concatenate(parts, axis=1) * pscale_ref[...]
    yb = jnp.dot(pl_lin.astype(BF16), wb_ref[...], preferred_element_type=F32)
    mix = jax.nn.sigmoid(ga_ref[...]) * ya + jax.nn.sigmoid(gb_ref[...]) * yb
    res = jnp.dot(mix.astype(BF16), wout_ref[...], preferred_element_type=F32)
    x1 = _layer_norm(ALPHA * x_ref[...] + res, g1_ref[...], b1_ref[...])

    nt = lambda a, b: lax.dot_general(a, b, (((1,), (1,)), ((), ())), preferred_element_type=F32)
    wr = wr_ref[...]
    wr_hi = wr.astype(BF16)
    wr_lo = (wr - wr_hi.astype(F32)).astype(BF16)
    x1_hi = x1.astype(BF16)
    x1_lo = (x1 - x1_hi.astype(F32)).astype(BF16)
    logits_t = nt(wr_hi, x1_hi) + (nt(wr_hi, x1_lo) + nt(wr_lo, x1_hi)) + br_ref[...]
    cls, w_lo, w_hi = _route(logits_t)

    onehot = lax.broadcasted_iota(I32, (CLASS_ROWS, tm), 0) == cls
    earlier = lax.broadcasted_iota(I32, (tm, tm), 0) < lax.broadcasted_iota(I32, (tm, tm), 1)
    before = jnp.dot(jnp.where(onehot, 1.0, 0.0).astype(BF16), jnp.where(earlier, 1.0, 0.0).astype(BF16),
                     preferred_element_type=F32)
    seen = carry_ref[:, 0:1]
    rank = jnp.sum(jnp.where(onehot, before + seen, 0.0), axis=0, keepdims=True)
    carry_ref[...] = carry_ref[...] + jnp.sum(jnp.where(onehot, 1.0, 0.0), axis=1, keepdims=True)
    cnt_ref[...] = carry_ref[...]
    dest = cls * capacity + rank.astype(I32)
    dest_ref[...] = jnp.broadcast_to(dest, (8, tm))

    sent(cur).wait()
    wrows = jnp.concatenate([w_lo, w_hi, jnp.zeros((LANES - 2, tm), F32)], axis=0)
    rows[cur][:, 0:D_MODEL] = x1
    rows[cur][:, D_MODEL:ROW_EXT] = wrows.T
    dvm_ref[...] = jnp.broadcast_to(dest, (8, tm))
    slots_arrived(cur).start()

    @pl.when(i == n_tiles - 1)
    def _():
        slots_arrived(cur).wait()
        send(cur, False)
        sent(cur).wait()
        sent(1 - cur).wait()


def _mix(hg_p, hg_s, pooled_p, pooled_s, ga, gb, x, w, layer, capacity):
    n = x.shape[0]
    n_p = hg_p.shape[0]
    tm = _row_tile(math.gcd(n_p, n - n_p))
    ntp = n_p // tm
    row = lambda w: pl.BlockSpec((tm, w), lambda i: (i, 0))
    prow = lambda w: pl.BlockSpec((tm, w), lambda i: (jnp.minimum(i, ntp - 1), 0))
    srow = lambda w: pl.BlockSpec((tm, w), lambda i: (jnp.maximum(i - ntp, 0), 0))
    per_layer = (w["w_pool"], w["pool_scale"], w["w_proj_a"], w["w_proj_b"], w["w_out"], w["ln1_g"], w["ln1_b"])
    shared = (w["w_router_t"], w["b_router_col"])
    consts = per_layer + shared
    return pl.pallas_call(
        functools.partial(_mix_kernel, tm=tm, n_prompt_tiles=ntp, n_tiles=n // tm, capacity=capacity),
        grid=(n // tm,),
        in_specs=[prow(D_MLSTM), srow(D_MLSTM), prow(D_POOL), srow(D_POOL), row(D_MODEL), row(D_MODEL), row(D_MODEL)]
                 + [_layer_spec(c, layer) for c in per_layer] + [_const_spec(c.shape) for c in shared],
        out_specs=(pl.BlockSpec((8, tm), lambda i: (0, i)),
                   pl.BlockSpec((CLASS_ROWS, LANES), lambda i: (0, 0)),
                   pl.BlockSpec(memory_space=pl.ANY)),
        out_shape=(jax.ShapeDtypeStruct((8, n), I32),
                   jax.ShapeDtypeStruct((CLASS_ROWS, LANES), F32),
                   jax.ShapeDtypeStruct((N_CLASSES * capacity + 2 * tm, ROW_EXT), F32)),
        scratch_shapes=[pltpu.VMEM((tm, ROW_EXT), F32), pltpu.VMEM((tm, ROW_EXT), F32), pltpu.VMEM((8, tm), I32),
                        pltpu.SMEM((tm,), I32), pltpu.SMEM((tm,), I32), pltpu.VMEM((CLASS_ROWS, LANES), F32),
                        pltpu.SemaphoreType.DMA((2,)), pltpu.SemaphoreType.DMA],
        compiler_params=_params(("arbitrary",)),
        name="mix",
    )(hg_p, hg_s, pooled_p, pooled_s, ga, gb, x, *consts)


def _moe_kernel(blk_ref, elo_ref, ehi_ref, nvalid_ref, ntiles_ref,
                xs_ref, wg_lo, wu_lo, wd_lo, wg_hi, wu_hi, wd_hi, g2_ref, b2_ref, ys_ref):
    i = pl.program_id(0)

    @pl.when(i < ntiles_ref[0])
    def _():
        valid = lax.broadcasted_iota(I32, (MOE_TILE, 1), 0) < nvalid_ref[i]
        xe = xs_ref[...]
        x = jnp.where(valid, xe[:, 0:D_MODEL], 0.0)
        w_lo = jnp.where(valid, xe[:, D_MODEL:D_MODEL + 1], 0.0)
        w_hi = jnp.where(valid, xe[:, D_MODEL + 1:D_MODEL + 2], 0.0)
        xb = x.astype(BF16)

        def expert(wg, wu, wd, w):
            g = jnp.dot(xb, wg[0], preferred_element_type=F32)
            u = jnp.dot(xb, wu[0], preferred_element_type=F32)
            hid = (g * jax.nn.sigmoid(g)) * u * w
            return jnp.dot(hid.astype(BF16), wd[0], preferred_element_type=F32)

        y = expert(wg_lo, wu_lo, wd_lo, w_lo) + expert(wg_hi, wu_hi, wd_hi, w_hi)
        ys_ref[...] = _layer_norm(ALPHA * x + y, g2_ref[...], b2_ref[...])


def _moe(xs, tables, w, layer, max_tiles):
    blk, e_lo, e_hi, n_valid, n_tiles = tables
    up = lambda sel: pl.BlockSpec((None, 1, D_MODEL, D_EXPERT),
                                  lambda i, b, lo, hi, nv, nt: (layer, (lo, hi)[sel][i], 0, 0))
    down = lambda sel: pl.BlockSpec((None, 1, D_EXPERT, D_MODEL),
                                    lambda i, b, lo, hi, nv, nt: (layer, (lo, hi)[sel][i], 0, 0))
    grid_spec = pltpu.PrefetchScalarGridSpec(
        num_scalar_prefetch=5,
        grid=(max_tiles,),
        in_specs=[pl.BlockSpec((MOE_TILE, ROW_EXT), lambda i, b, *_: (b[i], 0)),
                  up(0), up(0), down(0), up(1), up(1), down(1),
                  _layer_spec(w["ln2_g"], layer), _layer_spec(w["ln2_b"], layer)],
        out_specs=pl.BlockSpec((MOE_TILE, D_MODEL), lambda i, b, *_: (b[i], 0)))
    return pl.pallas_call(
        _moe_kernel,
        grid_spec=grid_spec,
        out_shape=jax.ShapeDtypeStruct((xs.shape[0], D_MODEL), F32),
        compiler_params=_params(("arbitrary",)),
        name="moe",
    )(blk, e_lo, e_hi, n_valid, n_tiles, xs, w["w_e_gate"], w["w_e_up"], w["w_e_down"],
      w["w_e_gate"], w["w_e_up"], w["w_e_down"], w["ln2_g"], w["ln2_b"])


def _tile_tables(counts, capacity, max_tiles):
    cnt = counts[:N_CLASSES, 0].astype(I32)
    tiles = (cnt + MOE_TILE - 1) // MOE_TILE
    ends = jnp.cumsum(tiles)
    starts = ends - tiles
    n_tiles = ends[-1]
    t = jnp.minimum(jnp.arange(max_tiles, dtype=I32), jnp.maximum(n_tiles - 1, 0))
    cls = jnp.sum((ends[None, :] <= t[:, None]).astype(I32), axis=1)
    onehot = (jnp.arange(N_CLASSES, dtype=I32)[None, :] == cls[:, None]).astype(I32)
    within = t - jnp.sum(onehot * starts[None, :], axis=1)
    n_valid = jnp.clip(jnp.sum(onehot * cnt[None, :], axis=1) - within * MOE_TILE, 0, MOE_TILE)
    blk = cls * (capacity // MOE_TILE) + within
    grp = cls // len(PAIRS)
    pair = cls % len(PAIRS)
    lo = jnp.where(pair < 3, 0, jnp.where(pair < 5, 1, 2))
    hi = jnp.where(pair < 3, pair + 1, jnp.where(pair < 5, pair - 1, 3))
    return (blk, grp * EXPERTS_PER_GROUP + lo, grp * EXPERTS_PER_GROUP + hi, n_valid,
            n_tiles.reshape(1).astype(I32))


def _unpermute_kernel(dest_ref, ys_ref, prompt_ref, sample_ref, buf_ref, sem, *, tm, n_prompt_tiles, n_tiles):
    i = pl.program_id(0)
    cur = i % 2

    def request(tile, slot):
        def fetch(r, carry):
            pltpu.make_async_copy(ys_ref.at[pl.ds(dest_ref[tile * tm + r], 1), :],
                                  buf_ref.at[slot, pl.ds(r, 1), :], sem.at[slot]).start()
            return carry

        lax.fori_loop(0, tm, fetch, 0, unroll=8)

    arrived = lambda slot: pltpu.make_async_copy(ys_ref.at[pl.ds(0, tm), :], buf_ref.at[slot], sem.at[slot])

    @pl.when(i == 0)
    def _():
        request(0, 0)

    @pl.when(i + 1 < n_tiles)
    def _():
        request(i + 1, 1 - cur)

    arrived(cur).wait()

    @pl.when(i < n_prompt_tiles)
    def _():
        prompt_ref[...] = buf_ref[cur]

    @pl.when(i >= n_prompt_tiles)
    def _():
        sample_ref[...] = buf_ref[cur]


def _unpermute(ys, dest, n_prompt):
    n = dest.shape[0]
    tm = _row_tile(math.gcd(n_prompt, n - n_prompt))
    ntp = n_prompt // tm
    grid_spec = pltpu.PrefetchScalarGridSpec(
        num_scalar_prefetch=1,
        grid=(n // tm,),
        in_specs=[pl.BlockSpec(memory_space=pl.ANY)],
        out_specs=(pl.BlockSpec((tm, D_MODEL), lambda i, d: (jnp.minimum(i, ntp - 1), 0)),
                   pl.BlockSpec((tm, D_MODEL), lambda i, d: (jnp.maximum(i - ntp, 0), 0))),
        scratch_shapes=[pltpu.VMEM((2, tm, D_MODEL), F32), pltpu.SemaphoreType.DMA((2,))])
    return pl.pallas_call(
        functools.partial(_unpermute_kernel, tm=tm, n_prompt_tiles=ntp, n_tiles=n // tm),
        grid_spec=grid_spec,
        out_shape=(jax.ShapeDtypeStruct((n_prompt, D_MODEL), F32),
                   jax.ShapeDtypeStruct((n - n_prompt, D_MODEL), F32)),
        compiler_params=_params(("arbitrary",)),
        name="unpermute",
    )(dest, ys)


def _prepare_weights(w_in, b_gate, hn_gain, w_pool, pool_scale, w_proj_a, w_proj_b, w_out, ln1_g, ln1_b,
                     ln2_g, ln2_b, w_router, b_router, w_e_gate, w_e_up, w_e_down):
    w_qkvo = w_in[:, :, :_G0].astype(BF16)
    w_rest = w_in[:, :, _U0:].astype(BF16)
    w_gate = w_in[:, :, _G0:_U0]
    w_gate_p = jnp.pad(w_gate, ((0, 0), (0, 0), (0, LANES - 2 * N_HEADS))).astype(BF16)
    w_gate_t = jnp.pad(jnp.swapaxes(w_gate, 1, 2), ((0, 0), (0, GATE_ROWS - 2 * N_HEADS), (0, 0))).astype(BF16)
    w_key_t = jnp.swapaxes(w_in[:, :, D_MLSTM:2 * D_MLSTM], 1, 2).astype(BF16)
    b_row = jnp.pad(b_gate, ((0, 0), (0, LANES - 2 * N_HEADS)))[:, None, :]
    b_col = jnp.pad(b_gate, ((0, 0), (0, GATE_ROWS - 2 * N_HEADS)))[:, :, None]
    wr = w_router.T.reshape(N_EXPERT_GROUPS, EXPERTS_PER_GROUP, D_MODEL).swapaxes(0, 1)
    wr = jnp.pad(wr, ((0, 0), (0, 8 - N_EXPERT_GROUPS), (0, 0))).reshape(ROUTER_ROWS, D_MODEL)
    br = b_router.reshape(N_EXPERT_GROUPS, EXPERTS_PER_GROUP).T
    br = jnp.pad(br, ((0, 0), (0, 8 - N_EXPERT_GROUPS))).reshape(ROUTER_ROWS, 1)
    per_row = lambda a: a.reshape(DEPTH, 1, -1)
    return dict(
        w_qkvo=w_qkvo, w_rest=w_rest, w_gate=w_gate_p, w_gate_t=w_gate_t, w_key_t=w_key_t, b_row=b_row, b_col=b_col,
        gain=per_row(hn_gain), w_pool=w_pool.astype(BF16), pool_scale=per_row(pool_scale),
        w_proj_a=w_proj_a.astype(BF16), w_proj_b=w_proj_b.astype(BF16), w_out=w_out.astype(BF16),
        ln1_g=per_row(ln1_g), ln1_b=per_row(ln1_b), ln2_g=per_row(ln2_g), ln2_b=per_row(ln2_b),
        w_router_t=wr, b_router_col=br,
        w_e_gate=w_e_gate.astype(BF16), w_e_up=w_e_up.astype(BF16), w_e_down=w_e_down.astype(BF16))


def kernel(x_prompt, x_sample, state_C, state_n, state_m, state_pool, w_in, b_gate, hn_gain, w_pool, pool_scale,
           w_proj_a, w_proj_b, w_out, ln1_g, ln1_b, ln2_g, ln2_b, w_router, b_router, w_e_gate, w_e_up, w_e_down):
    w = _prepare_weights(w_in, b_gate, hn_gain, w_pool, pool_scale, w_proj_a, w_proj_b, w_out, ln1_g, ln1_b,
                         ln2_g, ln2_b, w_router, b_router, w_e_gate, w_e_up, w_e_down)
    n_pseq, p_len, _ = x_prompt.shape
    n_sseq, s_len, _ = x_sample.shape
    n_p = n_pseq * p_len
    n_s = n_sseq * s_len
    n = n_p + n_s
    capacity = -(-n // MOE_TILE) * MOE_TILE
    max_tiles = n // MOE_TILE + N_CLASSES
    np_, mp, bp, ns, ms, bs = [], [], [], [], [], []
    c_p = c_s = None
    ys = (x_prompt.reshape(n_p, D_MODEL), x_sample.reshape(n_s, D_MODEL))
    dest = None
    for l in range(DEPTH):
        q, k, v, o, u, ga, gb, gcol, grow, kt, x = _inproj(ys, w, l, n_p, None if l == 0 else dest[0])
        hg_p, c_p, n1p, m1p = _mlstm_prompt(q, k, v, gcol, grow, o, w["gain"], c_p, l, n_pseq, p_len)
        mtok = jnp.pad(jnp.repeat(state_m[l], s_len, axis=0), ((0, 0), (0, LANES - N_HEADS)))
        hg_s, c_s, n1s, m1s = _mlstm_sample(q, k, kt, v, gcol, grow, mtok, o, w["gain"], state_C, state_n, c_s,
                                            l, n_p, n_sseq, s_len)
        pooled_p, tail = _pool_prompt(u, n_pseq, p_len)
        pooled_s, nbuf = _pool_sample(u[n_p:].reshape(n_sseq, s_len, D_POOL), state_pool, l, PAST_LEN)
        dest, counts, xs = _mix(hg_p, hg_s, pooled_p, pooled_s.reshape(n_s, D_POOL), ga, gb, x, w, l, capacity)
        ys = _moe(xs, _tile_tables(counts, capacity, max_tiles), w, l, max_tiles)
        np_.append(n1p[:, :, :, 0])
        mp.append(m1p[:, :, 0])
        bp.append(tail.reshape(n_pseq, -1, D_POOL)[:, -POOL_BUF:])
        ns.append(n1s)
        ms.append(m1s[:, :, 0])
        bs.append(nbuf)
    y_p, y_s = _unpermute(ys, dest[0], n_p)
    st = jnp.stack
    return (y_p.reshape(n_pseq, p_len, D_MODEL), y_s.reshape(n_sseq, s_len, D_MODEL),
            c_p, st(np_), st(mp), st(bp), c_s, st(ns), st(ms), st(bs))
```

```python
import functools
import math

import jax
import jax.numpy as jnp
from jax import lax
from jax.experimental import pallas as pl
from jax.experimental.pallas import tpu as pltpu

F32 = jnp.float32
BF16 = jnp.bfloat16
I32 = jnp.int32

D_MODEL = 1024
N_HEADS = 4
HEAD_DIM = 256
D_MLSTM = N_HEADS * HEAD_DIM
POOL_WINDOWS = (2, 4, 8, 16)
POOL_GROUP_DIM = 128
D_POOL = len(POOL_WINDOWS) * POOL_GROUP_DIM
POOL_BUF = 15
N_EXPERTS = 16
N_EXPERT_GROUPS = 4
EXPERTS_PER_GROUP = 4
D_EXPERT = 512
DEPTH = 4
PAST_LEN = 16384
ALPHA = (2 * DEPTH) ** 0.25
LN_EPS = 1e-5
K_SCALE = HEAD_DIM ** -0.5

LANES = 128
GATE_ROWS = 16
ROUTER_ROWS = 32
VMEM_LIMIT = 52 * 1024 * 1024
MLSTM_PROMPT_CHUNK = 512
MLSTM_STRIP = 256
SAMPLE_BLOCK_ROWS = 128
SAMPLE_WINDOW = 16

PAIRS = ((0, 1), (0, 2), (0, 3), (1, 2), (1, 3), (2, 3))
N_CLASSES = N_EXPERT_GROUPS * len(PAIRS)
CLASS_ROWS = 32
MOE_TILE = 256
ROW_EXT = D_MODEL + LANES

_G0 = 4 * D_MLSTM
_U0 = _G0 + 2 * N_HEADS
_SEGS = ((0, 0, 1024), (0, 1024, 2048), (0, 2048, 3072), (0, 3072, 4096), (1, 0, 512), (1, 512, 1536), (1, 1536, 2560))


def _params(sem, **kw):
    return pltpu.CompilerParams(dimension_semantics=sem, vmem_limit_bytes=VMEM_LIMIT, **kw)


def _const_spec(shape):
    nd = len(shape)
    return pl.BlockSpec(shape, lambda *_: (0,) * nd, pipeline_mode=pl.Buffered(1))


def _layer_spec(stacked, layer):
    tail = stacked.shape[1:]
    return pl.BlockSpec((None,) + tail, lambda *_: (layer,) + (0,) * len(tail), pipeline_mode=pl.Buffered(1))


def _row_tile(n, cap=512):
    t = cap
    while n % t:
        t //= 2
    return t


def _inproj_kernel(xp_ref, xs_ref, wa_ref, wb_ref, wg_ref, wgt_ref, wkt_ref, brow_ref, bcol_ref,
                   q_ref, k_ref, v_ref, o_ref, u_ref, ga_ref, gb_ref, gcol_ref, grow_ref, kt_ref, x_ref,
                   *, n_prompt_tiles):
    x32 = jnp.where(pl.program_id(0) < n_prompt_tiles, xp_ref[...], xs_ref[...])
    x_ref[...] = x32
    _inproj_body(x32.astype(BF16), wa_ref, wb_ref, wg_ref, wgt_ref, wkt_ref, brow_ref, bcol_ref,
                 q_ref, k_ref, v_ref, o_ref, u_ref, ga_ref, gb_ref, gcol_ref, grow_ref, kt_ref, n_prompt_tiles)


def _inproj_gather_kernel(dest_ref, ys_ref, wa_ref, wb_ref, wg_ref, wgt_ref, wkt_ref, brow_ref, bcol_ref,
                          q_ref, k_ref, v_ref, o_ref, u_ref, ga_ref, gb_ref, gcol_ref, grow_ref, kt_ref, x_ref,
                          xbuf0_ref, xbuf1_ref, sem, *, n_prompt_tiles, n_tiles, tm):
    i = pl.program_id(0)
    bufs = (xbuf0_ref, xbuf1_ref)

    def request(tile, slot, lo, hi, unrolled):
        def one(r, carry=0):
            pltpu.make_async_copy(ys_ref.at[pl.ds(dest_ref[tile * tm + r], 1), :],
                                  bufs[slot].at[pl.ds(r, 1), :], sem.at[slot]).start(
                                      priority=r % 2 if unrolled else 0)
            return carry
        if unrolled:
            for r in range(lo, hi):
                one(r)
        else:
            lax.fori_loop(lo, hi, one, 0, unroll=8)

    arrived = lambda slot: pltpu.make_async_copy(ys_ref.at[pl.ds(0, tm), :], bufs[slot], sem.at[slot])

    @pl.when(i == 0)
    def _():
        request(0, 0, 0, tm, False)

    def step(cur):
        arrived(cur).wait()
        request(jnp.minimum(i + 1, n_tiles - 1), 1 - cur, 0, tm, True)
        x32 = bufs[cur][...]
        x_ref[...] = x32
        _inproj_body(x32.astype(BF16), wa_ref, wb_ref, wg_ref, wgt_ref, wkt_ref, brow_ref, bcol_ref, q_ref, k_ref, v_ref,
                     o_ref, u_ref, ga_ref, gb_ref, gcol_ref, grow_ref, kt_ref, n_prompt_tiles)

        @pl.when(i == n_tiles - 1)
        def _():
            arrived(1 - cur).wait()

    for parity in (0, 1):
        pl.when(i % 2 == parity)(functools.partial(step, parity))


def _inproj_body(x, wa_ref, wb_ref, wg_ref, wgt_ref, wkt_ref, brow_ref, bcol_ref,
                 q_ref, k_ref, v_ref, o_ref, u_ref, ga_ref, gb_ref, gcol_ref, grow_ref, kt_ref, n_prompt_tiles):
    @pl.when(pl.program_id(0) >= n_prompt_tiles)
    def _():
        kt = lax.dot_general(wkt_ref[...], x, (((1,), (1,)), ((), ())), preferred_element_type=F32)
        kt_ref[...] = (kt * K_SCALE).astype(BF16)

    def seg(i):
        ref, lo, hi = _SEGS[i]
        return jnp.dot(x, (wa_ref, wb_ref)[ref][:, lo:hi], preferred_element_type=F32)

    q_ref[...] = seg(0).astype(BF16)
    k_ref[...] = (seg(1) * K_SCALE).astype(BF16)
    v_ref[...] = seg(2).astype(BF16)
    o_ref[...] = seg(3)
    u_ref[...] = seg(4)
    ga_ref[...] = seg(5)
    gb_ref[...] = seg(6)
    g = jnp.dot(x, wg_ref[...], preferred_element_type=F32) + brow_ref[...]
    lane = lax.broadcasted_iota(I32, g.shape, 1)
    gcol_ref[...] = jnp.where(lane < N_HEADS, g, jax.nn.log_sigmoid(g))
    gt = lax.dot_general(wgt_ref[...], x, (((1,), (1,)), ((), ())), preferred_element_type=F32) + bcol_ref[...]
    sub = lax.broadcasted_iota(I32, gt.shape, 0)
    grow_ref[...] = jnp.where(sub < N_HEADS, gt, jax.nn.log_sigmoid(gt))


def _inproj(x, w, layer, n_prompt, dest=None):
    n = sum(a.shape[0] for a in x) if dest is None else dest.shape[0]
    tm = _row_tile(math.gcd(n_prompt, n - n_prompt))
    ntp = n_prompt // tm
    row = lambda w: pl.BlockSpec((tm, w), lambda i, *_: (i, 0))
    consts = (w["w_qkvo"], w["w_rest"], w["w_gate"], w["w_gate_t"], w["w_key_t"], w["b_row"], w["b_col"])
    const_specs = [_layer_spec(c, layer) for c in consts]
    out_shape = [
        jax.ShapeDtypeStruct((n, D_MLSTM), BF16), jax.ShapeDtypeStruct((n, D_MLSTM), BF16),
        jax.ShapeDtypeStruct((n, D_MLSTM), BF16), jax.ShapeDtypeStruct((n, D_MLSTM), F32),
        jax.ShapeDtypeStruct((n, D_POOL), F32), jax.ShapeDtypeStruct((n, D_MODEL), F32),
        jax.ShapeDtypeStruct((n, D_MODEL), F32), jax.ShapeDtypeStruct((n, LANES), F32),
        jax.ShapeDtypeStruct((GATE_ROWS, n), F32), jax.ShapeDtypeStruct((D_MLSTM, n - n_prompt), BF16)]
    out_specs = [row(D_MLSTM), row(D_MLSTM), row(D_MLSTM), row(D_MLSTM), row(D_POOL), row(D_MODEL),
                 row(D_MODEL), row(LANES), pl.BlockSpec((GATE_ROWS, tm), lambda i, *_: (0, i)),
                 pl.BlockSpec((D_MLSTM, tm), lambda i, *_: (0, jnp.maximum(i - ntp, 0))), row(D_MODEL)]
    out_shape.append(jax.ShapeDtypeStruct((n, D_MODEL), F32))
    if dest is None:
        return pl.pallas_call(
            functools.partial(_inproj_kernel, n_prompt_tiles=ntp),
            grid=(n // tm,),
            in_specs=[pl.BlockSpec((tm, D_MODEL), lambda i: (jnp.minimum(i, ntp - 1), 0)),
                      pl.BlockSpec((tm, D_MODEL), lambda i: (jnp.maximum(i - ntp, 0), 0))] + const_specs,
            out_specs=out_specs,
            out_shape=out_shape,
            compiler_params=_params(("arbitrary",)),
            name="inproj",
        )(*x, *consts)
    grid_spec = pltpu.PrefetchScalarGridSpec(
        num_scalar_prefetch=1,
        grid=(n // tm,),
        in_specs=[pl.BlockSpec(memory_space=pl.ANY)] + const_specs,
        out_specs=out_specs,
        scratch_shapes=[pltpu.VMEM((tm, D_MODEL), F32), pltpu.VMEM((tm, D_MODEL), F32),
                        pltpu.SemaphoreType.DMA((2,))])
    return pl.pallas_call(
        functools.partial(_inproj_gather_kernel, n_prompt_tiles=ntp, n_tiles=n // tm, tm=tm),
        grid_spec=grid_spec,
        out_shape=out_shape,
        compiler_params=_params(("arbitrary",)),
        name="inproj_gather",
    )(dest, x, *consts)


def _head_out(hval, o, gain):
    mu = jnp.mean(hval, axis=1, keepdims=True)
    xc = hval - mu
    var = jnp.mean(xc * xc, axis=1, keepdims=True)
    return (jax.nn.sigmoid(o) * (xc * lax.rsqrt(var + LN_EPS) * gain)).astype(BF16)


def _split3(x):
    x1 = x.astype(BF16)
    r1 = x - x1.astype(F32)
    x2 = r1.astype(BF16)
    x3 = (r1 - x2.astype(F32)).astype(BF16)
    return x1, x2, x3


def _mlstm_prompt_kernel(*refs, rows, strip, chained):
    if chained:
        q_ref, k_ref, v_ref, gcol_ref, grow_ref, o_ref, gain_ref, _, hg_ref, c_ref, n_ref, m_ref = refs
    else:
        q_ref, k_ref, v_ref, gcol_ref, grow_ref, o_ref, gain_ref, hg_ref, c_ref, n_ref, m_ref = refs

    @pl.when(pl.program_id(1) == 0)
    def _():
        c_ref[...] = jnp.zeros_like(c_ref)
        n_ref[...] = jnp.zeros_like(n_ref)
        m_ref[...] = jnp.zeros_like(m_ref)

    t_idx = lax.broadcasted_iota(I32, (rows, 1), 0)
    s_idx = lax.broadcasted_iota(I32, (1, rows), 1)
    mask = s_idx <= t_idx
    lower = jnp.where(mask, 1.0, 0.0).astype(BF16)
    upper = jnp.where(t_idx <= s_idx, 1.0, 0.0).astype(BF16)
    gcol = gcol_ref[...]
    grow = grow_ref[...]
    bcol = sum(jnp.dot(lower, p, preferred_element_type=F32) for p in _split3(gcol))
    brow = sum(jnp.dot(p, upper, preferred_element_type=F32) for p in _split3(grow))
    ones = jnp.ones((rows, LANES), BF16)
    neg_inf = jnp.float32(-jnp.inf)
    heads = range(N_HEADS)
    sls = [slice(h * HEAD_DIM, (h + 1) * HEAD_DIM) for h in heads]
    qs = [q_ref[:, sl] for sl in sls]
    ks = [k_ref[:, sl] for sl in sls]
    vos = [jnp.concatenate([v_ref[:, sl], ones], axis=1) for sl in sls]
    b_cs = [bcol[:, N_HEADS + h:N_HEADS + h + 1] for h in heads]
    b_rs = [brow[N_HEADS + h:N_HEADS + h + 1, :] for h in heads]
    g_rs = [grow[h:h + 1, :] - b_rs[h] for h in heads]
    m_prevs = [m_ref[0, h:h + 1, 0:1] for h in heads]
    c_prevs = [c_ref[0, h] for h in heads]
    n_prevs = [n_ref[0, h] for h in heads]
    qks = [lax.dot_general(qs[h], ks[h], (((1,), (1,)), ((), ())), preferred_element_type=F32) for h in heads]
    inters = [jnp.dot(qs[h], jnp.concatenate([c_prevs[h].astype(BF16), n_prevs[h].astype(BF16)], axis=1),
                      preferred_element_type=F32) for h in heads]
    gms = [jnp.where(mask, g_rs[h], neg_inf) for h in heads]
    tops = [jnp.maximum(m_prevs[h], jnp.max(gms[h], axis=1, keepdims=True)) for h in heads]
    ss = [(qks[h] * jnp.exp(gms[h] - tops[h])).astype(BF16) for h in heads]
    intras = [jnp.dot(ss[h], vos[h], preferred_element_type=F32) for h in heads]
    b_lasts = [b_rs[h][:, rows - 1:rows] for h in heads]
    d_lasts = [b_lasts[h] - b_cs[h] + gcol[:, h:h + 1] for h in heads]
    m_news = [jnp.maximum(b_lasts[h] + m_prevs[h], jnp.max(d_lasts[h], axis=0, keepdims=True)) for h in heads]
    kws = [(ks[h].astype(F32) * jnp.exp(d_lasts[h] - m_news[h])).astype(BF16) for h in heads]
    upds = [lax.dot_general(kws[h], vos[h], (((0,), (0,)), ((), ())), preferred_element_type=F32) for h in heads]
    s_inters = [jnp.exp(m_prevs[h] - tops[h]) for h in heads]
    nums = [s_inters[h] * inters[h][:, 0:HEAD_DIM] + intras[h][:, 0:HEAD_DIM] for h in heads]
    dens = [s_inters[h] * inters[h][:, HEAD_DIM:HEAD_DIM + 1] + intras[h][:, HEAD_DIM:HEAD_DIM + 1] for h in heads]
    hvals = [nums[h] / jnp.maximum(jnp.abs(dens[h]), jnp.exp(-(b_cs[h] + tops[h]))) for h in heads]
    mus = [jnp.mean(hvals[h], axis=1, keepdims=True) for h in heads]
    xcs = [hvals[h] - mus[h] for h in heads]
    vars_ = [jnp.mean(xcs[h] * xcs[h], axis=1, keepdims=True) for h in heads]
    for h in heads:
        hn = xcs[h] * lax.rsqrt(vars_[h] + LN_EPS) * gain_ref[:, sls[h]]
        hg_ref[:, sls[h]] = (jax.nn.sigmoid(o_ref[:, sls[h]]) * hn).astype(BF16)
    for h in heads:
        s_last = jnp.exp(b_lasts[h] + m_prevs[h] - m_news[h])
        c_ref[0, h] = s_last * c_prevs[h] + upds[h][:, 0:HEAD_DIM]
        n_ref[0, h] = s_last * n_prevs[h] + upds[h][:, HEAD_DIM:]
        m_ref[0, h:h + 1, :] = jnp.broadcast_to(m_news[h], (1, LANES))


def _mlstm_sample_kernel(*refs, seq_len, window, block, chained):
    if chained:
        (q_ref, k_ref, kt_ref, v_ref, gcol_ref, grow_ref, mtok_ref, o_ref, gain_ref, c0_ref, n0_ref, _,
         hg_ref, c1_ref, n1_ref, m1_ref) = refs
    else:
        (q_ref, k_ref, kt_ref, v_ref, gcol_ref, grow_ref, mtok_ref, o_ref, gain_ref, c0_ref, n0_ref,
         hg_ref, c1_ref, n1_ref, m1_ref) = refs
    n_seq = window // seq_len
    shift = int(math.log2(seq_len))
    w0 = (pl.program_id(0) % (block // window)) * window
    r_idx = lax.broadcasted_iota(I32, (window, 1), 0)
    t_idx = w0 + r_idx
    s_idx = lax.broadcasted_iota(I32, (1, block), 1)
    same = jnp.right_shift(t_idx, shift) == jnp.right_shift(s_idx, shift)
    mask = same & (s_idx <= t_idx)
    mask_t = same & (t_idx <= s_idx)
    r_seq = jnp.right_shift(r_idx, shift)
    l_seq = jnp.right_shift(s_idx - w0, shift)
    gcol = gcol_ref[...]
    grow = grow_ref[...]
    mtok = mtok_ref[...]
    neg_inf = jnp.float32(-jnp.inf)
    for h in range(N_HEADS):
        sl = slice(h * HEAD_DIM, (h + 1) * HEAD_DIM)
        q = q_ref[:, sl]
        k = k_ref[:, sl]
        kt = kt_ref[sl, :]
        v = v_ref[:, sl]
        li_r = grow[h:h + 1, :]
        lf_r = grow[N_HEADS + h:N_HEADS + h + 1, :]
        li_c = gcol[:, h:h + 1]
        lf_c = gcol[:, N_HEADS + h:N_HEADS + h + 1]
        m_c = mtok[:, h:h + 1]
        b_c = jnp.sum(jnp.where(mask, lf_r, 0.0), axis=1, keepdims=True)
        b_r = jnp.sum(jnp.where(mask_t, lf_c, 0.0), axis=0, keepdims=True)
        dmat = jnp.where(mask, b_c - b_r + li_r, neg_inf)
        inter = b_c + m_c
        m_t = jnp.maximum(inter, jnp.max(dmat, axis=1, keepdims=True))
        s_inter = jnp.exp(inter - m_t)
        s = jnp.dot(q, kt, preferred_element_type=F32) * jnp.exp(dmat - m_t)
        intra = jnp.dot(s.astype(BF16), v, preferred_element_type=F32)
        qf = q.astype(F32)
        qc = qn = None
        for j in range(n_seq):
            qc_j = jnp.dot(q, c0_ref[j, h].astype(BF16), preferred_element_type=F32)
            qn_j = jnp.sum(qf * n0_ref[j, h:h + 1, :], axis=1, keepdims=True)
            qc = qc_j if j == 0 else jnp.where(r_seq == j, qc_j, qc)
            qn = qn_j if j == 0 else jnp.where(r_seq == j, qn_j, qn)
        num = s_inter * qc + intra
        den = s_inter * qn + jnp.sum(s, axis=1, keepdims=True)
        hval = num / jnp.maximum(jnp.abs(den), jnp.exp(-m_t))
        hg_ref[:, sl] = _head_out(hval, o_ref[:, sl], gain_ref[:, sl])
        kf = k.astype(F32)
        ktf = kt.astype(F32)
        for j in range(n_seq):
            lsel = l_seq == j
            m_j = mtok[j * seq_len:j * seq_len + 1, h:h + 1]
            b_last = jnp.sum(jnp.where(lsel, lf_r, 0.0), axis=1, keepdims=True)
            d_r = jnp.where(lsel, b_last - b_r + li_r, neg_inf)
            d_c = jnp.where(r_seq == j, b_last - b_c + li_c, neg_inf)
            m_new = jnp.maximum(b_last + m_j, jnp.max(d_r, axis=1, keepdims=True))
            s_last = jnp.exp(b_last + m_j - m_new)
            kwt = (ktf * jnp.exp(d_r - m_new)).astype(BF16)
            c1_ref[j, h] = s_last * c0_ref[j, h] + jnp.dot(kwt, v, preferred_element_type=F32)
            n1_ref[j, h:h + 1, :] = (s_last * n0_ref[j, h:h + 1, :]
                                     + jnp.sum(kf * jnp.exp(d_c - m_new), axis=0, keepdims=True))
            m1_ref[j, h:h + 1, :] = jnp.broadcast_to(m_new, (1, LANES))


def _chain(c_all):
    if c_all is None:
        return [], []
    return [c_all], [pl.BlockSpec(memory_space=pl.ANY)]


def _mlstm_prompt(q, k, v, gcol, grow, o, gain, c_all, layer, n_seq, seq_len):
    n = n_seq * seq_len
    chunk = math.gcd(seq_len, MLSTM_PROMPT_CHUNK)
    nc = seq_len // chunk
    row = lambda w: pl.BlockSpec((chunk, w), lambda b, c: (b * nc + c, 0))
    st = lambda *tail: pl.BlockSpec((1,) + tail, lambda b, c: (b,) + (0,) * len(tail))
    extra, extra_specs = _chain(c_all)
    n_in = 7
    return pl.pallas_call(
        functools.partial(_mlstm_prompt_kernel, rows=chunk, strip=min(chunk, MLSTM_STRIP), chained=bool(extra)),
        grid=(n_seq, nc),
        in_specs=[row(D_MLSTM), row(D_MLSTM), row(D_MLSTM), row(LANES),
                  pl.BlockSpec((GATE_ROWS, chunk), lambda b, c: (0, b * nc + c)),
                  row(D_MLSTM), _layer_spec(gain, layer)] + extra_specs,
        out_specs=(row(D_MLSTM),
                   pl.BlockSpec((None, 1, N_HEADS, HEAD_DIM, HEAD_DIM), lambda b, c: (layer, b, 0, 0, 0)),
                   st(N_HEADS, HEAD_DIM, LANES), st(N_HEADS, LANES)),
        out_shape=(jax.ShapeDtypeStruct((n, D_MLSTM), BF16),
                   jax.ShapeDtypeStruct((DEPTH, n_seq, N_HEADS, HEAD_DIM, HEAD_DIM), F32),
                   jax.ShapeDtypeStruct((n_seq, N_HEADS, HEAD_DIM, LANES), F32),
                   jax.ShapeDtypeStruct((n_seq, N_HEADS, LANES), F32)),
        input_output_aliases={n_in: 1} if extra else {},
        compiler_params=_params(("parallel", "arbitrary")),
        name="mlstm_prompt",
    )(q, k, v, gcol, grow, o, gain, *extra)


def _mlstm_sample(q, k, kt, v, gcol, grow, mtok, o, gain, c0, n0, c_all, layer, first_row, n_seq, seq_len):
    n = n_seq * seq_len
    window, block = SAMPLE_WINDOW, SAMPLE_BLOCK_ROWS
    assert n % block == 0 and first_row % block == 0 and window % seq_len == 0
    per_win = window // seq_len
    sub = block // window
    row = lambda w: pl.BlockSpec((window, w), lambda i: (first_row // window + i, 0))
    extra, extra_specs = _chain(c_all)
    n_in = 11
    return pl.pallas_call(
        functools.partial(_mlstm_sample_kernel, seq_len=seq_len, window=window, block=block, chained=bool(extra)),
        grid=(n // window,),
        in_specs=[row(D_MLSTM), row(D_MLSTM),
                  pl.BlockSpec((D_MLSTM, block), lambda i: (0, i // sub)),
                  pl.BlockSpec((block, D_MLSTM), lambda i: (first_row // block + i // sub, 0)),
                  row(LANES),
                  pl.BlockSpec((GATE_ROWS, block), lambda i: (0, first_row // block + i // sub)),
                  pl.BlockSpec((window, LANES), lambda i: (i, 0)),
                  row(D_MLSTM), _layer_spec(gain, layer),
                  pl.BlockSpec((None, per_win, N_HEADS, HEAD_DIM, HEAD_DIM), lambda i: (layer, i, 0, 0, 0)),
                  pl.BlockSpec((None, per_win, N_HEADS, HEAD_DIM), lambda i: (layer, i, 0, 0))] + extra_specs,
        out_specs=(pl.BlockSpec((window, D_MLSTM), lambda i: (i, 0)),
                   pl.BlockSpec((None, per_win, N_HEADS, HEAD_DIM, HEAD_DIM), lambda i: (layer, i, 0, 0, 0)),
                   pl.BlockSpec((per_win, N_HEADS, HEAD_DIM), lambda i: (i, 0, 0)),
                   pl.BlockSpec((per_win, N_HEADS, LANES), lambda i: (i, 0, 0))),
        out_shape=(jax.ShapeDtypeStruct((n, D_MLSTM), BF16),
                   jax.ShapeDtypeStruct((DEPTH, n_seq, N_HEADS, HEAD_DIM, HEAD_DIM), F32),
                   jax.ShapeDtypeStruct((n_seq, N_HEADS, HEAD_DIM), F32),
                   jax.ShapeDtypeStruct((n_seq, N_HEADS, LANES), F32)),
        input_output_aliases={n_in: 1} if extra else {},
        compiler_params=_params(("parallel",)),
        name="mlstm_sample",
    )(q, k, kt, v, gcol, grow, mtok, o, gain, c0, n0, *extra)


def _pool_prompt_kernel(u_ref, prev_ref, out_ref, tail_ref, *, tm, tiles_per_seq):
    tile = pl.program_id(0) % tiles_per_seq
    head = 16
    tail_ref[...] = u_ref[tm - head:tm, :]
    pos = tile * tm + lax.broadcasted_iota(I32, (tm, 1), 0)
    for g, w in enumerate(POOL_WINDOWS):
        sl = slice(g * POOL_GROUP_DIM, (g + 1) * POOL_GROUP_DIM)
        acc = jnp.concatenate([jnp.where(tile == 0, 0.0, prev_ref[:, sl]), u_ref[:, sl]], axis=0)
        span = 1
        while span < w:
            acc = acc + pltpu.roll(acc, span, 0)
            span *= 2
        cnt = jnp.minimum(pos + 1, w).astype(F32)
        out_ref[:, sl] = (acc[head:] / cnt - u_ref[:, sl]).astype(BF16)


def _pool_prompt(u, n_seq, seq_len):
    n = n_seq * seq_len
    tm = _row_tile(seq_len)
    head = 16
    return pl.pallas_call(
        functools.partial(_pool_prompt_kernel, tm=tm, tiles_per_seq=seq_len // tm),
        grid=(n // tm,),
        in_specs=[pl.BlockSpec((tm, D_POOL), lambda i: (i, 0)),
                  pl.BlockSpec((head, D_POOL), lambda i: (jnp.maximum(i * (tm // head) - 1, 0), 0))],
        out_specs=(pl.BlockSpec((tm, D_POOL), lambda i: (i, 0)),
                   pl.BlockSpec((head, D_POOL), lambda i: (i // (seq_len // tm), 0))),
        out_shape=(jax.ShapeDtypeStruct((n, D_POOL), BF16), jax.ShapeDtypeStruct((n_seq * head, D_POOL), F32)),
        compiler_params=_params(("arbitrary",)),
        name="pool_prompt",
    )(u, u)


def _pool_sample_kernel(u_ref, buf_ref, out_ref, nbuf_ref, ext_ref, *, seq_len, start):
    ext_ref[:, 0:POOL_BUF, :] = buf_ref[...]
    ext_ref[:, POOL_BUF:POOL_BUF + seq_len, :] = u_ref[...]
    pos = start + lax.broadcasted_iota(I32, (1, seq_len, 1), 1)
    for g, w in enumerate(POOL_WINDOWS):
        sl = slice(g * POOL_GROUP_DIM, (g + 1) * POOL_GROUP_DIM)
        acc = ext_ref[:, POOL_BUF:POOL_BUF + seq_len, sl]
        for d in range(1, w):
            acc = acc + ext_ref[:, POOL_BUF - d:POOL_BUF - d + seq_len, sl]
        cnt = jnp.minimum(pos + 1, w).astype(F32)
        out_ref[:, :, sl] = (acc / cnt - u_ref[:, :, sl]).astype(BF16)
    nbuf_ref[...] = ext_ref[:, seq_len:seq_len + POOL_BUF, :]


def _pool_sample(u3, buf, layer, start):
    n_seq, seq_len, _ = u3.shape
    bs = _row_tile(n_seq, 32)
    spec = lambda r: pl.BlockSpec((bs, r, D_POOL), lambda i: (i, 0, 0))
    return pl.pallas_call(
        functools.partial(_pool_sample_kernel, seq_len=seq_len, start=start),
        grid=(n_seq // bs,),
        in_specs=[spec(seq_len), pl.BlockSpec((None, bs, POOL_BUF, D_POOL), lambda i: (layer, i, 0, 0))],
        out_specs=(spec(seq_len), spec(POOL_BUF)),
        out_shape=(jax.ShapeDtypeStruct((n_seq, seq_len, D_POOL), BF16),
                   jax.ShapeDtypeStruct((n_seq, POOL_BUF, D_POOL), F32)),
        scratch_shapes=[pltpu.VMEM((bs, POOL_BUF + seq_len + 5, D_POOL), F32)],
        compiler_params=_params(("parallel",)),
        name="pool_sample",
    )(u3, buf)


def _layer_norm(y, g, b):
    mu = jnp.mean(y, axis=1, keepdims=True)
    yc = y - mu
    var = jnp.mean(yc * yc, axis=1, keepdims=True)
    return yc * lax.rsqrt(var + LN_EPS) * g + b


def _route(logits_t):
    tokens = logits_t.shape[1]
    grp = lax.broadcasted_iota(I32, (8, tokens), 0)
    live = grp < N_EXPERT_GROUPS
    neg_inf = jnp.float32(-jnp.inf)
    lm = [jnp.where(live, logits_t[8 * m:8 * m + 8, :], neg_inf) for m in range(EXPERTS_PER_GROUP)]
    mx = jnp.max(jnp.maximum(jnp.maximum(lm[0], lm[1]), jnp.maximum(lm[2], lm[3])), axis=0, keepdims=True)
    ex = [jnp.exp(l - mx) for l in lm]
    tot = jnp.sum(ex[0] + ex[1] + ex[2] + ex[3], axis=0, keepdims=True)
    p = [e / tot for e in ex]
    top1 = jnp.maximum(jnp.maximum(p[0], p[1]), jnp.maximum(p[2], p[3]))
    i1 = jnp.where(p[0] == top1, 0, jnp.where(p[1] == top1, 1, jnp.where(p[2] == top1, 2, 3)))
    r = [jnp.where(i1 == m, -1.0, p[m]) for m in range(EXPERTS_PER_GROUP)]
    top2 = jnp.maximum(jnp.maximum(r[0], r[1]), jnp.maximum(r[2], r[3]))
    i2 = jnp.where(r[0] == top2, 0, jnp.where(r[1] == top2, 1, jnp.where(r[2] == top2, 2, 3)))
    gscore = jnp.where(live, top1 + top2, neg_inf)
    gmax = jnp.max(gscore, axis=0, keepdims=True)
    gsel = jnp.min(jnp.where(gscore == gmax, grp, 8), axis=0, keepdims=True)
    chosen = grp == gsel
    tsum = top1 + top2
    w1 = top1 / tsum
    w2 = top2 / tsum
    first_is_lo = i1 < i2
    lo = jnp.minimum(i1, i2)
    hi = jnp.maximum(i1, i2)
    pair = jnp.where(lo == 0, hi - 1, jnp.where(lo == 1, hi + 1, 5))
    pick = lambda a: jnp.sum(jnp.where(chosen, a, jnp.zeros_like(a)), axis=0, keepdims=True)
    cls = pick(grp * len(PAIRS) + pair)
    w_lo = pick(jnp.where(first_is_lo, w1, w2))
    w_hi = pick(jnp.where(first_is_lo, w2, w1))
    return cls, w_lo, w_hi


def _mix_kernel(hgp_ref, hgs_ref, plp_ref, pls_ref, ga_ref, gb_ref, x_ref, wpool_ref, pscale_ref, wa_ref, wb_ref,
                wout_ref, g1_ref, b1_ref, wr_ref, br_ref,
                dest_ref, cnt_ref, xs_ref,
                rows0_ref, rows1_ref, dvm_ref, dsm0_ref, dsm1_ref, carry_ref, row_sem, idx_sem,
                *, tm, n_prompt_tiles, n_tiles, capacity):
    i = pl.program_id(0)
    rows = (rows0_ref, rows1_ref)
    dsm = (dsm0_ref, dsm1_ref)
    spare = N_CLASSES * capacity
    sent = lambda s: pltpu.make_async_copy(rows[s], xs_ref.at[pl.ds(0, tm), :], row_sem.at[s])

    def send(s, unrolled):
        def one(r, carry=0):
            pltpu.make_async_copy(rows[s].at[pl.ds(r, 1), :], xs_ref.at[pl.ds(dsm[s][r], 1), :],
                                  row_sem.at[s]).start(priority=r % 2 if unrolled else 0)
            return carry
        if unrolled:
            for r in range(tm):
                one(r)
        else:
            lax.fori_loop(0, tm, one, 0, unroll=8)

    @pl.when(i == 0)
    def _():
        carry_ref[...] = jnp.zeros_like(carry_ref)
        rows0_ref[...] = jnp.zeros_like(rows0_ref)
        rows1_ref[...] = jnp.zeros_like(rows1_ref)

        def spare_rows(r, carry):
            dsm0_ref[r] = spare + r
            return carry

        lax.fori_loop(0, tm, spare_rows, 0)
        send(0, False)
        dvm_ref[...] = spare + tm + lax.broadcasted_iota(I32, (8, tm), 1)
        pltpu.make_async_copy(dvm_ref.at[0], dsm1_ref, idx_sem).start()

    for parity in (0, 1):
        pl.when(i % 2 == parity)(functools.partial(
            _mix_step, parity, i, hgp_ref, hgs_ref, plp_ref, pls_ref, ga_ref, gb_ref, x_ref, wpool_ref, pscale_ref,
            wa_ref, wb_ref, wout_ref, g1_ref, b1_ref, wr_ref, br_ref, dest_ref, cnt_ref, rows, dvm_ref, dsm,
            carry_ref, idx_sem, send, sent, tm, n_prompt_tiles, n_tiles, capacity))


def _mix_step(cur, i, hgp_ref, hgs_ref, plp_ref, pls_ref, ga_ref, gb_ref, x_ref, wpool_ref, pscale_ref, wa_ref,
              wb_ref, wout_ref, g1_ref, b1_ref, wr_ref, br_ref, dest_ref, cnt_ref, rows, dvm_ref, dsm, carry_ref,
              idx_sem, send, sent, tm, n_prompt_tiles, n_tiles, capacity):
    slots_arrived = lambda s: pltpu.make_async_copy(dvm_ref.at[0], dsm[s], idx_sem)
    slots_arrived(1 - cur).wait()
    send(1 - cur, True)
    is_prompt = i < n_prompt_tiles
    hg = jnp.where(is_prompt, hgp_ref[...], hgs_ref[...])
    pooled = jnp.where(is_prompt, plp_ref[...], pls_ref[...])
    ya = jnp.dot(hg, wa_ref[...], preferred_element_type=F32)
    parts = []
    for g in range(len(POOL_WINDOWS)):
        sl = slice(g * POOL_GROUP_DIM, (g + 1) * POOL_GROUP_DIM)
        parts.append(jnp.dot(pooled[:, sl], wpool_ref[g], preferred_element_type=F32))
    pl_lin = jnp.concatenate(parts, axis=1) * pscale_ref[...]
    yb = jnp.dot(pl_lin.astype(BF16), wb_ref[...], preferred_element_type=F32)
    mix = jax.nn.sigmoid(ga_ref[...]) * ya + jax.nn.sigmoid(gb_ref[...]) * yb
    res = jnp.dot(mix.astype(BF16), wout_ref[...], preferred_element_type=F32)
    x1 = _layer_norm(ALPHA * x_ref[...] + res, g1_ref[...], b1_ref[...])

    nt = lambda a, b: lax.dot_general(a, b, (((1,), (1,)), ((), ())), preferred_element_type=F32)
    wr = wr_ref[...]
    wr_hi = wr.astype(BF16)
    wr_lo = (wr - wr_hi.astype(F32)).astype(BF16)
    x1_hi = x1.astype(BF16)
    x1_lo = (x1 - x1_hi.astype(F32)).astype(BF16)
    logits_t = nt(wr_hi, x1_hi) + (nt(wr_hi, x1_lo) + nt(wr_lo, x1_hi)) + br_ref[...]
    cls, w_lo, w_hi = _route(logits_t)

    onehot = lax.broadcasted_iota(I32, (CLASS_ROWS, tm), 0) == cls
    earlier = lax.broadcasted_iota(I32, (tm, tm), 0) < lax.broadcasted_iota(I32, (tm, tm), 1)
    before = jnp.dot(jnp.where(onehot, 1.0, 0.0).astype(BF16), jnp.where(earlier, 1.0, 0.0).astype(BF16),
                     preferred_element_type=F32)
    seen = carry_ref[:, 0:1]
    rank = jnp.sum(jnp.where(onehot, before + seen, 0.0), axis=0, keepdims=True)
    carry_ref[...] = carry_ref[...] + jnp.sum(jnp.where(onehot, 1.0, 0.0), axis=1, keepdims=True)
    cnt_ref[...] = carry_ref[...]
    dest = cls * capacity + rank.astype(I32)
    dest_ref[...] = jnp.broadcast_to(dest, (8, tm))

    sent(cur).wait()
    wrows = jnp.concatenate([w_lo, w_hi, jnp.zeros((LANES - 2, tm), F32)], axis=0)
    rows[cur][:, 0:D_MODEL] = x1
    rows[cur][:, D_MODEL:ROW_EXT] = wrows.T
    dvm_ref[...] = jnp.broadcast_to(dest, (8, tm))
    slots_arrived(cur).start()

    @pl.when(i == n_tiles - 1)
    def _():
        slots_arrived(cur).wait()
        send(cur, False)
        sent(cur).wait()
        sent(1 - cur).wait()


def _mix(hg_p, hg_s, pooled_p, pooled_s, ga, gb, x, w, layer, capacity):
    n = x.shape[0]
    n_p = hg_p.shape[0]
    tm = _row_tile(math.gcd(n_p, n - n_p))
    ntp = n_p // tm
    row = lambda w: pl.BlockSpec((tm, w), lambda i: (i, 0))
    prow = lambda w: pl.BlockSpec((tm, w), lambda i: (jnp.minimum(i, ntp - 1), 0))
    srow = lambda w: pl.BlockSpec((tm, w), lambda i: (jnp.maximum(i - ntp, 0), 0))
    per_layer = (w["w_pool"], w["pool_scale"], w["w_proj_a"], w["w_proj_b"], w["w_out"], w["ln1_g"], w["ln1_b"])
    shared = (w["w_router_t"], w["b_router_col"])
    consts = per_layer + shared
    return pl.pallas_call(
        functools.partial(_mix_kernel, tm=tm, n_prompt_tiles=ntp, n_tiles=n // tm, capacity=capacity),
        grid=(n // tm,),
        in_specs=[prow(D_MLSTM), srow(D_MLSTM), prow(D_POOL), srow(D_POOL), row(D_MODEL), row(D_MODEL), row(D_MODEL)]
                 + [_layer_spec(c, layer) for c in per_layer] + [_const_spec(c.shape) for c in shared],
        out_specs=(pl.BlockSpec((8, tm), lambda i: (0, i)),
                   pl.BlockSpec((CLASS_ROWS, LANES), lambda i: (0, 0)),
                   pl.BlockSpec(memory_space=pl.ANY)),
        out_shape=(jax.ShapeDtypeStruct((8, n), I32),
                   jax.ShapeDtypeStruct((CLASS_ROWS, LANES), F32),
                   jax.ShapeDtypeStruct((N_CLASSES * capacity + 2 * tm, ROW_EXT), F32)),
        scratch_shapes=[pltpu.VMEM((tm, ROW_EXT), F32), pltpu.VMEM((tm, ROW_EXT), F32), pltpu.VMEM((8, tm), I32),
                        pltpu.SMEM((tm,), I32), pltpu.SMEM((tm,), I32), pltpu.VMEM((CLASS_ROWS, LANES), F32),
                        pltpu.SemaphoreType.DMA((2,)), pltpu.SemaphoreType.DMA],
        compiler_params=_params(("arbitrary",)),
        name="mix",
    )(hg_p, hg_s, pooled_p, pooled_s, ga, gb, x, *consts)


def _moe_kernel(blk_ref, elo_ref, ehi_ref, nvalid_ref, ntiles_ref,
                xs_ref, wg_lo, wu_lo, wd_lo, wg_hi, wu_hi, wd_hi, g2_ref, b2_ref, ys_ref):
    i = pl.program_id(0)

    @pl.when(i < ntiles_ref[0])
    def _():
        valid = lax.broadcasted_iota(I32, (MOE_TILE, 1), 0) < nvalid_ref[i]
        xe = xs_ref[...]
        x = jnp.where(valid, xe[:, 0:D_MODEL], 0.0)
        w_lo = jnp.where(valid, xe[:, D_MODEL:D_MODEL + 1], 0.0)
        w_hi = jnp.where(valid, xe[:, D_MODEL + 1:D_MODEL + 2], 0.0)
        xb = x.astype(BF16)

        def expert(wg, wu, wd, w):
            g = jnp.dot(xb, wg[0], preferred_element_type=F32)
            u = jnp.dot(xb, wu[0], preferred_element_type=F32)
            hid = (g * jax.nn.sigmoid(g)) * u * w
            return jnp.dot(hid.astype(BF16), wd[0], preferred_element_type=F32)

        y = expert(wg_lo, wu_lo, wd_lo, w_lo) + expert(wg_hi, wu_hi, wd_hi, w_hi)
        ys_ref[...] = _layer_norm(ALPHA * x + y, g2_ref[...], b2_ref[...])


def _moe(xs, tables, w, layer, max_tiles):
    blk, e_lo, e_hi, n_valid, n_tiles = tables
    up = lambda sel: pl.BlockSpec((None, 1, D_MODEL, D_EXPERT),
                                  lambda i, b, lo, hi, nv, nt: (layer, (lo, hi)[sel][i], 0, 0))
    down = lambda sel: pl.BlockSpec((None, 1, D_EXPERT, D_MODEL),
                                    lambda i, b, lo, hi, nv, nt: (layer, (lo, hi)[sel][i], 0, 0))
    grid_spec = pltpu.PrefetchScalarGridSpec(
        num_scalar_prefetch=5,
        grid=(max_tiles,),
        in_specs=[pl.BlockSpec((MOE_TILE, ROW_EXT), lambda i, b, *_: (b[i], 0)),
                  up(0), up(0), down(0), up(1), up(1), down(1),
                  _layer_spec(w["ln2_g"], layer), _layer_spec(w["ln2_b"], layer)],
        out_specs=pl.BlockSpec((MOE_TILE, D_MODEL), lambda i, b, *_: (b[i], 0)))
    return pl.pallas_call(
        _moe_kernel,
        grid_spec=grid_spec,
        out_shape=jax.ShapeDtypeStruct((xs.shape[0], D_MODEL), F32),
        compiler_params=_params(("arbitrary",)),
        name="moe",
    )(blk, e_lo, e_hi, n_valid, n_tiles, xs, w["w_e_gate"], w["w_e_up"], w["w_e_down"],
      w["w_e_gate"], w["w_e_up"], w["w_e_down"], w["ln2_g"], w["ln2_b"])


def _tile_tables(counts, capacity, max_tiles):
    cnt = counts[:N_CLASSES, 0].astype(I32)
    tiles = (cnt + MOE_TILE - 1) // MOE_TILE
    ends = jnp.cumsum(tiles)
    starts = ends - tiles
    n_tiles = ends[-1]
    t = jnp.minimum(jnp.arange(max_tiles, dtype=I32), jnp.maximum(n_tiles - 1, 0))
    cls = jnp.sum((ends[None, :] <= t[:, None]).astype(I32), axis=1)
    onehot = (jnp.arange(N_CLASSES, dtype=I32)[None, :] == cls[:, None]).astype(I32)
    within = t - jnp.sum(onehot * starts[None, :], axis=1)
    n_valid = jnp.clip(jnp.sum(onehot * cnt[None, :], axis=1) - within * MOE_TILE, 0, MOE_TILE)
    blk = cls * (capacity // MOE_TILE) + within
    grp = cls // len(PAIRS)
    pair = cls % len(PAIRS)
    lo = jnp.where(pair < 3, 0, jnp.where(pair < 5, 1, 2))
    hi = jnp.where(pair < 3, pair + 1, jnp.where(pair < 5, pair - 1, 3))
    return (blk, grp * EXPERTS_PER_GROUP + lo, grp * EXPERTS_PER_GROUP + hi, n_valid,
            n_tiles.reshape(1).astype(I32))


def _unpermute_kernel(dest_ref, ys_ref, prompt_ref, sample_ref, buf_ref, sem, *, tm, n_prompt_tiles, n_tiles):
    i = pl.program_id(0)
    cur = i % 2

    def request(tile, slot):
        def fetch(r, carry):
            pltpu.make_async_copy(ys_ref.at[pl.ds(dest_ref[tile * tm + r], 1), :],
                                  buf_ref.at[slot, pl.ds(r, 1), :], sem.at[slot]).start()
            return carry

        lax.fori_loop(0, tm, fetch, 0, unroll=8)

    arrived = lambda slot: pltpu.make_async_copy(ys_ref.at[pl.ds(0, tm), :], buf_ref.at[slot], sem.at[slot])

    @pl.when(i == 0)
    def _():
        request(0, 0)

    @pl.when(i + 1 < n_tiles)
    def _():
        request(i + 1, 1 - cur)

    arrived(cur).wait()

    @pl.when(i < n_prompt_tiles)
    def _():
        prompt_ref[...] = buf_ref[cur]

    @pl.when(i >= n_prompt_tiles)
    def _():
        sample_ref[...] = buf_ref[cur]


def _unpermute(ys, dest, n_prompt):
    n = dest.shape[0]
    tm = _row_tile(math.gcd(n_prompt, n - n_prompt))
    ntp = n_prompt // tm
    grid_spec = pltpu.PrefetchScalarGridSpec(
        num_scalar_prefetch=1,
        grid=(n // tm,),
        in_specs=[pl.BlockSpec(memory_space=pl.ANY)],
        out_specs=(pl.BlockSpec((tm, D_MODEL), lambda i, d: (jnp.minimum(i, ntp - 1), 0)),
                   pl.BlockSpec((tm, D_MODEL), lambda i, d: (jnp.maximum(i - ntp, 0), 0))),
        scratch_shapes=[pltpu.VMEM((2, tm, D_MODEL), F32), pltpu.SemaphoreType.DMA((2,))])
    return pl.pallas_call(
        functools.partial(_unpermute_kernel, tm=tm, n_prompt_tiles=ntp, n_tiles=n // tm),
        grid_spec=grid_spec,
        out_shape=(jax.ShapeDtypeStruct((n_prompt, D_MODEL), F32),
                   jax.ShapeDtypeStruct((n - n_prompt, D_MODEL), F32)),
        compiler_params=_params(("arbitrary",)),
        name="unpermute",
    )(dest, ys)


def _prepare_weights(w_in, b_gate, hn_gain, w_pool, pool_scale, w_proj_a, w_proj_b, w_out, ln1_g, ln1_b,
                     ln2_g, ln2_b, w_router, b_router, w_e_gate, w_e_up, w_e_down):
    w_qkvo = w_in[:, :, :_G0].astype(BF16)
    w_rest = w_in[:, :, _U0:].astype(BF16)
    w_gate = w_in[:, :, _G0:_U0]
    w_gate_p = jnp.pad(w_gate, ((0, 0), (0, 0), (0, LANES - 2 * N_HEADS))).astype(BF16)
    w_gate_t = jnp.pad(jnp.swapaxes(w_gate, 1, 2), ((0, 0), (0, GATE_ROWS - 2 * N_HEADS), (0, 0))).astype(BF16)
    w_key_t = jnp.swapaxes(w_in[:, :, D_MLSTM:2 * D_MLSTM], 1, 2).astype(BF16)
    b_row = jnp.pad(b_gate, ((0, 0), (0, LANES - 2 * N_HEADS)))[:, None, :]
    b_col = jnp.pad(b_gate, ((0, 0), (0, GATE_ROWS - 2 * N_HEADS)))[:, :, None]
    wr = w_router.T.reshape(N_EXPERT_GROUPS, EXPERTS_PER_GROUP, D_MODEL).swapaxes(0, 1)
    wr = jnp.pad(wr, ((0, 0), (0, 8 - N_EXPERT_GROUPS), (0, 0))).reshape(ROUTER_ROWS, D_MODEL)
    br = b_router.reshape(N_EXPERT_GROUPS, EXPERTS_PER_GROUP).T
    br = jnp.pad(br, ((0, 0), (0, 8 - N_EXPERT_GROUPS))).reshape(ROUTER_ROWS, 1)
    per_row = lambda a: a.reshape(DEPTH, 1, -1)
    return dict(
        w_qkvo=w_qkvo, w_rest=w_rest, w_gate=w_gate_p, w_gate_t=w_gate_t, w_key_t=w_key_t, b_row=b_row, b_col=b_col,
        gain=per_row(hn_gain), w_pool=w_pool.astype(BF16), pool_scale=per_row(pool_scale),
        w_proj_a=w_proj_a.astype(BF16), w_proj_b=w_proj_b.astype(BF16), w_out=w_out.astype(BF16),
        ln1_g=per_row(ln1_g), ln1_b=per_row(ln1_b), ln2_g=per_row(ln2_g), ln2_b=per_row(ln2_b),
        w_router_t=wr, b_router_col=br,
        w_e_gate=w_e_gate.astype(BF16), w_e_up=w_e_up.astype(BF16), w_e_down=w_e_down.astype(BF16))


def kernel(x_prompt, x_sample, state_C, state_n, state_m, state_pool, w_in, b_gate, hn_gain, w_pool, pool_scale,
           w_proj_a, w_proj_b, w_out, ln1_g, ln1_b, ln2_g, ln2_b, w_router, b_router, w_e_gate, w_e_up, w_e_down):
    w = _prepare_weights(w_in, b_gate, hn_gain, w_pool, pool_scale, w_proj_a, w_proj_b, w_out, ln1_g, ln1_b,
                         ln2_g, ln2_b, w_router, b_router, w_e_gate, w_e_up, w_e_down)
    n_pseq, p_len, _ = x_prompt.shape
    n_sseq, s_len, _ = x_sample.shape
    n_p = n_pseq * p_len
    n_s = n_sseq * s_len
    n = n_p + n_s
    capacity = -(-n // MOE_TILE) * MOE_TILE
    max_tiles = n // MOE_TILE + N_CLASSES
    np_, mp, bp, ns, ms, bs = [], [], [], [], [], []
    c_p = c_s = None
    ys = (x_prompt.reshape(n_p, D_MODEL), x_sample.reshape(n_s, D_MODEL))
    dest = None
    for l in range(DEPTH):
        q, k, v, o, u, ga, gb, gcol, grow, kt, x = _inproj(ys, w, l, n_p, None if l == 0 else dest[0])
        hg_p, c_p, n1p, m1p = _mlstm_prompt(q, k, v, gcol, grow, o, w["gain"], c_p, l, n_pseq, p_len)
        mtok = jnp.pad(jnp.repeat(state_m[l], s_len, axis=0), ((0, 0), (0, LANES - N_HEADS)))
        hg_s, c_s, n1s, m1s = _mlstm_sample(q, k, kt, v, gcol, grow, mtok, o, w["gain"], state_C, state_n, c_s,
                                            l, n_p, n_sseq, s_len)
        pooled_p, tail = _pool_prompt(u, n_pseq, p_len)
        pooled_s, nbuf = _pool_sample(u[n_p:].reshape(n_sseq, s_len, D_POOL), state_pool, l, PAST_LEN)
        dest, counts, xs = _mix(hg_p, hg_s, pooled_p, pooled_s.reshape(n_s, D_POOL), ga, gb, x, w, l, capacity)
        ys = _moe(xs, _tile_tables(counts, capacity, max_tiles), w, l, max_tiles)
        np_.append(n1p[:, :, :, 0])
        mp.append(m1p[:, :, 0])
        bp.append(tail.reshape(n_pseq, -1, D_POOL)[:, -POOL_BUF:])
        ns.append(n1s)
        ms.append(m1s[:, :, 0])
        bs.append(nbuf)
    y_p, y_s = _unpermute(ys, dest[0], n_p)
    st = jnp.stack
    return (y_p.reshape(n_pseq, p_len, D_MODEL), y_s.reshape(n_sseq, s_len, D_MODEL),
            c_p, st(np_), st(mp), st(bp), c_s, st(ns), st(ms), st(bs))
```
